```python
import math
import jax, jax.numpy as jnp
from jax import lax
import numpy as np

D_MODEL = 2048
BATCH = 4
SEQ = 2048
DEPTH = 4

N_MIXERS = 3
D_INNER = D_MODEL
S5_GROUP = 16
S5_STATE = 64
S5_GROUPS = D_INNER // S5_GROUP
FOX_HEAD_DIM = 128
FOX_HEADS = D_INNER // FOX_HEAD_DIM
Q_BLOCK = 128
POOL_WINDOWS = (2, 4, 8, 16)
POOL_GROUPS = len(POOL_WINDOWS)
POOL_GROUP_DIM = D_INNER // POOL_GROUPS
N_S5 = (DEPTH + 2) // 3
N_FOX = (DEPTH + 1) // 3
N_POOL = DEPTH // 3
EPS = 1e-6
DT_MIN = 1e-3
DT_MAX = 1e-1

kernel_name = "hybrid_s5_fox_pool_interleaved"


def rmsnorm(x, w):
    xf = x.astype(jnp.float32)
    y = xf * lax.rsqrt(jnp.mean(xf * xf, axis=-1, keepdims=True) + EPS)
    return (y * w.astype(jnp.float32)).astype(x.dtype)


def _s5_combine(left, right):
    a1r, a1i, b1r, b1i = left
    a2r, a2i, b2r, b2i = right
    ar = a2r * a1r - a2i * a1i
    ai = a2r * a1i + a2i * a1r
    br = a2r * b1r - a2i * b1i + b2r
    bi = a2r * b1i + a2i * b1r + b2i
    return (ar, ai, br, bi)


def s5_mixer(u, a_re, a_im, log_dt, b_re, b_im, c_re, c_im, d_skip, w_glu, b_glu):
    f32 = jnp.float32
    bsz, L, E = u.shape
    ar = a_re.astype(f32)
    ai = a_im.astype(f32)
    dt = jnp.exp(log_dt.astype(f32))[:, None]
    mag = jnp.exp(ar * dt)
    abar_r = mag * jnp.cos(ai * dt)
    abar_i = mag * jnp.sin(ai * dt)
    den = ar * ar + ai * ai
    xr = abar_r - 1.0
    fr = (xr * ar + abar_i * ai) / den
    fi = (abar_i * ar - xr * ai) / den
    br = b_re.astype(f32)
    bi = b_im.astype(f32)
    bbar_r = fr[..., None] * br - fi[..., None] * bi
    bbar_i = fr[..., None] * bi + fi[..., None] * br
    ug = u.astype(f32).reshape(bsz, L, S5_GROUPS, S5_GROUP)
    bu_r = jnp.einsum('blgc,gpc->blgp', ug, bbar_r)
    bu_i = jnp.einsum('blgc,gpc->blgp', ug, bbar_i)
    a_r_el = jnp.broadcast_to(abar_r[None, None], (1, L, S5_GROUPS, S5_STATE))
    a_i_el = jnp.broadcast_to(abar_i[None, None], (1, L, S5_GROUPS, S5_STATE))
    _, _, h_r, h_i = lax.associative_scan(_s5_combine, (a_r_el, a_i_el, bu_r, bu_i), axis=1)
    y = (jnp.einsum('blgp,gcp->blgc', h_r, c_re.astype(f32))
         - jnp.einsum('blgp,gcp->blgc', h_i, c_im.astype(f32)))
    y = y.reshape(bsz, L, E) + d_skip.astype(f32) * u.astype(f32)
    g = jax.nn.gelu(y)
    y = g * jax.nn.sigmoid(g @ w_glu.astype(f32) + b_glu.astype(f32))
    return y.astype(u.dtype)


def fox_mixer(q, k, v, f_logit, q_norm_w, k_norm_w):
    f32 = jnp.float32
    bsz, L, H, Dh = q.shape
    q = rmsnorm(q, q_norm_w)
    k = rmsnorm(k, k_norm_w)
    cum = jnp.cumsum(jax.nn.log_sigmoid(f_logit.astype(f32)), axis=1)
    cum_k = cum.transpose(0, 2, 1)[:, :, None, :]
    scale = Dh ** -0.5
    nb = L // Q_BLOCK
    qb = q.reshape(bsz, nb, Q_BLOCK, H, Dh).transpose(1, 0, 2, 3, 4)
    cb = cum.reshape(bsz, nb, Q_BLOCK, H).transpose(1, 0, 2, 3)
    kpos = jnp.arange(L)

    def block(args):
        i, q_i, c_i = args
        s = jnp.einsum('bqhd,bkhd->bhqk', q_i, k).astype(f32) * scale
        s = s + c_i.transpose(0, 2, 1)[..., None] - cum_k
        qpos = i * Q_BLOCK + jnp.arange(Q_BLOCK)
        s = jnp.where(kpos[None, :] <= qpos[:, None], s, -jnp.inf)
        p = jax.nn.softmax(s, axis=-1)
        return jnp.einsum('bhqk,bkhd->bqhd', p.astype(v.dtype), v)

    out = lax.map(block, (jnp.arange(nb), qb, cb))
    return out.transpose(1, 0, 2, 3, 4).reshape(bsz, L, H * Dh)


def pool_mixer(u, w_group, layer_scale):
    f32 = jnp.float32
    bsz, L, E = u.shape
    uf = u.astype(f32)
    cs = jnp.concatenate([jnp.zeros((bsz, 1, E), f32), jnp.cumsum(uf, axis=1)], axis=1)
    t = jnp.arange(L)
    pooled = []
    for g, w in enumerate(POOL_WINDOWS):
        sl = slice(g * POOL_GROUP_DIM, (g + 1) * POOL_GROUP_DIM)
        lo = jnp.maximum(t + 1 - w, 0)
        s = cs[:, 1:, sl] - cs[:, lo, sl]
        cnt = jnp.minimum(t + 1, w).astype(f32)
        pooled.append(s / cnt[None, :, None])
    pooled = jnp.stack(pooled, axis=2)
    ug = uf.reshape(bsz, L, POOL_GROUPS, POOL_GROUP_DIM)
    mixed = jnp.einsum('blgc,gcd->blgd', pooled - ug, w_group.astype(f32))
    return (mixed.reshape(bsz, L, E) * layer_scale.astype(f32)).astype(u.dtype)


def setup_inputs(seed: int = 0) -> dict:
    key = jax.random.key(seed)
    ks = jax.random.split(key, 24)
    f32 = jnp.float32
    nrm = lambda k, shape, s: jax.random.normal(k, shape, f32) * s
    E, G, P, C = D_INNER, S5_GROUPS, S5_STATE, S5_GROUP
    x = nrm(ks[0], (BATCH, SEQ, D_MODEL), 1.0)
    norm_w = 1.0 + nrm(ks[1], (DEPTH, D_MODEL), 0.02)
    out_proj = nrm(ks[2], (DEPTH, E, D_MODEL), E ** -0.5 / math.sqrt(DEPTH))
    s5_in_proj = nrm(ks[3], (N_S5, D_MODEL, 2 * E), D_MODEL ** -0.5)
    n_idx = jnp.arange(P, dtype=f32)
    s5_a_re = -0.5 + nrm(ks[4], (N_S5, G, P), 0.01)
    s5_a_im = math.pi * n_idx[None, None, :] + nrm(ks[5], (N_S5, G, P), 0.01)
    s5_log_dt = jax.random.uniform(ks[6], (N_S5, G), f32, math.log(DT_MIN), math.log(DT_MAX))
    s5_b_re = nrm(ks[7], (N_S5, G, P, C), (2 * C) ** -0.5)
    s5_b_im = nrm(ks[8], (N_S5, G, P, C), (2 * C) ** -0.5)
    s5_c_re = nrm(ks[9], (N_S5, G, C, P), (2 * P) ** -0.5 * 2.0)
    s5_c_im = nrm(ks[10], (N_S5, G, C, P), (2 * P) ** -0.5 * 2.0)
    s5_d = nrm(ks[11], (N_S5, E), 1.0)
    s5_w_glu = nrm(ks[12], (N_S5, E, E), E ** -0.5)
    s5_b_glu = nrm(ks[13], (N_S5, E), 0.01)
    fox_in_proj = nrm(ks[14], (N_FOX, D_MODEL, 4 * E + FOX_HEADS), D_MODEL ** -0.5)
    fox_q_norm = 1.0 + nrm(ks[15], (N_FOX, FOX_HEAD_DIM), 0.02)
    fox_k_norm = 1.0 + nrm(ks[16], (N_FOX, FOX_HEAD_DIM), 0.02)
    fox_f_bias = jax.random.uniform(ks[17], (N_FOX, FOX_HEADS), f32, 1.0, 4.0)
    pool_in_proj = nrm(ks[18], (N_POOL, D_MODEL, 2 * E), D_MODEL ** -0.5)
    pool_w_group = nrm(ks[19], (N_POOL, POOL_GROUPS, POOL_GROUP_DIM, POOL_GROUP_DIM), POOL_GROUP_DIM ** -0.5)
    pool_scale = 1.0 + nrm(ks[20], (N_POOL, E), 0.1)
    return {"x": x, "norm_w": norm_w, "out_proj": out_proj,
            "s5_in_proj": s5_in_proj, "s5_a_re": s5_a_re, "s5_a_im": s5_a_im,
            "s5_log_dt": s5_log_dt, "s5_b_re": s5_b_re, "s5_b_im": s5_b_im,
            "s5_c_re": s5_c_re, "s5_c_im": s5_c_im, "s5_d": s5_d,
            "s5_w_glu": s5_w_glu, "s5_b_glu": s5_b_glu,
            "fox_in_proj": fox_in_proj, "fox_q_norm": fox_q_norm,
            "fox_k_norm": fox_k_norm, "fox_f_bias": fox_f_bias,
            "pool_in_proj": pool_in_proj, "pool_w_group": pool_w_group,
            "pool_scale": pool_scale}


def reference(x, norm_w, out_proj, s5_in_proj, s5_a_re, s5_a_im, s5_log_dt, s5_b_re, s5_b_im,
              s5_c_re, s5_c_im, s5_d, s5_w_glu, s5_b_glu, fox_in_proj, fox_q_norm, fox_k_norm,
              fox_f_bias, pool_in_proj, pool_w_group, pool_scale):
    E = D_INNER
    bsz, L, _ = x.shape
    h = x
    for i in range(DEPTH):
        kind = i % N_MIXERS
        j = i // N_MIXERS
        xn = rmsnorm(h, norm_w[i])
        if kind == 0:
            proj = xn @ s5_in_proj[j]
            u, z = proj[..., :E], proj[..., E:]
            y = s5_mixer(u, s5_a_re[j], s5_a_im[j], s5_log_dt[j], s5_b_re[j], s5_b_im[j],
                         s5_c_re[j], s5_c_im[j], s5_d[j], s5_w_glu[j], s5_b_glu[j])
        elif kind == 1:
            proj = xn @ fox_in_proj[j]
            hs = (bsz, L, FOX_HEADS, FOX_HEAD_DIM)
            q = proj[..., :E].reshape(hs)
            k = proj[..., E:2 * E].reshape(hs)
            v = proj[..., 2 * E:3 * E].reshape(hs)
            z = proj[..., 3 * E:4 * E]
            f_logit = proj[..., 4 * E:] + fox_f_bias[j]
            y = fox_mixer(q, k, v, f_logit, fox_q_norm[j], fox_k_norm[j])
        else:
            proj = xn @ pool_in_proj[j]
            u, z = proj[..., :E], proj[..., E:]
            y = pool_mixer(u, pool_w_group[j], pool_scale[j])
        h = h + (y * jax.nn.silu(z)) @ out_proj[i]
    return h
```

```python
import functools
import math

import jax
import jax.numpy as jnp
from jax import lax
from jax.experimental import pallas as pl
from jax.experimental.pallas import tpu as pltpu

F32 = jnp.float32
BF16 = jnp.bfloat16

LANES = 128
S5_GROUP = 16
S5_STATE = 64
TILE_GROUPS = LANES // S5_GROUP
TILE_STATES = TILE_GROUPS * S5_STATE
CHUNK = 8
FOX_HEAD_DIM = 128
POOL_WINDOWS = (2, 4, 8, 16)
POOL_HALO = 16
EPS = 1e-6
NEG_BIG = -1e30
VMEM_LIMIT = 56 * 1024 * 1024


def _params(semantics):
    return pltpu.CompilerParams(dimension_semantics=semantics, vmem_limit_bytes=VMEM_LIMIT)


def _rms_rows(x, w):
    ms = jnp.mean(x * x, axis=-1, keepdims=True)
    return x * lax.rsqrt(ms + EPS) * w


def _silu(z):
    return z * jax.nn.sigmoid(z)


def _norm_proj_body(x_ref, nw_ref, w_ref, o_ref, xn_ref, *, n_axis, row_chunk):
    @pl.when(pl.program_id(n_axis) == 0)
    def _():
        def chunk(c, carry):
            rows = pl.ds(pl.multiple_of(c * row_chunk, row_chunk), row_chunk)
            xn_ref[rows, :] = _rms_rows(x_ref[rows, :], nw_ref[...]).astype(BF16)
            return carry
        lax.fori_loop(0, x_ref.shape[0] // row_chunk, chunk, 0)

    o_ref[...] = jnp.dot(xn_ref[...], w_ref[...].astype(BF16), preferred_element_type=F32)


def _norm_proj_extra_body(x_ref, nw_ref, w_ref, wf_ref, bf_ref, o_ref, of_ref, xn_ref, *, row_chunk):
    @pl.when(pl.program_id(1) == 0)
    def _():
        def chunk(c, carry):
            rows = pl.ds(pl.multiple_of(c * row_chunk, row_chunk), row_chunk)
            xn_ref[rows, :] = _rms_rows(x_ref[rows, :], nw_ref[...]).astype(BF16)
            return carry
        lax.fori_loop(0, x_ref.shape[0] // row_chunk, chunk, 0)
        of_ref[...] = jnp.dot(xn_ref[...], wf_ref[...].astype(BF16),
                              preferred_element_type=F32) + bf_ref[...]

    o_ref[...] = jnp.dot(xn_ref[...], w_ref[...].astype(BF16), preferred_element_type=F32)


def _norm_proj(h2d, norm_w, w, *, tm, tn):
    m, d = h2d.shape
    n = w.shape[1]
    return pl.pallas_call(
        functools.partial(_norm_proj_body, n_axis=1, row_chunk=min(tm, 128)),
        grid=(m // tm, n // tn),
        in_specs=[pl.BlockSpec((tm, d), lambda i, j: (i, 0)),
                  pl.BlockSpec((1, d), lambda i, j: (0, 0)),
                  pl.BlockSpec((d, tn), lambda i, j: (0, j))],
        out_specs=pl.BlockSpec((tm, tn), lambda i, j: (i, j)),
        out_shape=jax.ShapeDtypeStruct((m, n), F32),
        scratch_shapes=[pltpu.VMEM((tm, d), BF16)],
        compiler_params=_params(("parallel", "arbitrary")),
        name="norm_proj",
    )(h2d, norm_w.reshape(1, d), w)


def _norm_proj_extra(h2d, norm_w, w, w_extra, b_extra, *, tm, tn):
    m, d = h2d.shape
    n_main = (w.shape[1] // tn) * tn
    return pl.pallas_call(
        functools.partial(_norm_proj_extra_body, row_chunk=min(tm, 128)),
        grid=(m // tm, n_main // tn),
        in_specs=[pl.BlockSpec((tm, d), lambda i, j: (i, 0)),
                  pl.BlockSpec((1, d), lambda i, j: (0, 0)),
                  pl.BlockSpec((d, tn), lambda i, j: (0, j)),
                  pl.BlockSpec((d, LANES), lambda i, j: (0, 0)),
                  pl.BlockSpec((1, LANES), lambda i, j: (0, 0))],
        out_specs=[pl.BlockSpec((tm, tn), lambda i, j: (i, j)),
                   pl.BlockSpec((tm, LANES), lambda i, j: (i, 0))],
        out_shape=[jax.ShapeDtypeStruct((m, n_main), F32),
                   jax.ShapeDtypeStruct((m, LANES), F32)],
        scratch_shapes=[pltpu.VMEM((tm, d), BF16)],
        compiler_params=_params(("parallel", "arbitrary")),
        name="norm_proj_extra",
    )(h2d, norm_w.reshape(1, d), w, w_extra, b_extra)


def _norm_proj_tmajor(h2d, norm_w, w, *, chunk, rm, tn):
    m, d = h2d.shape
    n = w.shape[1]
    r = m // chunk
    hv = h2d.reshape(r, chunk * d)
    rb = r // rm
    return pl.pallas_call(
        functools.partial(_norm_proj_body, n_axis=2, row_chunk=min(rm, 128)),
        grid=(chunk, rb, n // tn),
        in_specs=[pl.BlockSpec((rm, d), lambda t, i, j: (i, t)),
                  pl.BlockSpec((1, d), lambda t, i, j: (0, 0)),
                  pl.BlockSpec((d, tn), lambda t, i, j: (0, j))],
        out_specs=pl.BlockSpec((rm, tn), lambda t, i, j: (t * rb + i, j)),
        out_shape=jax.ShapeDtypeStruct((m, n), F32),
        scratch_shapes=[pltpu.VMEM((rm, d), BF16)],
        compiler_params=_params(("parallel", "parallel", "arbitrary")),
        name="norm_proj_tmajor",
    )(hv, norm_w.reshape(1, d), w)


def _out_proj_body(y_ref, z_ref, h_ref, w_ref, o_ref):
    a = (y_ref[...] * _silu(z_ref[...])).astype(BF16)
    o_ref[...] = h_ref[...] + jnp.dot(a, w_ref[...], preferred_element_type=F32)


def _out_proj(y2d, proj2d, z_block, h2d, w_bf16, *, tm):
    m, e = y2d.shape
    d = w_bf16.shape[1]
    return pl.pallas_call(
        _out_proj_body,
        grid=(m // tm,),
        in_specs=[pl.BlockSpec((tm, e), lambda i: (i, 0)),
                  pl.BlockSpec((tm, e), lambda i: (i, z_block)),
                  pl.BlockSpec((tm, d), lambda i: (i, 0)),
                  pl.BlockSpec((e, d), lambda i: (0, 0), pipeline_mode=pl.Buffered(1))],
        out_specs=pl.BlockSpec((tm, d), lambda i: (i, 0)),
        out_shape=jax.ShapeDtypeStruct((m, d), F32),
        compiler_params=_params(("parallel",)),
        name="out_proj",
    )(y2d, proj2d, h2d, w_bf16)


def _out_proj_tmajor(y_tm, proj_tm, h2d, w_bf16, *, chunk, rm):
    m, e = y_tm.shape
    d = w_bf16.shape[1]
    r = m // chunk
    rb = r // rm
    hv = h2d.reshape(r, chunk * d)
    out = pl.pallas_call(
        _out_proj_body,
        grid=(chunk, rb),
        in_specs=[pl.BlockSpec((rm, e), lambda t, i: (t * rb + i, 0)),
                  pl.BlockSpec((rm, e), lambda t, i: (t * rb + i, 1)),
                  pl.BlockSpec((rm, d), lambda t, i: (i, t)),
                  pl.BlockSpec((e, d), lambda t, i: (0, 0), pipeline_mode=pl.Buffered(1))],
        out_specs=pl.BlockSpec((rm, d), lambda t, i: (i, t)),
        out_shape=jax.ShapeDtypeStruct((r, chunk * d), F32),
        compiler_params=_params(("parallel", "parallel")),
        name="out_proj_tmajor",
    )(y_tm, proj_tm, hv, w_bf16)
    return out.reshape(m, d)


def _s5_prep_body(arow_ref, acol_ref, bt_ref, ct_ref, cn_ref,
                  wb_ref, wc_ref, toe_ref, lam_ref, *, chunk):
    ns = TILE_STATES

    def discretise(ar, ai, log_dt, k):
        dt = jnp.exp(log_dt)
        mag = jnp.exp(ar * dt * k)
        ang = ai * dt * k
        return mag * jnp.cos(ang), mag * jnp.sin(ang)

    ar, ai, ldt = arow_ref[0:1, :], arow_ref[1:2, :], arow_ref[2:3, :]
    k_rows = lax.broadcasted_iota(jnp.int32, (chunk + 1, 1), 0).astype(F32)
    pr, pi = discretise(ar, ai, ldt, k_rows)
    abar_r, abar_i = pr[1:2, :], pi[1:2, :]
    den = ar * ar + ai * ai
    xr = abar_r - 1.0
    fr = (xr * ar + abar_i * ai) / den
    fi = (abar_i * ar - xr * ai) / den
    bt_re, bt_im = bt_ref[0], bt_ref[1]
    bbar_re = fr * bt_re - fi * bt_im
    bbar_im = fr * bt_im + fi * bt_re
    lam_ref[:, 0:ns] = pr[chunk:chunk + 1, :]
    lam_ref[:, ns:2 * ns] = pi[chunk:chunk + 1, :]

    ct_cat = ct_ref[...]
    zero_blk = jnp.zeros((LANES, LANES), F32)
    k_prev = zero_blk
    for tau in range(chunk):
        qr, qi = pr[tau:tau + 1, :], pi[tau:tau + 1, :]
        g_re = bbar_re * qr - bbar_im * qi
        g_im = bbar_re * qi + bbar_im * qr
        t = chunk - 1 - tau
        wb_ref[t * LANES:(t + 1) * LANES, 0:ns] = g_re.astype(BF16)
        wb_ref[t * LANES:(t + 1) * LANES, ns:2 * ns] = g_im.astype(BF16)
        g_cat = jnp.concatenate([g_re, -g_im], axis=1)
        k_tau = jnp.dot(g_cat, ct_cat, preferred_element_type=F32,
                        precision=lax.Precision.HIGHEST)
        toe_ref[0:LANES, tau * LANES:(tau + 1) * LANES] = k_tau.astype(BF16)
        toe_ref[LANES:2 * LANES, tau * LANES:(tau + 1) * LANES] = k_prev.astype(BF16)
        k_prev = k_tau

    arc, aic, ldtc = acol_ref[:, 0:1], acol_ref[:, 1:2], acol_ref[:, 2:3]
    k_cols = (lax.broadcasted_iota(jnp.int32, (1, chunk), 1) + 1).astype(F32)
    qcr, qci = discretise(arc, aic, ldtc, k_cols)
    cn_re, cn_im = cn_ref[0], cn_ref[1]
    for t in range(chunk):
        qr, qi = qcr[:, t:t + 1], qci[:, t:t + 1]
        wc_ref[0:ns, t * LANES:(t + 1) * LANES] = (cn_re * qr - cn_im * qi).astype(BF16)
        wc_ref[ns:2 * ns, t * LANES:(t + 1) * LANES] = (-(cn_re * qi + cn_im * qr)).astype(BF16)


def _block_diag_tiles(w, n_tiles):
    g, a, p = w.shape
    w4 = w.reshape(n_tiles, TILE_GROUPS, a, p)
    eye = jnp.eye(TILE_GROUPS, dtype=w.dtype)
    bd = w4[:, :, :, None, :] * eye[None, :, None, :, None]
    return bd.reshape(n_tiles, TILE_GROUPS * a, TILE_GROUPS * p)


def _s5_prep(a_re, a_im, log_dt, b_re, b_im, c_re, c_im, *, chunk):
    g, p = a_re.shape
    nt = g // TILE_GROUPS
    ns = TILE_STATES
    ldt = jnp.broadcast_to(log_dt[:, None], (g, p))
    a_rows = jnp.stack([a_re, a_im, ldt], axis=0).reshape(3, nt, ns).transpose(1, 0, 2)
    a_cols = a_rows.transpose(0, 2, 1)
    bt = jnp.stack([_block_diag_tiles(b_re.transpose(0, 2, 1), nt),
                    _block_diag_tiles(b_im.transpose(0, 2, 1), nt)], axis=1)
    c_bd_re = _block_diag_tiles(c_re, nt)
    c_bd_im = _block_diag_tiles(c_im, nt)
    cn = jnp.stack([c_bd_re.transpose(0, 2, 1), c_bd_im.transpose(0, 2, 1)], axis=1)
    ct_cat = jnp.concatenate([cn[:, 0], cn[:, 1]], axis=1)
    kw = chunk * LANES
    return pl.pallas_call(
        functools.partial(_s5_prep_body, chunk=chunk),
        grid=(nt,),
        in_specs=[pl.BlockSpec((None, 3, ns), lambda j: (j, 0, 0)),
                  pl.BlockSpec((None, ns, 3), lambda j: (j, 0, 0)),
                  pl.BlockSpec((None, 2, LANES, ns), lambda j: (j, 0, 0, 0)),
                  pl.BlockSpec((None, 2 * ns, LANES), lambda j: (j, 0, 0)),
                  pl.BlockSpec((None, 2, ns, LANES), lambda j: (j, 0, 0, 0))],
        out_specs=[pl.BlockSpec((None, kw, 2 * ns), lambda j: (j, 0, 0)),
                   pl.BlockSpec((None, 2 * ns, kw), lambda j: (j, 0, 0)),
                   pl.BlockSpec((None, 2 * LANES, kw), lambda j: (j, 0, 0)),
                   pl.BlockSpec((None, 1, 2 * ns), lambda j: (j, 0, 0))],
        out_shape=[jax.ShapeDtypeStruct((nt, kw, 2 * ns), BF16),
                   jax.ShapeDtypeStruct((nt, 2 * ns, kw), BF16),
                   jax.ShapeDtypeStruct((nt, 2 * LANES, kw), BF16),
                   jax.ShapeDtypeStruct((nt, 1, 2 * ns), F32)],
        compiler_params=_params(("parallel",)),
        name="s5_prep",
    )(a_rows, a_cols, bt, ct_cat, cn)


def _s5_core_body(u_ref, d_ref, wb_ref, wc_ref, toe_ref, lam_ref, o_ref,
                  ub_ref, hin_ref, hprev_ref, y_ref, *, chunk, n_seq, n_chunks):
    ns = TILE_STATES
    for t in range(chunk):
        ub_ref[:, t * LANES:(t + 1) * LANES] = u_ref[t].astype(BF16)
    nk = ns // LANES
    h_in = jnp.dot(ub_ref[...], wb_ref[...], preferred_element_type=F32)
    for k in range(2 * nk):
        hin_ref[k] = h_in[:, k * LANES:(k + 1) * LANES]

    lam = [jnp.broadcast_to(lam_ref[:, k * LANES:(k + 1) * LANES], (n_seq, LANES)) for k in range(2 * nk)]
    state = [jnp.zeros((n_seq, LANES), F32) for _ in range(2 * nk)]
    for c in range(n_chunks):
        rows = pl.ds(c, n_seq, stride=n_chunks)
        new_state = []
        for k in range(nk):
            hr, hi = state[k], state[nk + k]
            hprev_ref[k, rows, :] = hr
            hprev_ref[nk + k, rows, :] = hi
            lr, li = lam[k], lam[nk + k]
            new_state.append((lr * hr - li * hi + hin_ref[k, rows, :],
                              lr * hi + li * hr + hin_ref[nk + k, rows, :]))
        state = [s[0] for s in new_state] + [s[1] for s in new_state]

    h_prev = jnp.concatenate([hprev_ref[k].astype(BF16) for k in range(2 * nk)], axis=1)
    y_ref[...] = jnp.dot(h_prev, wc_ref[...], preferred_element_type=F32)
    for a in range(chunk // 2):
        lo = 2 * a * LANES
        width = (chunk - 2 * a) * LANES
        y_ref[:, lo:] += jnp.dot(ub_ref[:, lo:lo + 2 * LANES], toe_ref[:, 0:width],
                                 preferred_element_type=F32)
    for t in range(chunk):
        y = y_ref[:, t * LANES:(t + 1) * LANES] + d_ref[...] * u_ref[t]
        o_ref[t] = jax.nn.gelu(y)


def _s5_core(proj_tm, d_skip, wb, wc, toe, lam, *, chunk, n_seq):
    m, n2 = proj_tm.shape
    e = n2 // 2
    r = m // chunk
    nt = e // LANES
    ns2 = 2 * TILE_STATES
    kw = chunk * LANES
    pv = proj_tm.reshape(chunk, r, n2)
    out = pl.pallas_call(
        functools.partial(_s5_core_body, chunk=chunk, n_seq=n_seq, n_chunks=r // n_seq),
        grid=(nt,),
        in_specs=[pl.BlockSpec((chunk, r, LANES), lambda j: (0, 0, j)),
                  pl.BlockSpec((1, LANES), lambda j: (0, j)),
                  pl.BlockSpec((None, kw, ns2), lambda j: (j, 0, 0)),
                  pl.BlockSpec((None, ns2, kw), lambda j: (j, 0, 0)),
                  pl.BlockSpec((None, 2 * LANES, kw), lambda j: (j, 0, 0)),
                  pl.BlockSpec((None, 1, ns2), lambda j: (j, 0, 0))],
        out_specs=pl.BlockSpec((chunk, r, LANES), lambda j: (0, 0, j)),
        out_shape=jax.ShapeDtypeStruct((chunk, r, e), F32),
        scratch_shapes=[pltpu.VMEM((r, kw), BF16),
                        pltpu.VMEM((ns2 // LANES, r, LANES), F32),
                        pltpu.VMEM((ns2 // LANES, r, LANES), F32),
                        pltpu.VMEM((r, kw), F32)],
        compiler_params=_params(("parallel",)),
        name="s5_core",
    )(pv, d_skip.reshape(1, e), wb, wc, toe, lam)
    return out.reshape(m, e)


def _glu_body(g_ref, w_ref, b_ref, o_ref):
    g = g_ref[...]
    acc = jnp.dot(g.astype(BF16), w_ref[...], preferred_element_type=F32) + b_ref[...]
    o_ref[...] = g * jax.nn.sigmoid(acc)


def _glu(g2d, w_bf16, b, *, tm):
    m, e = g2d.shape
    return pl.pallas_call(
        _glu_body,
        grid=(m // tm,),
        in_specs=[pl.BlockSpec((tm, e), lambda i: (i, 0)),
                  pl.BlockSpec((e, e), lambda i: (0, 0), pipeline_mode=pl.Buffered(1)),
                  pl.BlockSpec((1, e), lambda i: (0, 0))],
        out_specs=pl.BlockSpec((tm, e), lambda i: (i, 0)),
        out_shape=jax.ShapeDtypeStruct((m, e), F32),
        compiler_params=_params(("parallel",)),
        name="s5_glu",
    )(g2d, w_bf16, b.reshape(1, e))


def _fox_cum_body(f_ref, o_ref, *, blk):
    seq = f_ref.shape[0]
    row = lax.broadcasted_iota(jnp.int32, (blk, blk), 0)
    col = lax.broadcasted_iota(jnp.int32, (blk, blk), 1)
    tri = (col <= row).astype(F32)
    carry = jnp.zeros((1, LANES), F32)
    for i in range(seq // blk):
        x = f_ref[i * blk:(i + 1) * blk, :]
        ls = jnp.minimum(x, 0.0) - jnp.log1p(jnp.exp(-jnp.abs(x)))
        cum = jnp.dot(tri, ls, preferred_element_type=F32,
                      precision=lax.Precision.HIGHEST) + carry
        o_ref[i * blk:(i + 1) * blk, :] = cum
        carry = cum[blk - 1:blk, :]


def _fox_cum(f3d, *, blk):
    b, seq, _ = f3d.shape
    return pl.pallas_call(
        functools.partial(_fox_cum_body, blk=blk),
        grid=(b,),
        in_specs=[pl.BlockSpec((None, seq, LANES), lambda i: (i, 0, 0))],
        out_specs=pl.BlockSpec((None, seq, LANES), lambda i: (i, 0, 0)),
        out_shape=jax.ShapeDtypeStruct((b, seq, LANES), F32),
        compiler_params=_params(("parallel",)),
        name="fox_cum",
    )(f3d)


def _fox_attn_body(q_ref, k_ref, v_ref, cq_ref, ck_ref, qw_ref, kw_ref, o_ref,
                   kn_ref, vb_ref, m_ref, l_ref, acc_ref, *, tq, scale, row_chunk):
    head = pl.program_id(1)
    qi = pl.program_id(2)
    seq = k_ref.shape[0]

    @pl.when(qi == 0)
    def _():
        def chunk(c, carry):
            rows = pl.ds(pl.multiple_of(c * row_chunk, row_chunk), row_chunk)
            kn_ref[rows, :] = _rms_rows(k_ref[rows, :], kw_ref[...]).astype(BF16)
            vb_ref[rows, :] = v_ref[rows, :].astype(BF16)
            return carry
        lax.fori_loop(0, seq // row_chunk, chunk, 0)

    qn = (_rms_rows(q_ref[...], qw_ref[...]) * scale).astype(BF16)
    lane = lax.broadcasted_iota(jnp.int32, (tq, LANES), 1)
    cq = jnp.sum(jnp.where(lane == head, cq_ref[...], 0.0), axis=-1, keepdims=True)

    m_ref[...] = jnp.full((tq, 1), NEG_BIG, F32)
    l_ref[...] = jnp.zeros((tq, 1), F32)
    acc_ref[...] = jnp.zeros((tq, FOX_HEAD_DIM), F32)

    def step(kb, masked):
        rows = pl.ds(pl.multiple_of(kb * tq, tq), tq)
        s = lax.dot_general(qn, kn_ref[rows, :], (((1,), (1,)), ((), ())),
                            preferred_element_type=F32)
        s = s + (cq - ck_ref[pl.ds(kb, 1), :])
        if masked:
            r_i = lax.broadcasted_iota(jnp.int32, (tq, tq), 0)
            c_i = lax.broadcasted_iota(jnp.int32, (tq, tq), 1)
            s = jnp.where(c_i <= r_i, s, NEG_BIG)
        m_old = m_ref[...]
        m_new = jnp.maximum(m_old, jnp.max(s, axis=-1, keepdims=True))
        p = jnp.exp(s - m_new)
        alpha = jnp.exp(m_old - m_new)
        l_ref[...] = alpha * l_ref[...] + jnp.sum(p, axis=-1, keepdims=True)
        acc_ref[...] = alpha * acc_ref[...] + jnp.dot(p.astype(BF16), vb_ref[rows, :],
                                                      preferred_element_type=F32)
        m_ref[...] = m_new

    def off_diag(kb, carry):
        step(kb, False)
        return carry

    lax.fori_loop(0, qi, off_diag, 0)
    step(qi, True)
    o_ref[...] = acc_ref[...] / l_ref[...]


def _fox_attn(proj3d, cum3d, cum_k, q_w, k_w, *, heads, tq):
    b, seq, _ = proj3d.shape
    dh = FOX_HEAD_DIM
    nq = seq // tq
    return pl.pallas_call(
        functools.partial(_fox_attn_body, tq=tq, scale=dh ** -0.5, row_chunk=min(seq, 256)),
        grid=(b, heads, nq),
        in_specs=[pl.BlockSpec((None, tq, dh), lambda bi, h, i: (bi, i, h)),
                  pl.BlockSpec((None, seq, dh), lambda bi, h, i: (bi, 0, heads + h)),
                  pl.BlockSpec((None, seq, dh), lambda bi, h, i: (bi, 0, 2 * heads + h)),
                  pl.BlockSpec((None, tq, LANES), lambda bi, h, i: (bi, i, 0)),
                  pl.BlockSpec((None, None, nq, tq), lambda bi, h, i: (bi, h, 0, 0)),
                  pl.BlockSpec((1, dh), lambda bi, h, i: (0, 0)),
                  pl.BlockSpec((1, dh), lambda bi, h, i: (0, 0))],
        out_specs=pl.BlockSpec((None, tq, dh), lambda bi, h, i: (bi, i, h)),
        out_shape=jax.ShapeDtypeStruct((b, seq, heads * dh), F32),
        scratch_shapes=[pltpu.VMEM((seq, dh), BF16),
                        pltpu.VMEM((seq, dh), BF16),
                        pltpu.VMEM((tq, 1), F32),
                        pltpu.VMEM((tq, 1), F32),
                        pltpu.VMEM((tq, dh), F32)],
        compiler_params=_params(("parallel", "parallel", "arbitrary")),
        name="fox_attn",
    )(proj3d, proj3d, proj3d, cum3d, cum_k, q_w.reshape(1, dh), k_w.reshape(1, dh))


def _pool_body(u_ref, w_ref, s_ref, o_ref, sa_ref, sb_ref, *, windows):
    seq, cg = u_ref.shape
    halo = POOL_HALO
    grp = pl.program_id(1)
    zeros = jnp.zeros((halo, cg), F32)
    sa_ref[0:halo, :] = zeros
    sb_ref[0:halo, :] = zeros
    t_idx = lax.broadcasted_iota(jnp.int32, (seq, 1), 0)

    for gi, win in enumerate(windows):
        @pl.when(grp == gi)
        def _(win=win):
            u = u_ref[...]
            sa_ref[halo:halo + seq, :] = u
            bufs = (sa_ref, sb_ref)
            s = u
            shift = 1
            level = 0
            while shift < win:
                src = bufs[level % 2]
                s = src[halo:halo + seq, :] + src[halo - shift:halo - shift + seq, :]
                shift *= 2
                level += 1
                if shift < win:
                    bufs[level % 2][halo:halo + seq, :] = s
            cnt = jnp.minimum(t_idx + 1, win).astype(F32)
            diff = (s / cnt - u).astype(BF16)
            mixed = jnp.dot(diff, w_ref[...].astype(BF16), preferred_element_type=F32)
            o_ref[...] = mixed * s_ref[...]


def _pool(proj3d, w_group, scale, *, windows):
    b, seq, n2 = proj3d.shape
    e = n2 // 2
    ng = len(windows)
    cg = e // ng
    return pl.pallas_call(
        functools.partial(_pool_body, windows=windows),
        grid=(b, ng),
        in_specs=[pl.BlockSpec((None, seq, cg), lambda bi, g: (bi, 0, g)),
                  pl.BlockSpec((None, cg, cg), lambda bi, g: (g, 0, 0)),
                  pl.BlockSpec((1, cg), lambda bi, g: (0, g))],
        out_specs=pl.BlockSpec((None, seq, cg), lambda bi, g: (bi, 0, g)),
        out_shape=jax.ShapeDtypeStruct((b, seq, e), F32),
        scratch_shapes=[pltpu.VMEM((POOL_HALO + seq, cg), F32),
                        pltpu.VMEM((POOL_HALO + seq, cg), F32)],
        compiler_params=_params(("parallel", "parallel")),
        name="pool_mix",
    )(proj3d, w_group, scale.reshape(1, e))


def _pick(n, pref):
    t = min(n, pref)
    while n % t:
        t //= 2
    return t


def kernel(x, norm_w, out_proj, s5_in_proj, s5_a_re, s5_a_im, s5_log_dt, s5_b_re, s5_b_im, s5_c_re, s5_c_im, s5_d, s5_w_glu, s5_b_glu, fox_in_proj, fox_q_norm, fox_k_norm, fox_f_bias, pool_in_proj, pool_w_group, pool_scale):
    bsz, seq, d = x.shape
    depth = norm_w.shape[0]
    e = s5_d.shape[1]
    heads = e // FOX_HEAD_DIM
    m = bsz * seq
    n_mixers = 3
    r = m // CHUNK

    tm = _pick(m, 1024)
    rm = _pick(r, 1024)
    tm_out = _pick(m, 512)
    rm_out = _pick(r, 512)
    tq = _pick(seq, 512)

    h = x.reshape(m, d)
    for i in range(depth):
        kind, j = i % n_mixers, i // n_mixers
        w_out = out_proj[i].astype(BF16)
        if kind == 0:
            wb, wc, toe, lam = _s5_prep(s5_a_re[j], s5_a_im[j], s5_log_dt[j], s5_b_re[j], s5_b_im[j],
                                        s5_c_re[j], s5_c_im[j], chunk=CHUNK)
            proj = _norm_proj_tmajor(h, norm_w[i], s5_in_proj[j], chunk=CHUNK, rm=rm, tn=_pick(2 * e, 512))
            g = _s5_core(proj, s5_d[j], wb, wc, toe, lam, chunk=CHUNK, n_seq=bsz)
            y = _glu(g, s5_w_glu[j].astype(BF16), s5_b_glu[j], tm=tm_out)
            h = _out_proj_tmajor(y, proj, h, w_out, chunk=CHUNK, rm=rm_out)
        elif kind == 1:
            w_in = fox_in_proj[j]
            n_main = 4 * e
            w_f = jnp.pad(w_in[:, n_main:], ((0, 0), (0, LANES - heads)))
            b_f = jnp.pad(fox_f_bias[j], (0, LANES - heads)).reshape(1, LANES)
            proj, f_logit = _norm_proj_extra(h, norm_w[i], w_in, w_f, b_f, tm=tm, tn=_pick(n_main, 512))
            cum = _fox_cum(f_logit.reshape(bsz, seq, LANES), blk=_pick(seq, 256))
            cum_k = cum[:, :, :heads].transpose(0, 2, 1).reshape(bsz, heads, seq // tq, tq)
            y = _fox_attn(proj.reshape(bsz, seq, n_main), cum, cum_k, fox_q_norm[j], fox_k_norm[j],
                          heads=heads, tq=tq)
            h = _out_proj(y.reshape(m, e), proj, 3, h, w_out, tm=tm_out)
        else:
            proj = _norm_proj(h, norm_w[i], pool_in_proj[j], tm=tm, tn=_pick(2 * e, 512))
            y = _pool(proj.reshape(bsz, seq, 2 * e), pool_w_group[j], pool_scale[j], windows=POOL_WINDOWS)
            h = _out_proj(y.reshape(m, e), proj, 1, h, w_out, tm=tm_out)
    return h.reshape(bsz, seq, d)
```

```python
import functools
import math

import jax
import jax.numpy as jnp
from jax import lax
from jax.experimental import pallas as pl
from jax.experimental.pallas import tpu as pltpu

F32 = jnp.float32
BF16 = jnp.bfloat16

LANES = 128
S5_GROUP = 16
S5_STATE = 64
TILE_GROUPS = LANES // S5_GROUP
TILE_STATES = TILE_GROUPS * S5_STATE
CHUNK = 8
FOX_HEAD_DIM = 128
POOL_WINDOWS = (2, 4, 8, 16)
POOL_HALO = 16
EPS = 1e-6
NEG_BIG = -1e30
VMEM_LIMIT = 56 * 1024 * 1024


def _params(semantics):
    return pltpu.CompilerParams(dimension_semantics=semantics, vmem_limit_bytes=VMEM_LIMIT)


def _rms_rows(x, w):
    ms = jnp.mean(x * x, axis=-1, keepdims=True)
    return x * lax.rsqrt(ms + EPS) * w


def _silu(z):
    return z * jax.nn.sigmoid(z)


def _norm_proj_body(x_ref, nw_ref, w_ref, o_ref, xn_ref, *, n_axis, row_chunk):
    @pl.when(pl.program_id(n_axis) == 0)
    def _():
        def chunk(c, carry):
            rows = pl.ds(pl.multiple_of(c * row_chunk, row_chunk), row_chunk)
            xn_ref[rows, :] = _rms_rows(x_ref[rows, :], nw_ref[...]).astype(BF16)
            return carry
        lax.fori_loop(0, x_ref.shape[0] // row_chunk, chunk, 0)

    o_ref[...] = jnp.dot(xn_ref[...], w_ref[...].astype(BF16), preferred_element_type=F32)


def _norm_proj_extra_body(x_ref, nw_ref, w_ref, wf_ref, bf_ref, o_ref, of_ref, xn_ref, *, row_chunk):
    @pl.when(pl.program_id(1) == 0)
    def _():
        def chunk(c, carry):
            rows = pl.ds(pl.multiple_of(c * row_chunk, row_chunk), row_chunk)
            xn_ref[rows, :] = _rms_rows(x_ref[rows, :], nw_ref[...]).astype(BF16)
            return carry
        lax.fori_loop(0, x_ref.shape[0] // row_chunk, chunk, 0)
        of_ref[...] = jnp.dot(xn_ref[...], wf_ref[...].astype(BF16),
                              preferred_element_type=F32) + bf_ref[...]

    o_ref[...] = jnp.dot(xn_ref[...], w_ref[...].astype(BF16), preferred_element_type=F32)


def _norm_proj(h2d, norm_w, w, *, tm, tn):
    m, d = h2d.shape
    n = w.shape[1]
    return pl.pallas_call(
        functools.partial(_norm_proj_body, n_axis=1, row_chunk=min(tm, 128)),
        grid=(m // tm, n // tn),
        in_specs=[pl.BlockSpec((tm, d), lambda i, j: (i, 0)),
                  pl.BlockSpec((1, d), lambda i, j: (0, 0)),
                  pl.BlockSpec((d, tn), lambda i, j: (0, j))],
        out_specs=pl.BlockSpec((tm, tn), lambda i, j: (i, j)),
        out_shape=jax.ShapeDtypeStruct((m, n), F32),
        scratch_shapes=[pltpu.VMEM((tm, d), BF16)],
        compiler_params=_params(("parallel", "arbitrary")),
        name="norm_proj",
    )(h2d, norm_w.reshape(1, d), w)


def _norm_proj_extra(h2d, norm_w, w, w_extra, b_extra, *, tm, tn):
    m, d = h2d.shape
    n_main = (w.shape[1] // tn) * tn
    return pl.pallas_call(
        functools.partial(_norm_proj_extra_body, row_chunk=min(tm, 128)),
        grid=(m // tm, n_main // tn),
        in_specs=[pl.BlockSpec((tm, d), lambda i, j: (i, 0)),
                  pl.BlockSpec((1, d), lambda i, j: (0, 0)),
                  pl.BlockSpec((d, tn), lambda i, j: (0, j)),
                  pl.BlockSpec((d, LANES), lambda i, j: (0, 0)),
                  pl.BlockSpec((1, LANES), lambda i, j: (0, 0))],
        out_specs=[pl.BlockSpec((tm, tn), lambda i, j: (i, j)),
                   pl.BlockSpec((tm, LANES), lambda i, j: (i, 0))],
        out_shape=[jax.ShapeDtypeStruct((m, n_main), F32),
                   jax.ShapeDtypeStruct((m, LANES), F32)],
        scratch_shapes=[pltpu.VMEM((tm, d), BF16)],
        compiler_params=_params(("parallel", "arbitrary")),
        name="norm_proj_extra",
    )(h2d, norm_w.reshape(1, d), w, w_extra, b_extra)


def _out_proj_body(y_ref, z_ref, h_ref, w_ref, o_ref):
    a = (y_ref[...] * _silu(z_ref[...])).astype(BF16)
    o_ref[...] = h_ref[...] + jnp.dot(a, w_ref[...], preferred_element_type=F32)


def _out_proj(y2d, proj2d, z_block, h2d, w_bf16, *, tm):
    m, e = y2d.shape
    d = w_bf16.shape[1]
    return pl.pallas_call(
        _out_proj_body,
        grid=(m // tm,),
        in_specs=[pl.BlockSpec((tm, e), lambda i: (i, 0)),
                  pl.BlockSpec((tm, e), lambda i: (i, z_block)),
                  pl.BlockSpec((tm, d), lambda i: (i, 0)),
                  pl.BlockSpec((e, d), lambda i: (0, 0), pipeline_mode=pl.Buffered(1))],
        out_specs=pl.BlockSpec((tm, d), lambda i: (i, 0)),
        out_shape=jax.ShapeDtypeStruct((m, d), F32),
        compiler_params=_params(("parallel",)),
        name="out_proj",
    )(y2d, proj2d, h2d, w_bf16)


def _s5_prep_body(a_ref, bt_ref, c_ref, wb_ref, wct_ref, toe_ref, lam_ref, *, chunk):
    ns = TILE_STATES
    n_lt = ns // LANES

    def block_diag(x):
        x2 = jnp.concatenate([x, x], axis=1)
        grp_row = lax.broadcasted_iota(jnp.int32, (LANES, LANES), 0) // S5_GROUP
        grp_lane = lax.broadcasted_iota(jnp.int32, (LANES, LANES), 1) // S5_STATE
        per_tile = LANES // S5_STATE
        return jnp.concatenate([jnp.where(grp_row == grp_lane + per_tile * t, x2, 0.0)
                                for t in range(n_lt)], axis=1)

    ar, ai, ldt = a_ref[0:1, :], a_ref[1:2, :], a_ref[2:3, :]
    dt = jnp.exp(ldt)
    k_rows = lax.broadcasted_iota(jnp.int32, (chunk + 1, 1), 0).astype(F32)
    mag = jnp.exp(ar * dt * k_rows)
    ang = ai * dt * k_rows
    pr, pi = mag * jnp.cos(ang), mag * jnp.sin(ang)
    abar_r, abar_i = pr[1:2, :], pi[1:2, :]
    den = ar * ar + ai * ai
    xr = abar_r - 1.0
    fr = (xr * ar + abar_i * ai) / den
    fi = (abar_i * ar - xr * ai) / den
    bt_re, bt_im = block_diag(bt_ref[0]), block_diag(bt_ref[1])
    bbar_re = fr * bt_re - fi * bt_im
    bbar_im = fr * bt_im + fi * bt_re
    c_re, c_im = block_diag(c_ref[0]), block_diag(c_ref[1])
    lam_ref[:, 0:ns] = pr[chunk:chunk + 1, :]
    lam_ref[:, ns:2 * ns] = pi[chunk:chunk + 1, :]

    for tau in range(chunk):
        qr, qi = pr[tau:tau + 1, :], pi[tau:tau + 1, :]
        g_re = (bbar_re * qr - bbar_im * qi).astype(BF16)
        g_im = (bbar_re * qi + bbar_im * qr).astype(BF16)
        t = chunk - 1 - tau
        wb_ref[t * LANES:(t + 1) * LANES, 0:ns] = g_re
        wb_ref[t * LANES:(t + 1) * LANES, ns:2 * ns] = g_im
        qr, qi = pr[tau + 1:tau + 2, :], pi[tau + 1:tau + 2, :]
        wct_ref[tau * LANES:(tau + 1) * LANES, 0:ns] = (c_re * qr - c_im * qi).astype(BF16)
        wct_ref[tau * LANES:(tau + 1) * LANES, ns:2 * ns] = (-(c_re * qi + c_im * qr)).astype(BF16)

    c_cat = jnp.concatenate([c_re, -c_im], axis=1).astype(BF16)
    k_all = lax.dot_general(wb_ref[...], c_cat, (((1,), (1,)), ((), ())),
                            preferred_element_type=F32)
    zero_blk = jnp.zeros((LANES, LANES), BF16)
    for tau in range(chunk):
        t = chunk - 1 - tau
        k_tau = k_all[t * LANES:(t + 1) * LANES, :].astype(BF16)
        toe_ref[0:LANES, tau * LANES:(tau + 1) * LANES] = k_tau
        if tau + 1 < chunk:
            toe_ref[LANES:2 * LANES, (tau + 1) * LANES:(tau + 2) * LANES] = k_tau
    toe_ref[LANES:2 * LANES, 0:LANES] = zero_blk


def _s5_prep(a_re, a_im, log_dt, b_re, b_im, c_re, c_im, *, chunk):
    g, p = a_re.shape
    nt = g // TILE_GROUPS
    ns = TILE_STATES
    ldt = jnp.broadcast_to(log_dt[:, None], (g, p))
    a_rows = jnp.stack([a_re.reshape(nt, ns), a_im.reshape(nt, ns), ldt.reshape(nt, ns)], axis=1)
    bt = jnp.stack([b_re, b_im], axis=0).transpose(1, 0, 3, 2).reshape(nt, TILE_GROUPS, 2, S5_GROUP, p)
    bt = bt.transpose(0, 2, 1, 3, 4).reshape(nt, 2, LANES, p)
    cc = jnp.stack([c_re.reshape(nt, LANES, p), c_im.reshape(nt, LANES, p)], axis=1)
    kw = chunk * LANES
    return pl.pallas_call(
        functools.partial(_s5_prep_body, chunk=chunk),
        grid=(nt,),
        in_specs=[pl.BlockSpec((None, 3, ns), lambda j: (j, 0, 0)),
                  pl.BlockSpec((None, 2, LANES, p), lambda j: (j, 0, 0, 0)),
                  pl.BlockSpec((None, 2, LANES, p), lambda j: (j, 0, 0, 0))],
        out_specs=[pl.BlockSpec((None, kw, 2 * ns), lambda j: (j, 0, 0)),
                   pl.BlockSpec((None, kw, 2 * ns), lambda j: (j, 0, 0)),
                   pl.BlockSpec((None, 2 * LANES, kw), lambda j: (j, 0, 0)),
                   pl.BlockSpec((None, 1, 2 * ns), lambda j: (j, 0, 0))],
        out_shape=[jax.ShapeDtypeStruct((nt, kw, 2 * ns), BF16),
                   jax.ShapeDtypeStruct((nt, kw, 2 * ns), BF16),
                   jax.ShapeDtypeStruct((nt, 2 * LANES, kw), BF16),
                   jax.ShapeDtypeStruct((nt, 1, 2 * ns), F32)],
        compiler_params=_params(("parallel",)),
        name="s5_prep",
    )(a_rows, bt, cc)


def _s5_core_body(u_ref, d_ref, wb_ref, wct_ref, toe_ref, lam_ref, o_ref,
                  ub_ref, slab_ref, hin_ref, hprev_ref, *, chunk, n_seq, n_chunks):
    ns = TILE_STATES
    nk = ns // LANES
    seq = n_chunks * chunk

    def token_rows(b, t):
        return pl.ds(b * seq + t, n_chunks, stride=chunk)

    def chunk_rows(b):
        return pl.ds(b, n_chunks, stride=n_seq)

    for t in range(chunk):
        for b in range(n_seq):
            slab_ref[t, chunk_rows(b), :] = u_ref[token_rows(b, t), :]
        ub_ref[:, t * LANES:(t + 1) * LANES] = slab_ref[t].astype(BF16)

    h_in = jnp.dot(ub_ref[...], wb_ref[...], preferred_element_type=F32)
    for k in range(2 * nk):
        hin_ref[k] = h_in[:, k * LANES:(k + 1) * LANES]

    lam = [jnp.broadcast_to(lam_ref[:, k * LANES:(k + 1) * LANES], (n_seq, LANES)) for k in range(2 * nk)]
    state = [jnp.zeros((n_seq, LANES), F32) for _ in range(2 * nk)]
    for c in range(n_chunks):
        rows = pl.ds(c * n_seq, n_seq)
        new_state = []
        for k in range(nk):
            hr, hi = state[k], state[nk + k]
            hprev_ref[k, rows, :] = hr
            hprev_ref[nk + k, rows, :] = hi
            lr, li = lam[k], lam[nk + k]
            new_state.append((lr * hr - li * hi + hin_ref[k, rows, :],
                              lr * hi + li * hr + hin_ref[nk + k, rows, :]))
        state = [s[0] for s in new_state] + [s[1] for s in new_state]

    h_prev = jnp.concatenate([hprev_ref[k].astype(BF16) for k in range(2 * nk)], axis=1)
    y = lax.dot_general(h_prev, wct_ref[...], (((1,), (1,)), ((), ())), preferred_element_type=F32)
    for t in range(chunk):
        slab_ref[t] = y[:, t * LANES:(t + 1) * LANES]
    for a in range(chunk // 2):
        lo = 2 * a * LANES
        width = (chunk - 2 * a) * LANES
        part = jnp.dot(ub_ref[:, lo:lo + 2 * LANES], toe_ref[:, 0:width], preferred_element_type=F32)
        for i in range(chunk - 2 * a):
            slab_ref[2 * a + i] += part[:, i * LANES:(i + 1) * LANES]
    for t in range(chunk):
        for b in range(n_seq):
            y_bt = slab_ref[t, chunk_rows(b), :] + d_ref[...] * u_ref[token_rows(b, t), :]
            o_ref[token_rows(b, t), :] = jax.nn.gelu(y_bt)


def _s5_core(proj2d, d_skip, wb, wct, toe, lam, *, chunk, n_seq):
    m, n2 = proj2d.shape
    e = n2 // 2
    r = m // chunk
    nt = e // LANES
    ns2 = 2 * TILE_STATES
    kw = chunk * LANES
    return pl.pallas_call(
        functools.partial(_s5_core_body, chunk=chunk, n_seq=n_seq, n_chunks=r // n_seq),
        grid=(nt,),
        in_specs=[pl.BlockSpec((m, LANES), lambda j: (0, j)),
                  pl.BlockSpec((1, LANES), lambda j: (0, j)),
                  pl.BlockSpec((None, kw, ns2), lambda j: (j, 0, 0)),
                  pl.BlockSpec((None, kw, ns2), lambda j: (j, 0, 0)),
                  pl.BlockSpec((None, 2 * LANES, kw), lambda j: (j, 0, 0)),
                  pl.BlockSpec((None, 1, ns2), lambda j: (j, 0, 0))],
        out_specs=pl.BlockSpec((m, LANES), lambda j: (0, j)),
        out_shape=jax.ShapeDtypeStruct((m, e), F32),
        scratch_shapes=[pltpu.VMEM((r, kw), BF16),
                        pltpu.VMEM((chunk, r, LANES), F32),
                        pltpu.VMEM((ns2 // LANES, r, LANES), F32),
                        pltpu.VMEM((ns2 // LANES, r, LANES), F32)],
        compiler_params=_params(("parallel",)),
        name="s5_core",
    )(proj2d, d_skip.reshape(1, e), wb, wct, toe, lam)


def _glu_body(g_ref, w_ref, b_ref, o_ref):
    g = g_ref[...]
    acc = jnp.dot(g.astype(BF16), w_ref[...], preferred_element_type=F32) + b_ref[...]
    o_ref[...] = g * jax.nn.sigmoid(acc)


def _glu(g2d, w_bf16, b, *, tm):
    m, e = g2d.shape
    return pl.pallas_call(
        _glu_body,
        grid=(m // tm,),
        in_specs=[pl.BlockSpec((tm, e), lambda i: (i, 0)),
                  pl.BlockSpec((e, e), lambda i: (0, 0), pipeline_mode=pl.Buffered(1)),
                  pl.BlockSpec((1, e), lambda i: (0, 0))],
        out_specs=pl.BlockSpec((tm, e), lambda i: (i, 0)),
        out_shape=jax.ShapeDtypeStruct((m, e), F32),
        compiler_params=_params(("parallel",)),
        name="s5_glu",
    )(g2d, w_bf16, b.reshape(1, e))


def _fox_cum_body(f_ref, o_ref, *, blk):
    seq = f_ref.shape[0]
    row = lax.broadcasted_iota(jnp.int32, (blk, blk), 0)
    col = lax.broadcasted_iota(jnp.int32, (blk, blk), 1)
    tri = (col <= row).astype(F32)
    carry = jnp.zeros((1, LANES), F32)
    for i in range(seq // blk):
        x = f_ref[i * blk:(i + 1) * blk, :]
        ls = jnp.minimum(x, 0.0) - jnp.log1p(jnp.exp(-jnp.abs(x)))
        cum = jnp.dot(tri, ls, preferred_element_type=F32,
                      precision=lax.Precision.HIGHEST) + carry
        o_ref[i * blk:(i + 1) * blk, :] = cum
        carry = cum[blk - 1:blk, :]


def _fox_cum(f3d, *, blk):
    b, seq, _ = f3d.shape
    return pl.pallas_call(
        functools.partial(_fox_cum_body, blk=blk),
        grid=(b,),
        in_specs=[pl.BlockSpec((None, seq, LANES), lambda i: (i, 0, 0))],
        out_specs=pl.BlockSpec((None, seq, LANES), lambda i: (i, 0, 0)),
        out_shape=jax.ShapeDtypeStruct((b, seq, LANES), F32),
        compiler_params=_params(("parallel",)),
        name="fox_cum",
    )(f3d)


def _fox_attn_body(q_ref, k_ref, v_ref, ck_ref, qw_ref, kw_ref, o_ref,
                   qn_ref, kn_ref, vb_ref, m_ref, l_ref, acc_ref, *, tq, scale, row_chunk):
    seq = k_ref.shape[0]
    nq = seq // tq
    log2e = 1.4426950408889634

    def chunk(c, carry):
        rows = pl.ds(pl.multiple_of(c * row_chunk, row_chunk), row_chunk)
        qn_ref[rows, :] = (_rms_rows(q_ref[rows, :], qw_ref[...]) * (scale * log2e)).astype(BF16)
        kn_ref[rows, :] = _rms_rows(k_ref[rows, :], kw_ref[...]).astype(BF16)
        vb_ref[rows, :] = v_ref[rows, :].astype(BF16)
        return carry
    lax.fori_loop(0, seq // row_chunk, chunk, 0)

    pairs = [(i, j) for i in range(nq) for j in range(i + 1)]

    def logits(i, j):
        return lax.dot_general(qn_ref[i * tq:(i + 1) * tq, :], kn_ref[j * tq:(j + 1) * tq, :],
                               (((1,), (1,)), ((), ())), preferred_element_type=F32)

    n_lt = tq // LANES
    s_next = logits(*pairs[0])
    for idx, (i, j) in enumerate(pairs):
        s = s_next
        if idx + 1 < len(pairs):
            s_next = logits(*pairs[idx + 1])
        c0 = ck_ref[i:i + 1, 0:1]
        dk = (ck_ref[j:j + 1, :] - c0) * log2e
        s_t = [s[:, t * LANES:(t + 1) * LANES] - dk[:, t * LANES:(t + 1) * LANES] for t in range(n_lt)]
        if i == j:
            r_i = lax.broadcasted_iota(jnp.int32, (tq, LANES), 0)
            c_i = lax.broadcasted_iota(jnp.int32, (tq, LANES), 1)
            s_t = [jnp.where(c_i + t * LANES <= r_i, s_t[t], NEG_BIG) for t in range(n_lt)]
        mx = functools.reduce(jnp.maximum, s_t)
        row_max = jnp.broadcast_to(jnp.max(mx, axis=-1, keepdims=True), (tq, LANES))
        v_blk = vb_ref[j * tq:(j + 1) * tq, :]
        if j == 0:
            m_new = row_max
        else:
            m_old = m_ref[...]
            m_new = jnp.maximum(m_old, row_max)
        p_t = [jnp.exp2(s_t[t] - m_new) for t in range(n_lt)]
        p_sum = functools.reduce(jnp.add, p_t)
        pv = jnp.dot(jnp.concatenate([p.astype(BF16) for p in p_t], axis=1), v_blk,
                     preferred_element_type=F32)
        if j == 0:
            l_new, acc_new = p_sum, pv
        else:
            alpha = jnp.exp2(m_old - m_new)
            l_new = alpha * l_ref[...] + p_sum
            acc_new = alpha * acc_ref[...] + pv
        if j == i:
            l_row = jnp.sum(l_new, axis=-1, keepdims=True)
            o_ref[i * tq:(i + 1) * tq, :] = acc_new / l_row
        else:
            m_ref[...] = m_new
            l_ref[...] = l_new
            acc_ref[...] = acc_new


def _fox_attn(proj3d, cum_k, q_w, k_w, *, heads, tq):
    b, seq, _ = proj3d.shape
    dh = FOX_HEAD_DIM
    nq = seq // tq
    return pl.pallas_call(
        functools.partial(_fox_attn_body, tq=tq, scale=dh ** -0.5, row_chunk=min(seq, 256)),
        grid=(b, heads),
        in_specs=[pl.BlockSpec((None, seq, dh), lambda bi, h: (bi, 0, h)),
                  pl.BlockSpec((None, seq, dh), lambda bi, h: (bi, 0, heads + h)),
                  pl.BlockSpec((None, seq, dh), lambda bi, h: (bi, 0, 2 * heads + h)),
                  pl.BlockSpec((None, None, nq, tq), lambda bi, h: (bi, h, 0, 0)),
                  pl.BlockSpec((1, dh), lambda bi, h: (0, 0)),
                  pl.BlockSpec((1, dh), lambda bi, h: (0, 0))],
        out_specs=pl.BlockSpec((None, seq, dh), lambda bi, h: (bi, 0, h)),
        out_shape=jax.ShapeDtypeStruct((b, seq, heads * dh), F32),
        scratch_shapes=[pltpu.VMEM((seq, dh), BF16),
                        pltpu.VMEM((seq, dh), BF16),
                        pltpu.VMEM((seq, dh), BF16),
                        pltpu.VMEM((tq, LANES), F32),
                        pltpu.VMEM((tq, LANES), F32),
                        pltpu.VMEM((tq, dh), F32)],
        compiler_params=_params(("parallel", "parallel")),
        name="fox_attn",
    )(proj3d, proj3d, proj3d, cum_k, q_w.reshape(1, dh), k_w.reshape(1, dh))


def _pool_body(u_ref, w_ref, s_ref, o_ref, sa_ref, sb_ref, *, windows):
    seq, cg = u_ref.shape
    halo = POOL_HALO
    grp = pl.program_id(1)
    zeros = jnp.zeros((halo, cg), F32)
    sa_ref[0:halo, :] = zeros
    sb_ref[0:halo, :] = zeros
    t_idx = lax.broadcasted_iota(jnp.int32, (seq, 1), 0)

    for gi, win in enumerate(windows):
        @pl.when(grp == gi)
        def _(win=win):
            u = u_ref[...]
            sa_ref[halo:halo + seq, :] = u
            bufs = (sa_ref, sb_ref)
            s = u
            shift = 1
            level = 0
            while shift < win:
                src = bufs[level % 2]
                s = src[halo:halo + seq, :] + src[halo - shift:halo - shift + seq, :]
                shift *= 2
                level += 1
                if shift < win:
                    bufs[level % 2][halo:halo + seq, :] = s
            cnt = jnp.minimum(t_idx + 1, win).astype(F32)
            diff = (s / cnt - u).astype(BF16)
            mixed = jnp.dot(diff, w_ref[...].astype(BF16), preferred_element_type=F32)
            o_ref[...] = mixed * s_ref[...]


def _pool(proj3d, w_group, scale, *, windows):
    b, seq, n2 = proj3d.shape
    e = n2 // 2
    ng = len(windows)
    cg = e // ng
    return pl.pallas_call(
        functools.partial(_pool_body, windows=windows),
        grid=(b, ng),
        in_specs=[pl.BlockSpec((None, seq, cg), lambda bi, g: (bi, 0, g)),
                  pl.BlockSpec((None, cg, cg), lambda bi, g: (g, 0, 0)),
                  pl.BlockSpec((1, cg), lambda bi, g: (0, g))],
        out_specs=pl.BlockSpec((None, seq, cg), lambda bi, g: (bi, 0, g)),
        out_shape=jax.ShapeDtypeStruct((b, seq, e), F32),
        scratch_shapes=[pltpu.VMEM((POOL_HALO + seq, cg), F32),
                        pltpu.VMEM((POOL_HALO + seq, cg), F32)],
        compiler_params=_params(("parallel", "parallel")),
        name="pool_mix",
    )(proj3d, w_group, scale.reshape(1, e))


def _pick(n, pref):
    t = min(n, pref)
    while n % t:
        t //= 2
    return t


def kernel(x, norm_w, out_proj, s5_in_proj, s5_a_re, s5_a_im, s5_log_dt, s5_b_re, s5_b_im, s5_c_re, s5_c_im, s5_d, s5_w_glu, s5_b_glu, fox_in_proj, fox_q_norm, fox_k_norm, fox_f_bias, pool_in_proj, pool_w_group, pool_scale):
    bsz, seq, d = x.shape
    depth = norm_w.shape[0]
    e = s5_d.shape[1]
    heads = e // FOX_HEAD_DIM
    m = bsz * seq
    n_mixers = 3

    tm = _pick(m, 1024)
    tm_out = _pick(m, 512)
    tq = _pick(seq, 512)

    h = x.reshape(m, d)
    for i in range(depth):
        kind, j = i % n_mixers, i // n_mixers
        w_out = out_proj[i].astype(BF16)
        if kind == 0:
            wb, wct, toe, lam = _s5_prep(s5_a_re[j], s5_a_im[j], s5_log_dt[j], s5_b_re[j], s5_b_im[j],
                                         s5_c_re[j], s5_c_im[j], chunk=CHUNK)
            proj = _norm_proj(h, norm_w[i], s5_in_proj[j], tm=tm, tn=_pick(2 * e, 512))
            g = _s5_core(proj, s5_d[j], wb, wct, toe, lam, chunk=CHUNK, n_seq=bsz)
            y = _glu(g, s5_w_glu[j].astype(BF16), s5_b_glu[j], tm=tm_out)
            h = _out_proj(y, proj, 1, h, w_out, tm=tm_out)
        elif kind == 1:
            w_in = fox_in_proj[j]
            n_main = 4 * e
            w_f = jnp.pad(w_in[:, n_main:], ((0, 0), (0, LANES - heads)))
            b_f = jnp.pad(fox_f_bias[j], (0, LANES - heads)).reshape(1, LANES)
            proj, f_logit = _norm_proj_extra(h, norm_w[i], w_in, w_f, b_f, tm=tm, tn=_pick(n_main, 512))
            cum = _fox_cum(f_logit.reshape(bsz, seq, LANES), blk=_pick(seq, 256))
            cum_k = cum[:, :, :heads].transpose(0, 2, 1).reshape(bsz, heads, seq // tq, tq)
            y = _fox_attn(proj.reshape(bsz, seq, n_main), cum_k, fox_q_norm[j], fox_k_norm[j],
                          heads=heads, tq=tq)
            h = _out_proj(y.reshape(m, e), proj, 3, h, w_out, tm=tm_out)
        else:
            proj = _norm_proj(h, norm_w[i], pool_in_proj[j], tm=tm, tn=_pick(2 * e, 512))
            y = _pool(proj.reshape(bsz, seq, 2 * e), pool_w_group[j], pool_scale[j], windows=POOL_WINDOWS)
            h = _out_proj(y.reshape(m, e), proj, 1, h, w_out, tm=tm_out)
    return h.reshape(bsz, seq, d)
```

```python
import functools
import math

import jax
import jax.numpy as jnp
from jax import lax
from jax.experimental import pallas as pl
from jax.experimental.pallas import tpu as pltpu

F32 = jnp.float32
BF16 = jnp.bfloat16

LANES = 128
S5_GROUP = 16
S5_STATE = 64
TILE_GROUPS = LANES // S5_GROUP
TILE_STATES = TILE_GROUPS * S5_STATE
CHUNK = 8
FOX_HEAD_DIM = 128
POOL_WINDOWS = (2, 4, 8, 16)
POOL_HALO = 16
EPS = 1e-6
NEG_BIG = -1e30
VMEM_LIMIT = 56 * 1024 * 1024


def _params(semantics):
    return pltpu.CompilerParams(dimension_semantics=semantics, vmem_limit_bytes=VMEM_LIMIT)


def _rms_rows(x, w):
    ms = jnp.mean(x * x, axis=-1, keepdims=True)
    return x * lax.rsqrt(ms + EPS) * w


def _silu(z):
    return z * jax.nn.sigmoid(z)


def _frozen_tile(i, j, n_tiles):
    return jnp.where(i == 0, j, n_tiles - 1)


def _rows_loop(n_rows, row_chunk, fn):
    def body(c, carry):
        fn(pl.ds(pl.multiple_of(c * row_chunk, row_chunk), row_chunk))
        return carry
    lax.fori_loop(0, n_rows // row_chunk, body, 0)


def _norm_proj_split_body(x_ref, nw_ref, w_ref, u_ref, z_ref, xn_ref, wbf_ref, *, n_f32_tiles, row_chunk):
    i, j = pl.program_id(0), pl.program_id(1)

    @pl.when(j == 0)
    def _():
        def norm(rows):
            xn_ref[rows, :] = _rms_rows(x_ref[rows, :], nw_ref[...]).astype(BF16)
        _rows_loop(x_ref.shape[0], row_chunk, norm)

    @pl.when(i == 0)
    def _():
        wbf_ref[j] = w_ref[...].astype(BF16)

    acc = jnp.dot(xn_ref[...], wbf_ref[j], preferred_element_type=F32)

    @pl.when(j < n_f32_tiles)
    def _():
        u_ref[...] = acc

    @pl.when(j >= n_f32_tiles)
    def _():
        z_ref[...] = acc.astype(BF16)


def _norm_proj_split(h2d, norm_w, w, *, tm, tn):
    m, d = h2d.shape
    n = w.shape[1]
    nj = n // tn
    na = nj // 2
    return pl.pallas_call(
        functools.partial(_norm_proj_split_body, n_f32_tiles=na, row_chunk=min(tm, 128)),
        grid=(m // tm, nj),
        in_specs=[pl.BlockSpec((tm, d), lambda i, j: (i, 0)),
                  pl.BlockSpec((1, d), lambda i, j: (0, 0)),
                  pl.BlockSpec((d, tn), lambda i, j: (0, _frozen_tile(i, j, nj)))],
        out_specs=[pl.BlockSpec((tm, tn), lambda i, j: (i, jnp.minimum(j, na - 1))),
                   pl.BlockSpec((tm, tn), lambda i, j: (i, jnp.maximum(j - na, 0)))],
        out_shape=[jax.ShapeDtypeStruct((m, n // 2), F32),
                   jax.ShapeDtypeStruct((m, n // 2), BF16)],
        scratch_shapes=[pltpu.VMEM((tm, d), BF16),
                        pltpu.VMEM((nj, d, tn), BF16)],
        compiler_params=_params(("arbitrary", "arbitrary")),
        name="norm_proj_split",
    )(h2d, norm_w.reshape(1, d), w)


def _norm_proj_fox_body(x_ref, nw_ref, w_ref, wf_ref, bf_ref, o_ref, of_ref, xn_ref, wbf_ref, *, row_chunk):
    half, i, j = pl.program_id(0), pl.program_id(1), pl.program_id(2)

    @pl.when(j == 0)
    def _():
        def norm(rows):
            xn_ref[rows, :] = _rms_rows(x_ref[rows, :], nw_ref[...]).astype(BF16)
        _rows_loop(x_ref.shape[0], row_chunk, norm)

    @pl.when(jnp.logical_and(j == 0, half == 0))
    def _():
        of_ref[...] = jnp.dot(xn_ref[...], wf_ref[...].astype(BF16),
                              preferred_element_type=F32) + bf_ref[...]

    @pl.when(i == 0)
    def _():
        wbf_ref[j] = w_ref[...].astype(BF16)

    o_ref[...] = jnp.dot(xn_ref[...], wbf_ref[j], preferred_element_type=F32).astype(BF16)


def _norm_proj_fox(h2d, norm_w, w, w_extra, b_extra, *, n_main, n_half, tm, tn):
    m, d = h2d.shape
    nj = n_half // tn
    nh = n_main // n_half
    ni = m // tm
    return pl.pallas_call(
        functools.partial(_norm_proj_fox_body, row_chunk=min(tm, 128)),
        grid=(nh, ni, nj),
        in_specs=[pl.BlockSpec((tm, d), lambda hf, i, j: (i, 0)),
                  pl.BlockSpec((1, d), lambda hf, i, j: (0, 0)),
                  pl.BlockSpec((d, tn), lambda hf, i, j: (0, hf * nj + _frozen_tile(i, j, nj))),
                  pl.BlockSpec((d, LANES), lambda hf, i, j: (0, 0)),
                  pl.BlockSpec((1, LANES), lambda hf, i, j: (0, 0))],
        out_specs=[pl.BlockSpec((tm, tn), lambda hf, i, j: (i, hf * nj + j)),
                   pl.BlockSpec((tm, LANES), lambda hf, i, j: (jnp.where(hf == 0, i, ni - 1), 0))],
        out_shape=[jax.ShapeDtypeStruct((m, n_main), BF16),
                   jax.ShapeDtypeStruct((m, LANES), F32)],
        scratch_shapes=[pltpu.VMEM((tm, d), BF16),
                        pltpu.VMEM((nj, d, tn), BF16)],
        compiler_params=_params(("arbitrary", "arbitrary", "arbitrary")),
        name="norm_proj_fox",
    )(h2d, norm_w.reshape(1, d), w, w_extra, b_extra)


def _out_proj_body(y_ref, z_ref, h_ref, w_ref, o_ref, a_ref, wbf_ref, *, row_chunk):
    i, j = pl.program_id(0), pl.program_id(1)

    @pl.when(j == 0)
    def _():
        def gate(rows):
            a_ref[rows, :] = (y_ref[rows, :].astype(F32) * _silu(z_ref[rows, :].astype(F32))).astype(BF16)
        _rows_loop(y_ref.shape[0], row_chunk, gate)

    @pl.when(i == 0)
    def _():
        wbf_ref[j] = w_ref[...].astype(BF16)

    o_ref[...] = h_ref[...] + jnp.dot(a_ref[...], wbf_ref[j], preferred_element_type=F32)


def _out_proj(y2d, z2d, z_block, h2d, w, *, tm, tn):
    m, e = y2d.shape
    d = w.shape[1]
    nj = d // tn
    return pl.pallas_call(
        functools.partial(_out_proj_body, row_chunk=min(tm, 128)),
        grid=(m // tm, nj),
        in_specs=[pl.BlockSpec((tm, e), lambda i, j: (i, 0)),
                  pl.BlockSpec((tm, e), lambda i, j: (i, z_block)),
                  pl.BlockSpec((tm, tn), lambda i, j: (i, j)),
                  pl.BlockSpec((e, tn), lambda i, j: (0, _frozen_tile(i, j, nj)))],
        out_specs=pl.BlockSpec((tm, tn), lambda i, j: (i, j)),
        out_shape=jax.ShapeDtypeStruct((m, d), F32),
        scratch_shapes=[pltpu.VMEM((tm, e), BF16),
                        pltpu.VMEM((nj, e, tn), BF16)],
        compiler_params=_params(("arbitrary", "arbitrary")),
        name="out_proj",
    )(y2d, z2d, h2d, w)


def _s5_prep_body(a_ref, bt_ref, c_ref, wb_ref, wct_ref, toe_ref, lam_ref, *, chunk):
    ns = TILE_STATES
    n_lt = ns // LANES

    def block_diag(x):
        x2 = jnp.concatenate([x, x], axis=1)
        grp_row = lax.broadcasted_iota(jnp.int32, (LANES, LANES), 0) // S5_GROUP
        grp_lane = lax.broadcasted_iota(jnp.int32, (LANES, LANES), 1) // S5_STATE
        per_tile = LANES // S5_STATE
        return jnp.concatenate([jnp.where(grp_row == grp_lane + per_tile * t, x2, 0.0)
                                for t in range(n_lt)], axis=1)

    ar, ai, ldt = a_ref[0:1, :], a_ref[1:2, :], a_ref[2:3, :]
    dt = jnp.exp(ldt)
    k_rows = lax.broadcasted_iota(jnp.int32, (chunk + 1, 1), 0).astype(F32)
    mag = jnp.exp(ar * dt * k_rows)
    ang = ai * dt * k_rows
    pr, pi = mag * jnp.cos(ang), mag * jnp.sin(ang)
    abar_r, abar_i = pr[1:2, :], pi[1:2, :]
    den = ar * ar + ai * ai
    xr = abar_r - 1.0
    fr = (xr * ar + abar_i * ai) / den
    fi = (abar_i * ar - xr * ai) / den
    bt_re, bt_im = block_diag(bt_ref[0]), block_diag(bt_ref[1])
    bbar_re = fr * bt_re - fi * bt_im
    bbar_im = fr * bt_im + fi * bt_re
    c_re, c_im = block_diag(c_ref[0]), block_diag(c_ref[1])
    lam_ref[:, 0:ns] = pr[chunk:chunk + 1, :]
    lam_ref[:, ns:2 * ns] = pi[chunk:chunk + 1, :]

    for tau in range(chunk):
        qr, qi = pr[tau:tau + 1, :], pi[tau:tau + 1, :]
        g_re = (bbar_re * qr - bbar_im * qi).astype(BF16)
        g_im = (bbar_re * qi + bbar_im * qr).astype(BF16)
        t = chunk - 1 - tau
        wb_ref[t * LANES:(t + 1) * LANES, 0:ns] = g_re
        wb_ref[t * LANES:(t + 1) * LANES, ns:2 * ns] = g_im
        qr, qi = pr[tau + 1:tau + 2, :], pi[tau + 1:tau + 2, :]
        wct_ref[tau * LANES:(tau + 1) * LANES, 0:ns] = (c_re * qr - c_im * qi).astype(BF16)
        wct_ref[tau * LANES:(tau + 1) * LANES, ns:2 * ns] = (-(c_re * qi + c_im * qr)).astype(BF16)

    c_cat = jnp.concatenate([c_re, -c_im], axis=1).astype(BF16)
    k_all = lax.dot_general(wb_ref[...], c_cat, (((1,), (1,)), ((), ())),
                            preferred_element_type=F32)
    zero_blk = jnp.zeros((LANES, LANES), BF16)
    for tau in range(chunk):
        t = chunk - 1 - tau
        k_tau = k_all[t * LANES:(t + 1) * LANES, :].astype(BF16)
        toe_ref[0:LANES, tau * LANES:(tau + 1) * LANES] = k_tau
        if tau + 1 < chunk:
            toe_ref[LANES:2 * LANES, (tau + 1) * LANES:(tau + 2) * LANES] = k_tau
    toe_ref[LANES:2 * LANES, 0:LANES] = zero_blk


def _s5_prep(a_re, a_im, log_dt, b_re, b_im, c_re, c_im, *, chunk):
    g, p = a_re.shape
    nt = g // TILE_GROUPS
    ns = TILE_STATES
    ldt = jnp.broadcast_to(log_dt[:, None], (g, p))
    a_rows = jnp.stack([a_re.reshape(nt, ns), a_im.reshape(nt, ns), ldt.reshape(nt, ns)], axis=1)
    bt = jnp.stack([b_re, b_im], axis=0).transpose(1, 0, 3, 2).reshape(nt, TILE_GROUPS, 2, S5_GROUP, p)
    bt = bt.transpose(0, 2, 1, 3, 4).reshape(nt, 2, LANES, p)
    cc = jnp.stack([c_re.reshape(nt, LANES, p), c_im.reshape(nt, LANES, p)], axis=1)
    kw = chunk * LANES
    return pl.pallas_call(
        functools.partial(_s5_prep_body, chunk=chunk),
        grid=(nt,),
        in_specs=[pl.BlockSpec((None, 3, ns), lambda j: (j, 0, 0)),
                  pl.BlockSpec((None, 2, LANES, p), lambda j: (j, 0, 0, 0)),
                  pl.BlockSpec((None, 2, LANES, p), lambda j: (j, 0, 0, 0))],
        out_specs=[pl.BlockSpec((None, kw, 2 * ns), lambda j: (j, 0, 0)),
                   pl.BlockSpec((None, kw, 2 * ns), lambda j: (j, 0, 0)),
                   pl.BlockSpec((None, 2 * LANES, kw), lambda j: (j, 0, 0)),
                   pl.BlockSpec((None, 1, 2 * ns), lambda j: (j, 0, 0))],
        out_shape=[jax.ShapeDtypeStruct((nt, kw, 2 * ns), BF16),
                   jax.ShapeDtypeStruct((nt, kw, 2 * ns), BF16),
                   jax.ShapeDtypeStruct((nt, 2 * LANES, kw), BF16),
                   jax.ShapeDtypeStruct((nt, 1, 2 * ns), F32)],
        compiler_params=_params(("parallel",)),
        name="s5_prep",
    )(a_rows, bt, cc)


def _s5_core_body(u_ref, d_ref, wb_ref, wct_ref, toe_ref, lam_ref, o_ref,
                  ub_ref, slab_ref, hin_ref, hprev_ref, *, chunk, n_seq, n_chunks):
    ns = TILE_STATES
    nk = ns // LANES
    seq = n_chunks * chunk

    def token_rows(b, t):
        return pl.ds(b * seq + t, n_chunks, stride=chunk)

    def chunk_rows(b):
        return pl.ds(b, n_chunks, stride=n_seq)

    for t in range(chunk):
        for b in range(n_seq):
            slab_ref[t, chunk_rows(b), :] = u_ref[token_rows(b, t), :]
        ub_ref[:, t * LANES:(t + 1) * LANES] = slab_ref[t].astype(BF16)

    h_in = jnp.dot(ub_ref[...], wb_ref[...], preferred_element_type=F32)
    for k in range(2 * nk):
        hin_ref[k] = h_in[:, k * LANES:(k + 1) * LANES]

    lam = [jnp.broadcast_to(lam_ref[:, k * LANES:(k + 1) * LANES], (n_seq, LANES)) for k in range(2 * nk)]
    state = [jnp.zeros((n_seq, LANES), F32) for _ in range(2 * nk)]
    for c in range(n_chunks):
        rows = pl.ds(c * n_seq, n_seq)
        new_state = []
        for k in range(nk):
            hr, hi = state[k], state[nk + k]
            hprev_ref[k, rows, :] = hr
            hprev_ref[nk + k, rows, :] = hi
            lr, li = lam[k], lam[nk + k]
            new_state.append((lr * hr - li * hi + hin_ref[k, rows, :],
                              lr * hi + li * hr + hin_ref[nk + k, rows, :]))
        state = [s[0] for s in new_state] + [s[1] for s in new_state]

    h_prev = jnp.concatenate([hprev_ref[k].astype(BF16) for k in range(2 * nk)], axis=1)
    y = lax.dot_general(h_prev, wct_ref[...], (((1,), (1,)), ((), ())), preferred_element_type=F32)
    for t in range(chunk):
        slab_ref[t] = y[:, t * LANES:(t + 1) * LANES]
    for a in range(chunk // 2):
        lo = 2 * a * LANES
        width = (chunk - 2 * a) * LANES
        part = jnp.dot(ub_ref[:, lo:lo + 2 * LANES], toe_ref[:, 0:width], preferred_element_type=F32)
        for i in range(chunk - 2 * a):
            slab_ref[2 * a + i] += part[:, i * LANES:(i + 1) * LANES]
    for t in range(chunk):
        for b in range(n_seq):
            y_bt = slab_ref[t, chunk_rows(b), :] + d_ref[...] * u_ref[token_rows(b, t), :]
            o_ref[token_rows(b, t), :] = jax.nn.gelu(y_bt)


def _s5_core(u2d, d_skip, wb, wct, toe, lam, *, chunk, n_seq):
    m, e = u2d.shape
    r = m // chunk
    nt = e // LANES
    ns2 = 2 * TILE_STATES
    kw = chunk * LANES
    return pl.pallas_call(
        functools.partial(_s5_core_body, chunk=chunk, n_seq=n_seq, n_chunks=r // n_seq),
        grid=(nt,),
        in_specs=[pl.BlockSpec((m, LANES), lambda j: (0, j)),
                  pl.BlockSpec((1, LANES), lambda j: (0, j)),
                  pl.BlockSpec((None, kw, ns2), lambda j: (j, 0, 0)),
                  pl.BlockSpec((None, kw, ns2), lambda j: (j, 0, 0)),
                  pl.BlockSpec((None, 2 * LANES, kw), lambda j: (j, 0, 0)),
                  pl.BlockSpec((None, 1, ns2), lambda j: (j, 0, 0))],
        out_specs=pl.BlockSpec((m, LANES), lambda j: (0, j)),
        out_shape=jax.ShapeDtypeStruct((m, e), F32),
        scratch_shapes=[pltpu.VMEM((r, kw), BF16),
                        pltpu.VMEM((chunk, r, LANES), F32),
                        pltpu.VMEM((ns2 // LANES, r, LANES), F32),
                        pltpu.VMEM((ns2 // LANES, r, LANES), F32)],
        compiler_params=_params(("parallel",)),
        name="s5_core",
    )(u2d, d_skip.reshape(1, e), wb, wct, toe, lam)


def _glu_body(g_ref, gt_ref, w_ref, b_ref, o_ref, gb_ref, wbf_ref, *, row_chunk):
    i, j = pl.program_id(0), pl.program_id(1)

    @pl.when(j == 0)
    def _():
        def cast(rows):
            gb_ref[rows, :] = g_ref[rows, :].astype(BF16)
        _rows_loop(g_ref.shape[0], row_chunk, cast)

    @pl.when(i == 0)
    def _():
        wbf_ref[j] = w_ref[...].astype(BF16)

    acc = jnp.dot(gb_ref[...], wbf_ref[j], preferred_element_type=F32) + b_ref[...]
    o_ref[...] = (gt_ref[...] * jax.nn.sigmoid(acc)).astype(BF16)


def _glu(g2d, w, b, *, tm, tn):
    m, e = g2d.shape
    nj = e // tn
    return pl.pallas_call(
        functools.partial(_glu_body, row_chunk=min(tm, 128)),
        grid=(m // tm, nj),
        in_specs=[pl.BlockSpec((tm, e), lambda i, j: (i, 0)),
                  pl.BlockSpec((tm, tn), lambda i, j: (i, j)),
                  pl.BlockSpec((e, tn), lambda i, j: (0, _frozen_tile(i, j, nj))),
                  pl.BlockSpec((1, tn), lambda i, j: (0, j))],
        out_specs=pl.BlockSpec((tm, tn), lambda i, j: (i, j)),
        out_shape=jax.ShapeDtypeStruct((m, e), BF16),
        scratch_shapes=[pltpu.VMEM((tm, e), BF16),
                        pltpu.VMEM((nj, e, tn), BF16)],
        compiler_params=_params(("arbitrary", "arbitrary")),
        name="s5_glu",
    )(g2d, g2d, w, b.reshape(1, e))


def _fox_cum_body(f_ref, o_ref, *, blk):
    seq = f_ref.shape[0]
    row = lax.broadcasted_iota(jnp.int32, (blk, blk), 0)
    col = lax.broadcasted_iota(jnp.int32, (blk, blk), 1)
    tri = (col <= row).astype(F32)
    carry = jnp.zeros((1, LANES), F32)
    for i in range(seq // blk):
        x = f_ref[i * blk:(i + 1) * blk, :]
        ls = jnp.minimum(x, 0.0) - jnp.log1p(jnp.exp(-jnp.abs(x)))
        cum = jnp.dot(tri, ls, preferred_element_type=F32,
                      precision=lax.Precision.HIGHEST) + carry
        o_ref[i * blk:(i + 1) * blk, :] = cum
        carry = cum[blk - 1:blk, :]


def _fox_cum(f3d, *, blk):
    b, seq, _ = f3d.shape
    return pl.pallas_call(
        functools.partial(_fox_cum_body, blk=blk),
        grid=(b,),
        in_specs=[pl.BlockSpec((None, seq, LANES), lambda i: (i, 0, 0))],
        out_specs=pl.BlockSpec((None, seq, LANES), lambda i: (i, 0, 0)),
        out_shape=jax.ShapeDtypeStruct((b, seq, LANES), F32),
        compiler_params=_params(("parallel",)),
        name="fox_cum",
    )(f3d)


def _fox_attn_body(q_ref, k_ref, v_ref, ck_ref, qw_ref, kw_ref, o_ref,
                   qn_ref, kn_ref, m_ref, l_ref, acc_ref, *, tq, scale, row_chunk):
    seq = k_ref.shape[0]
    nq = seq // tq
    log2e = 1.4426950408889634

    def qk_norm(rows):
        qn_ref[rows, :] = (_rms_rows(q_ref[rows, :].astype(F32), qw_ref[...]) * (scale * log2e)).astype(BF16)
        kn_ref[rows, :] = _rms_rows(k_ref[rows, :].astype(F32), kw_ref[...]).astype(BF16)
    _rows_loop(seq, row_chunk, qk_norm)

    pairs = [(i, j) for i in range(nq) for j in range(i + 1)]

    def logits(i, j):
        return lax.dot_general(qn_ref[i * tq:(i + 1) * tq, :], kn_ref[j * tq:(j + 1) * tq, :],
                               (((1,), (1,)), ((), ())), preferred_element_type=F32)

    n_lt = tq // LANES
    s_next = logits(*pairs[0])
    for idx, (i, j) in enumerate(pairs):
        s = s_next
        if idx + 1 < len(pairs):
            s_next = logits(*pairs[idx + 1])
        c0 = ck_ref[i:i + 1, 0:1]
        dk = (ck_ref[j:j + 1, :] - c0) * log2e
        s_t = [s[:, t * LANES:(t + 1) * LANES] - dk[:, t * LANES:(t + 1) * LANES] for t in range(n_lt)]
        if i == j:
            r_i = lax.broadcasted_iota(jnp.int32, (tq, LANES), 0)
            c_i = lax.broadcasted_iota(jnp.int32, (tq, LANES), 1)
            s_t = [jnp.where(c_i + t * LANES <= r_i, s_t[t], NEG_BIG) for t in range(n_lt)]
        mx = functools.reduce(jnp.maximum, s_t)
        row_max = jnp.broadcast_to(jnp.max(mx, axis=-1, keepdims=True), (tq, LANES))
        v_blk = v_ref[j * tq:(j + 1) * tq, :]
        if j == 0:
            m_new = row_max
        else:
            m_old = m_ref[...]
            m_new = jnp.maximum(m_old, row_max)
        p_t = [jnp.exp2(s_t[t] - m_new) for t in range(n_lt)]
        p_sum = functools.reduce(jnp.add, p_t)
        pv = jnp.dot(jnp.concatenate([p.astype(BF16) for p in p_t], axis=1), v_blk,
                     preferred_element_type=F32)
        if j == 0:
            l_new, acc_new = p_sum, pv
        else:
            alpha = jnp.exp2(m_old - m_new)
            l_new = alpha * l_ref[...] + p_sum
            acc_new = alpha * acc_ref[...] + pv
        if j == i:
            l_row = jnp.sum(l_new, axis=-1, keepdims=True)
            o_ref[i * tq:(i + 1) * tq, :] = (acc_new / l_row).astype(BF16)
        else:
            m_ref[...] = m_new
            l_ref[...] = l_new
            acc_ref[...] = acc_new


def _fox_attn(proj3d, cum_k, q_w, k_w, *, heads, tq):
    b, seq, _ = proj3d.shape
    dh = FOX_HEAD_DIM
    nq = seq // tq
    return pl.pallas_call(
        functools.partial(_fox_attn_body, tq=tq, scale=dh ** -0.5, row_chunk=min(seq, 256)),
        grid=(b, heads),
        in_specs=[pl.BlockSpec((None, seq, dh), lambda bi, h: (bi, 0, h)),
                  pl.BlockSpec((None, seq, dh), lambda bi, h: (bi, 0, heads + h)),
                  pl.BlockSpec((None, seq, dh), lambda bi, h: (bi, 0, 2 * heads + h)),
                  pl.BlockSpec((None, None, nq, tq), lambda bi, h: (bi, h, 0, 0)),
                  pl.BlockSpec((1, dh), lambda bi, h: (0, 0)),
                  pl.BlockSpec((1, dh), lambda bi, h: (0, 0))],
        out_specs=pl.BlockSpec((None, seq, dh), lambda bi, h: (bi, 0, h)),
        out_shape=jax.ShapeDtypeStruct((b, seq, heads * dh), BF16),
        scratch_shapes=[pltpu.VMEM((seq, dh), BF16),
                        pltpu.VMEM((seq, dh), BF16),
                        pltpu.VMEM((tq, LANES), F32),
                        pltpu.VMEM((tq, LANES), F32),
                        pltpu.VMEM((tq, dh), F32)],
        compiler_params=_params(("parallel", "parallel")),
        name="fox_attn",
    )(proj3d, proj3d, proj3d, cum_k, q_w.reshape(1, dh), k_w.reshape(1, dh))


def _pool_body(u_ref, w_ref, s_ref, o_ref, sa_ref, sb_ref, *, windows):
    seq, cg = u_ref.shape
    halo = POOL_HALO
    grp = pl.program_id(1)
    zeros = jnp.zeros((halo, cg), F32)
    sa_ref[0:halo, :] = zeros
    sb_ref[0:halo, :] = zeros
    t_idx = lax.broadcasted_iota(jnp.int32, (seq, 1), 0)

    for gi, win in enumerate(windows):
        @pl.when(grp == gi)
        def _(win=win):
            u = u_ref[...]
            sa_ref[halo:halo + seq, :] = u
            bufs = (sa_ref, sb_ref)
            s = u
            shift = 1
            level = 0
            while shift < win:
                src = bufs[level % 2]
                s = src[halo:halo + seq, :] + src[halo - shift:halo - shift + seq, :]
                shift *= 2
                level += 1
                if shift < win:
                    bufs[level % 2][halo:halo + seq, :] = s
            cnt = jnp.minimum(t_idx + 1, win).astype(F32)
            diff = (s / cnt - u).astype(BF16)
            mixed = jnp.dot(diff, w_ref[...].astype(BF16), preferred_element_type=F32)
            o_ref[...] = (mixed * s_ref[...]).astype(BF16)


def _pool(u3d, w_group, scale, *, windows):
    b, seq, e = u3d.shape
    ng = len(windows)
    cg = e // ng
    return pl.pallas_call(
        functools.partial(_pool_body, windows=windows),
        grid=(b, ng),
        in_specs=[pl.BlockSpec((None, seq, cg), lambda bi, g: (bi, 0, g)),
                  pl.BlockSpec((None, cg, cg), lambda bi, g: (g, 0, 0)),
                  pl.BlockSpec((1, cg), lambda bi, g: (0, g))],
        out_specs=pl.BlockSpec((None, seq, cg), lambda bi, g: (bi, 0, g)),
        out_shape=jax.ShapeDtypeStruct((b, seq, e), BF16),
        scratch_shapes=[pltpu.VMEM((POOL_HALO + seq, cg), F32),
                        pltpu.VMEM((POOL_HALO + seq, cg), F32)],
        compiler_params=_params(("parallel", "parallel")),
        name="pool_mix",
    )(u3d, w_group, scale.reshape(1, e))


def _pick(n, pref):
    t = min(n, pref)
    while n % t:
        t //= 2
    return t


def kernel(x, norm_w, out_proj, s5_in_proj, s5_a_re, s5_a_im, s5_log_dt, s5_b_re, s5_b_im, s5_c_re, s5_c_im, s5_d, s5_w_glu, s5_b_glu, fox_in_proj, fox_q_norm, fox_k_norm, fox_f_bias, pool_in_proj, pool_w_group, pool_scale):
    bsz, seq, d = x.shape
    depth = norm_w.shape[0]
    e = s5_d.shape[1]
    heads = e // FOX_HEAD_DIM
    m = bsz * seq
    n_mixers = 3

    tm = _pick(m, 1024)
    tn = _pick(e, 512)
    tq = _pick(seq, 512)

    h = x.reshape(m, d)
    for i in range(depth):
        kind, j = i % n_mixers, i // n_mixers
        if kind == 0:
            wb, wct, toe, lam = _s5_prep(s5_a_re[j], s5_a_im[j], s5_log_dt[j], s5_b_re[j], s5_b_im[j],
                                         s5_c_re[j], s5_c_im[j], chunk=CHUNK)
            u, z = _norm_proj_split(h, norm_w[i], s5_in_proj[j], tm=tm, tn=tn)
            g = _s5_core(u, s5_d[j], wb, wct, toe, lam, chunk=CHUNK, n_seq=bsz)
            y = _glu(g, s5_w_glu[j], s5_b_glu[j], tm=tm, tn=tn)
            h = _out_proj(y, z, 0, h, out_proj[i], tm=tm, tn=tn)
        elif kind == 1:
            w_in = fox_in_proj[j]
            n_main = 4 * e
            w_f = jnp.pad(w_in[:, n_main:], ((0, 0), (0, LANES - heads)))
            b_f = jnp.pad(fox_f_bias[j], (0, LANES - heads)).reshape(1, LANES)
            proj, f_logit = _norm_proj_fox(h, norm_w[i], w_in, w_f, b_f, n_main=n_main, n_half=2 * e,
                                           tm=tm, tn=tn)
            cum = _fox_cum(f_logit.reshape(bsz, seq, LANES), blk=_pick(seq, 256))
            cum_k = cum[:, :, :heads].transpose(0, 2, 1).reshape(bsz, heads, seq // tq, tq)
            y = _fox_attn(proj.reshape(bsz, seq, n_main), cum_k, fox_q_norm[j], fox_k_norm[j],
                          heads=heads, tq=tq)
            h = _out_proj(y.reshape(m, e), proj, 3, h, out_proj[i], tm=tm, tn=tn)
        else:
            u, z = _norm_proj_split(h, norm_w[i], pool_in_proj[j], tm=tm, tn=tn)
            y = _pool(u.reshape(bsz, seq, e), pool_w_group[j], pool_scale[j], windows=POOL_WINDOWS)
            h = _out_proj(y.reshape(m, e), z, 0, h, out_proj[i], tm=tm, tn=tn)
    return h.reshape(bsz, seq, d)
```

```python
import functools
import math

import jax
import jax.numpy as jnp
from jax import lax
from jax.experimental import pallas as pl
from jax.experimental.pallas import tpu as pltpu

F32 = jnp.float32
BF16 = jnp.bfloat16

LANES = 128
S5_GROUP = 16
S5_STATE = 64
TILE_GROUPS = LANES // S5_GROUP
TILE_STATES = TILE_GROUPS * S5_STATE
CHUNK = 8
FOX_HEAD_DIM = 128
POOL_WINDOWS = (2, 4, 8, 16)
POOL_HALO = 16
EPS = 1e-6
NEG_BIG = -1e30
VMEM_LIMIT = 56 * 1024 * 1024


def _params(semantics):
    return pltpu.CompilerParams(dimension_semantics=semantics, vmem_limit_bytes=VMEM_LIMIT)


def _rms_rows(x, w):
    ms = jnp.mean(x * x, axis=-1, keepdims=True)
    return x * lax.rsqrt(ms + EPS) * w


def _silu(z):
    return z * jax.nn.sigmoid(z)


def _lhs_cached(j, make_lhs, lhs_ref, emit):
    @pl.when(j == 0)
    def _():
        lhs = make_lhs()
        lhs_ref[...] = lhs
        emit(lhs)

    @pl.when(j > 0)
    def _():
        emit(lhs_ref[...])


def _norm_proj_split_body(x_ref, nw_ref, w_ref, u_ref, z_ref, xn_ref, *, n_f32_tiles):
    j = pl.program_id(1)

    def emit(xn):
        acc = jnp.dot(xn, w_ref[...].astype(BF16), preferred_element_type=F32)

        @pl.when(j < n_f32_tiles)
        def _():
            u_ref[...] = acc

        @pl.when(j >= n_f32_tiles)
        def _():
            z_ref[...] = acc.astype(BF16)

    _lhs_cached(j, lambda: _rms_rows(x_ref[...], nw_ref[...]).astype(BF16), xn_ref, emit)


def _norm_proj_split(h2d, norm_w, w_all, layer, *, tm, tn):
    m, d = h2d.shape
    n = w_all.shape[2]
    nj = n // tn
    na = nj // 2
    return pl.pallas_call(
        functools.partial(_norm_proj_split_body, n_f32_tiles=na),
        grid=(m // tm, nj),
        in_specs=[pl.BlockSpec((tm, d), lambda i, j: (i, 0)),
                  pl.BlockSpec((1, d), lambda i, j: (0, 0)),
                  pl.BlockSpec((None, d, tn), lambda i, j: (layer, 0, j))],
        out_specs=[pl.BlockSpec((tm, tn), lambda i, j: (i, jnp.minimum(j, na - 1))),
                   pl.BlockSpec((tm, tn), lambda i, j: (i, jnp.maximum(j - na, 0)))],
        out_shape=[jax.ShapeDtypeStruct((m, n // 2), F32),
                   jax.ShapeDtypeStruct((m, n // 2), BF16)],
        scratch_shapes=[pltpu.VMEM((tm, d), BF16)],
        compiler_params=_params(("parallel", "arbitrary")),
        name="norm_proj_split",
    )(h2d, norm_w.reshape(1, d), w_all)


def _norm_proj_fox_body(x_ref, nw_ref, w_ref, wf_ref, bf_ref, o_ref, of_ref, xn_ref):
    j = pl.program_id(1)

    def emit(xn):
        o_ref[...] = jnp.dot(xn, w_ref[...].astype(BF16), preferred_element_type=F32).astype(BF16)

    def emit_first(xn):
        emit(xn)
        of_ref[...] = jnp.dot(xn, wf_ref[...].astype(BF16), preferred_element_type=F32) + bf_ref[...]

    @pl.when(j == 0)
    def _():
        xn = _rms_rows(x_ref[...], nw_ref[...]).astype(BF16)
        xn_ref[...] = xn
        emit_first(xn)

    @pl.when(j > 0)
    def _():
        emit(xn_ref[...])


def _norm_proj_fox(h2d, norm_w, w_all, layer, w_extra, b_extra, *, n_main, tm, tn):
    m, d = h2d.shape
    return pl.pallas_call(
        _norm_proj_fox_body,
        grid=(m // tm, n_main // tn),
        in_specs=[pl.BlockSpec((tm, d), lambda i, j: (i, 0)),
                  pl.BlockSpec((1, d), lambda i, j: (0, 0)),
                  pl.BlockSpec((None, d, tn), lambda i, j: (layer, 0, j)),
                  pl.BlockSpec((d, LANES), lambda i, j: (0, 0)),
                  pl.BlockSpec((1, LANES), lambda i, j: (0, 0))],
        out_specs=[pl.BlockSpec((tm, tn), lambda i, j: (i, j)),
                   pl.BlockSpec((tm, LANES), lambda i, j: (i, 0))],
        out_shape=[jax.ShapeDtypeStruct((m, n_main), BF16),
                   jax.ShapeDtypeStruct((m, LANES), F32)],
        scratch_shapes=[pltpu.VMEM((tm, d), BF16)],
        compiler_params=_params(("parallel", "arbitrary")),
        name="norm_proj_fox",
    )(h2d, norm_w.reshape(1, d), w_all, w_extra, b_extra)


def _out_proj_body(y_ref, z_ref, h_ref, w_ref, o_ref, a_ref):
    j = pl.program_id(1)

    def emit(a):
        o_ref[...] = h_ref[...] + jnp.dot(a, w_ref[...].astype(BF16), preferred_element_type=F32)

    def gate():
        return (y_ref[...].astype(F32) * _silu(z_ref[...].astype(F32))).astype(BF16)

    _lhs_cached(j, gate, a_ref, emit)


def _out_proj(y2d, z2d, z_block, h2d, w_all, layer, *, tm, tn):
    m, e = y2d.shape
    d = w_all.shape[2]
    return pl.pallas_call(
        _out_proj_body,
        grid=(m // tm, d // tn),
        in_specs=[pl.BlockSpec((tm, e), lambda i, j: (i, 0)),
                  pl.BlockSpec((tm, e), lambda i, j: (i, z_block)),
                  pl.BlockSpec((tm, tn), lambda i, j: (i, j)),
                  pl.BlockSpec((None, e, tn), lambda i, j: (layer, 0, j))],
        out_specs=pl.BlockSpec((tm, tn), lambda i, j: (i, j)),
        out_shape=jax.ShapeDtypeStruct((m, d), F32),
        scratch_shapes=[pltpu.VMEM((tm, e), BF16)],
        compiler_params=_params(("parallel", "arbitrary")),
        name="out_proj",
    )(y2d, z2d, h2d, w_all)


def _s5_prep_body(a_ref, bt_ref, c_ref, wb_ref, wct_ref, toe_ref, lam_ref, *, chunk):
    ns = TILE_STATES
    n_lt = ns // LANES

    def block_diag(x):
        x2 = jnp.concatenate([x, x], axis=1)
        grp_row = lax.broadcasted_iota(jnp.int32, (LANES, LANES), 0) // S5_GROUP
        grp_lane = lax.broadcasted_iota(jnp.int32, (LANES, LANES), 1) // S5_STATE
        per_tile = LANES // S5_STATE
        return jnp.concatenate([jnp.where(grp_row == grp_lane + per_tile * t, x2, 0.0)
                                for t in range(n_lt)], axis=1)

    ar, ai, ldt = a_ref[0:1, :], a_ref[1:2, :], a_ref[2:3, :]
    dt = jnp.exp(ldt)
    k_rows = lax.broadcasted_iota(jnp.int32, (chunk + 1, 1), 0).astype(F32)
    mag = jnp.exp(ar * dt * k_rows)
    ang = ai * dt * k_rows
    pr, pi = mag * jnp.cos(ang), mag * jnp.sin(ang)
    abar_r, abar_i = pr[1:2, :], pi[1:2, :]
    den = ar * ar + ai * ai
    xr = abar_r - 1.0
    fr = (xr * ar + abar_i * ai) / den
    fi = (abar_i * ar - xr * ai) / den
    bt_re, bt_im = block_diag(bt_ref[0]), block_diag(bt_ref[1])
    bbar_re = fr * bt_re - fi * bt_im
    bbar_im = fr * bt_im + fi * bt_re
    c_re, c_im = block_diag(c_ref[0]), block_diag(c_ref[1])
    lam_ref[:, 0:ns] = pr[chunk:chunk + 1, :]
    lam_ref[:, ns:2 * ns] = pi[chunk:chunk + 1, :]

    for tau in range(chunk):
        qr, qi = pr[tau:tau + 1, :], pi[tau:tau + 1, :]
        g_re = (bbar_re * qr - bbar_im * qi).astype(BF16)
        g_im = (bbar_re * qi + bbar_im * qr).astype(BF16)
        t = chunk - 1 - tau
        wb_ref[t * LANES:(t + 1) * LANES, 0:ns] = g_re
        wb_ref[t * LANES:(t + 1) * LANES, ns:2 * ns] = g_im
        qr, qi = pr[tau + 1:tau + 2, :], pi[tau + 1:tau + 2, :]
        wct_ref[tau * LANES:(tau + 1) * LANES, 0:ns] = (c_re * qr - c_im * qi).astype(BF16)
        wct_ref[tau * LANES:(tau + 1) * LANES, ns:2 * ns] = (-(c_re * qi + c_im * qr)).astype(BF16)

    c_cat = jnp.concatenate([c_re, -c_im], axis=1).astype(BF16)
    k_all = lax.dot_general(wb_ref[...], c_cat, (((1,), (1,)), ((), ())),
                            preferred_element_type=F32)
    zero_blk = jnp.zeros((LANES, LANES), BF16)
    for tau in range(chunk):
        t = chunk - 1 - tau
        k_tau = k_all[t * LANES:(t + 1) * LANES, :].astype(BF16)
        toe_ref[0:LANES, tau * LANES:(tau + 1) * LANES] = k_tau
        if tau + 1 < chunk:
            toe_ref[LANES:2 * LANES, (tau + 1) * LANES:(tau + 2) * LANES] = k_tau
    toe_ref[LANES:2 * LANES, 0:LANES] = zero_blk


def _s5_prep(a_re, a_im, log_dt, b_re, b_im, c_re, c_im, *, chunk):
    g, p = a_re.shape
    nt = g // TILE_GROUPS
    ns = TILE_STATES
    ldt = jnp.broadcast_to(log_dt[:, None], (g, p))
    a_rows = jnp.stack([a_re.reshape(nt, ns), a_im.reshape(nt, ns), ldt.reshape(nt, ns)], axis=1)
    bt = jnp.stack([b_re, b_im], axis=0).transpose(1, 0, 3, 2).reshape(nt, TILE_GROUPS, 2, S5_GROUP, p)
    bt = bt.transpose(0, 2, 1, 3, 4).reshape(nt, 2, LANES, p)
    cc = jnp.stack([c_re.reshape(nt, LANES, p), c_im.reshape(nt, LANES, p)], axis=1)
    kw = chunk * LANES
    return pl.pallas_call(
        functools.partial(_s5_prep_body, chunk=chunk),
        grid=(nt,),
        in_specs=[pl.BlockSpec((None, 3, ns), lambda j: (j, 0, 0)),
                  pl.BlockSpec((None, 2, LANES, p), lambda j: (j, 0, 0, 0)),
                  pl.BlockSpec((None, 2, LANES, p), lambda j: (j, 0, 0, 0))],
        out_specs=[pl.BlockSpec((None, kw, 2 * ns), lambda j: (j, 0, 0)),
                   pl.BlockSpec((None, kw, 2 * ns), lambda j: (j, 0, 0)),
                   pl.BlockSpec((None, 2 * LANES, kw), lambda j: (j, 0, 0)),
                   pl.BlockSpec((None, 1, 2 * ns), lambda j: (j, 0, 0))],
        out_shape=[jax.ShapeDtypeStruct((nt, kw, 2 * ns), BF16),
                   jax.ShapeDtypeStruct((nt, kw, 2 * ns), BF16),
                   jax.ShapeDtypeStruct((nt, 2 * LANES, kw), BF16),
                   jax.ShapeDtypeStruct((nt, 1, 2 * ns), F32)],
        compiler_params=_params(("parallel",)),
        name="s5_prep",
    )(a_rows, bt, cc)


def _s5_core_body(u_ref, d_ref, wb_ref, wct_ref, toe_ref, lam_ref, o_ref,
                  ub_ref, slab_ref, hin_ref, hprev_ref, *, chunk, n_seq, n_chunks):
    ns = TILE_STATES
    nk = ns // LANES
    seq = n_chunks * chunk

    def token_rows(b, t):
        return pl.ds(b * seq + t, n_chunks, stride=chunk)

    def chunk_rows(b):
        return pl.ds(b, n_chunks, stride=n_seq)

    for t in range(chunk):
        for b in range(n_seq):
            slab_ref[t, chunk_rows(b), :] = u_ref[token_rows(b, t), :]
        ub_ref[:, t * LANES:(t + 1) * LANES] = slab_ref[t].astype(BF16)

    h_in = jnp.dot(ub_ref[...], wb_ref[...], preferred_element_type=F32)
    for k in range(2 * nk):
        hin_ref[k] = h_in[:, k * LANES:(k + 1) * LANES]

    for a in range(chunk // 2):
        lo = 2 * a * LANES
        width = (chunk - 2 * a) * LANES
        part = jnp.dot(ub_ref[:, lo:lo + 2 * LANES], toe_ref[:, 0:width], preferred_element_type=F32)
        for i in range(chunk - 2 * a):
            if a == 0:
                slab_ref[i] = part[:, i * LANES:(i + 1) * LANES]
            else:
                slab_ref[2 * a + i] += part[:, i * LANES:(i + 1) * LANES]

    lam = [jnp.broadcast_to(lam_ref[:, k * LANES:(k + 1) * LANES], (n_seq, LANES)) for k in range(2 * nk)]
    state = [jnp.zeros((n_seq, LANES), F32) for _ in range(2 * nk)]
    for c in range(n_chunks):
        rows = pl.ds(c * n_seq, n_seq)
        new_state = []
        for k in range(nk):
            hr, hi = state[k], state[nk + k]
            hprev_ref[k, rows, :] = hr
            hprev_ref[nk + k, rows, :] = hi
            lr, li = lam[k], lam[nk + k]
            new_state.append((lr * hr - li * hi + hin_ref[k, rows, :],
                              lr * hi + li * hr + hin_ref[nk + k, rows, :]))
        state = [s[0] for s in new_state] + [s[1] for s in new_state]

    h_prev = jnp.concatenate([hprev_ref[k].astype(BF16) for k in range(2 * nk)], axis=1)
    for a in range(chunk // 2):
        lo = 2 * a * LANES
        y = lax.dot_general(h_prev, wct_ref[lo:lo + 2 * LANES, :], (((1,), (1,)), ((), ())),
                            preferred_element_type=F32)
        for i in range(2):
            t = 2 * a + i
            slab_ref[t] += y[:, i * LANES:(i + 1) * LANES]
            for b in range(n_seq):
                y_bt = slab_ref[t, chunk_rows(b), :] + d_ref[...] * u_ref[token_rows(b, t), :]
                o_ref[token_rows(b, t), :] = jax.nn.gelu(y_bt)


def _s5_core(u2d, d_skip, wb, wct, toe, lam, *, chunk, n_seq):
    m, e = u2d.shape
    r = m // chunk
    nt = e // LANES
    ns2 = 2 * TILE_STATES
    kw = chunk * LANES
    return pl.pallas_call(
        functools.partial(_s5_core_body, chunk=chunk, n_seq=n_seq, n_chunks=r // n_seq),
        grid=(nt,),
        in_specs=[pl.BlockSpec((m, LANES), lambda j: (0, j)),
                  pl.BlockSpec((1, LANES), lambda j: (0, j)),
                  pl.BlockSpec((None, kw, ns2), lambda j: (j, 0, 0)),
                  pl.BlockSpec((None, kw, ns2), lambda j: (j, 0, 0)),
                  pl.BlockSpec((None, 2 * LANES, kw), lambda j: (j, 0, 0)),
                  pl.BlockSpec((None, 1, ns2), lambda j: (j, 0, 0))],
        out_specs=pl.BlockSpec((m, LANES), lambda j: (0, j)),
        out_shape=jax.ShapeDtypeStruct((m, e), F32),
        scratch_shapes=[pltpu.VMEM((r, kw), BF16),
                        pltpu.VMEM((chunk, r, LANES), F32),
                        pltpu.VMEM((ns2 // LANES, r, LANES), F32),
                        pltpu.VMEM((ns2 // LANES, r, LANES), F32)],
        compiler_params=_params(("parallel",)),
        name="s5_core",
    )(u2d, d_skip.reshape(1, e), wb, wct, toe, lam)


def _glu_body(g_ref, gt_ref, w_ref, b_ref, o_ref, gb_ref):
    j = pl.program_id(1)

    def emit(gb):
        acc = jnp.dot(gb, w_ref[...].astype(BF16), preferred_element_type=F32) + b_ref[...]
        o_ref[...] = (gt_ref[...] * jax.nn.sigmoid(acc)).astype(BF16)

    _lhs_cached(j, lambda: g_ref[...].astype(BF16), gb_ref, emit)


def _glu(g2d, w_all, layer, b, *, tm, tn):
    m, e = g2d.shape
    return pl.pallas_call(
        _glu_body,
        grid=(m // tm, e // tn),
        in_specs=[pl.BlockSpec((tm, e), lambda i, j: (i, 0)),
                  pl.BlockSpec((tm, tn), lambda i, j: (i, j)),
                  pl.BlockSpec((None, e, tn), lambda i, j: (layer, 0, j)),
                  pl.BlockSpec((1, tn), lambda i, j: (0, j))],
        out_specs=pl.BlockSpec((tm, tn), lambda i, j: (i, j)),
        out_shape=jax.ShapeDtypeStruct((m, e), BF16),
        scratch_shapes=[pltpu.VMEM((tm, e), BF16)],
        compiler_params=_params(("parallel", "arbitrary")),
        name="s5_glu",
    )(g2d, g2d, w_all, b.reshape(1, e))


def _fox_cum_body(f_ref, o_ref, *, blk):
    seq = f_ref.shape[0]
    row = lax.broadcasted_iota(jnp.int32, (blk, blk), 0)
    col = lax.broadcasted_iota(jnp.int32, (blk, blk), 1)
    tri = (col <= row).astype(F32)
    carry = jnp.zeros((1, LANES), F32)
    for i in range(seq // blk):
        x = f_ref[i * blk:(i + 1) * blk, :]
        ls = jnp.minimum(x, 0.0) - jnp.log1p(jnp.exp(-jnp.abs(x)))
        cum = jnp.dot(tri, ls, preferred_element_type=F32,
                      precision=lax.Precision.HIGHEST) + carry
        o_ref[i * blk:(i + 1) * blk, :] = cum
        carry = cum[blk - 1:blk, :]


def _fox_cum(f3d, *, blk):
    b, seq, _ = f3d.shape
    return pl.pallas_call(
        functools.partial(_fox_cum_body, blk=blk),
        grid=(b,),
        in_specs=[pl.BlockSpec((None, seq, LANES), lambda i: (i, 0, 0))],
        out_specs=pl.BlockSpec((None, seq, LANES), lambda i: (i, 0, 0)),
        out_shape=jax.ShapeDtypeStruct((b, seq, LANES), F32),
        compiler_params=_params(("parallel",)),
        name="fox_cum",
    )(f3d)


def _fox_attn_body(q_ref, k_ref, v_ref, ck_ref, qw_ref, kw_ref, o_ref,
                   kn_ref, m_ref, l_ref, acc_ref, *, tq, scale):
    seq = k_ref.shape[0]
    nq = seq // tq
    log2e = 1.4426950408889634

    def q_block(i):
        rows = slice(i * tq, (i + 1) * tq)
        return (_rms_rows(q_ref[rows, :].astype(F32), qw_ref[...]) * (scale * log2e)).astype(BF16)

    def k_block(j):
        rows = slice(j * tq, (j + 1) * tq)
        kn_ref[rows, :] = _rms_rows(k_ref[rows, :].astype(F32), kw_ref[...]).astype(BF16)

    pairs = [(i, j) for i in range(nq) for j in range(i + 1)]
    qn = {}

    def logits(i, j):
        if j == 0:
            qn[i] = q_block(i)
        if j == i:
            k_block(j)
        return lax.dot_general(qn[i], kn_ref[j * tq:(j + 1) * tq, :],
                               (((1,), (1,)), ((), ())), preferred_element_type=F32)

    n_lt = tq // LANES
    s_next = logits(*pairs[0])
    for idx, (i, j) in enumerate(pairs):
        s = s_next
        if idx + 1 < len(pairs):
            s_next = logits(*pairs[idx + 1])
        c0 = ck_ref[i:i + 1, 0:1]
        dk = (ck_ref[j:j + 1, :] - c0) * log2e
        s_t = [s[:, t * LANES:(t + 1) * LANES] - dk[:, t * LANES:(t + 1) * LANES] for t in range(n_lt)]
        if i == j:
            r_i = lax.broadcasted_iota(jnp.int32, (tq, LANES), 0)
            c_i = lax.broadcasted_iota(jnp.int32, (tq, LANES), 1)
            s_t = [jnp.where(c_i + t * LANES <= r_i, s_t[t], NEG_BIG) for t in range(n_lt)]
        mx = functools.reduce(jnp.maximum, s_t)
        row_max = jnp.broadcast_to(jnp.max(mx, axis=-1, keepdims=True), (tq, LANES))
        v_blk = v_ref[j * tq:(j + 1) * tq, :]
        if j == 0:
            m_new = row_max
        else:
            m_old = m_ref[...]
            m_new = jnp.maximum(m_old, row_max)
        p_t = [jnp.exp2(s_t[t] - m_new) for t in range(n_lt)]
        p_sum = functools.reduce(jnp.add, p_t)
        pv = jnp.dot(jnp.concatenate([p.astype(BF16) for p in p_t], axis=1), v_blk,
                     preferred_element_type=F32)
        if j == 0:
            l_new, acc_new = p_sum, pv
        else:
            alpha = jnp.exp2(m_old - m_new)
            l_new = alpha * l_ref[...] + p_sum
            acc_new = alpha * acc_ref[...] + pv
        if j == i:
            l_row = jnp.sum(l_new, axis=-1, keepdims=True)
            o_ref[i * tq:(i + 1) * tq, :] = (acc_new / l_row).astype(BF16)
        else:
            m_ref[...] = m_new
            l_ref[...] = l_new
            acc_ref[...] = acc_new


def _fox_attn(proj3d, cum_k, q_w, k_w, *, heads, tq):
    b, seq, _ = proj3d.shape
    dh = FOX_HEAD_DIM
    nq = seq // tq
    return pl.pallas_call(
        functools.partial(_fox_attn_body, tq=tq, scale=dh ** -0.5),
        grid=(b, heads),
        in_specs=[pl.BlockSpec((None, seq, dh), lambda bi, h: (bi, 0, h)),
                  pl.BlockSpec((None, seq, dh), lambda bi, h: (bi, 0, heads + h)),
                  pl.BlockSpec((None, seq, dh), lambda bi, h: (bi, 0, 2 * heads + h)),
                  pl.BlockSpec((None, None, nq, tq), lambda bi, h: (bi, h, 0, 0)),
                  pl.BlockSpec((1, dh), lambda bi, h: (0, 0)),
                  pl.BlockSpec((1, dh), lambda bi, h: (0, 0))],
        out_specs=pl.BlockSpec((None, seq, dh), lambda bi, h: (bi, 0, h)),
        out_shape=jax.ShapeDtypeStruct((b, seq, heads * dh), BF16),
        scratch_shapes=[pltpu.VMEM((seq, dh), BF16),
                        pltpu.VMEM((tq, LANES), F32),
                        pltpu.VMEM((tq, LANES), F32),
                        pltpu.VMEM((tq, dh), F32)],
        compiler_params=_params(("parallel", "parallel")),
        name="fox_attn",
    )(proj3d, proj3d, proj3d, cum_k, q_w.reshape(1, dh), k_w.reshape(1, dh))


def _pool_body(u_ref, w_ref, s_ref, o_ref, sa_ref, sb_ref, *, windows):
    seq, cg = u_ref.shape
    halo = POOL_HALO
    grp = pl.program_id(1)
    zeros = jnp.zeros((halo, cg), F32)
    sa_ref[0:halo, :] = zeros
    sb_ref[0:halo, :] = zeros
    t_idx = lax.broadcasted_iota(jnp.int32, (seq, 1), 0)

    for gi, win in enumerate(windows):
        @pl.when(grp == gi)
        def _(win=win):
            u = u_ref[...]
            sa_ref[halo:halo + seq, :] = u
            bufs = (sa_ref, sb_ref)
            s = u
            shift = 1
            level = 0
            while shift < win:
                src = bufs[level % 2]
                s = src[halo:halo + seq, :] + src[halo - shift:halo - shift + seq, :]
                shift *= 2
                level += 1
                if shift < win:
                    bufs[level % 2][halo:halo + seq, :] = s
            cnt = jnp.minimum(t_idx + 1, win).astype(F32)
            diff = (s / cnt - u).astype(BF16)
            mixed = jnp.dot(diff, w_ref[...].astype(BF16), preferred_element_type=F32)
            o_ref[...] = (mixed * s_ref[...]).astype(BF16)


def _pool(u3d, w_group_all, layer, scale, *, windows):
    b, seq, e = u3d.shape
    ng = len(windows)
    cg = e // ng
    return pl.pallas_call(
        functools.partial(_pool_body, windows=windows),
        grid=(b, ng),
        in_specs=[pl.BlockSpec((None, seq, cg), lambda bi, g: (bi, 0, g)),
                  pl.BlockSpec((None, None, cg, cg), lambda bi, g: (layer, g, 0, 0)),
                  pl.BlockSpec((1, cg), lambda bi, g: (0, g))],
        out_specs=pl.BlockSpec((None, seq, cg), lambda bi, g: (bi, 0, g)),
        out_shape=jax.ShapeDtypeStruct((b, seq, e), BF16),
        scratch_shapes=[pltpu.VMEM((POOL_HALO + seq, cg), F32),
                        pltpu.VMEM((POOL_HALO + seq, cg), F32)],
        compiler_params=_params(("parallel", "parallel")),
        name="pool_mix",
    )(u3d, w_group_all, scale.reshape(1, e))


def _pick(n, pref):
    t = min(n, pref)
    while n % t:
        t //= 2
    return t


def kernel(x, norm_w, out_proj, s5_in_proj, s5_a_re, s5_a_im, s5_log_dt, s5_b_re, s5_b_im, s5_c_re, s5_c_im, s5_d, s5_w_glu, s5_b_glu, fox_in_proj, fox_q_norm, fox_k_norm, fox_f_bias, pool_in_proj, pool_w_group, pool_scale):
    bsz, seq, d = x.shape
    depth = norm_w.shape[0]
    e = s5_d.shape[1]
    heads = e // FOX_HEAD_DIM
    m = bsz * seq
    n_mixers = 3

    tm = _pick(m, 1024)
    tn = _pick(e, 512)
    tn_fox = _pick(e, 1024)
    tq = _pick(seq, 512)

    h = x.reshape(m, d)
    for i in range(depth):
        kind, j = i % n_mixers, i // n_mixers
        if kind == 0:
            wb, wct, toe, lam = _s5_prep(s5_a_re[j], s5_a_im[j], s5_log_dt[j], s5_b_re[j], s5_b_im[j],
                                         s5_c_re[j], s5_c_im[j], chunk=CHUNK)
            u, z = _norm_proj_split(h, norm_w[i], s5_in_proj, j, tm=tm, tn=tn)
            g = _s5_core(u, s5_d[j], wb, wct, toe, lam, chunk=CHUNK, n_seq=bsz)
            y = _glu(g, s5_w_glu, j, s5_b_glu[j], tm=tm, tn=tn)
            h = _out_proj(y, z, 0, h, out_proj, i, tm=tm, tn=tn)
        elif kind == 1:
            n_main = 4 * e
            w_f = jnp.pad(fox_in_proj[j][:, n_main:], ((0, 0), (0, LANES - heads)))
            b_f = jnp.pad(fox_f_bias[j], (0, LANES - heads)).reshape(1, LANES)
            proj, f_logit = _norm_proj_fox(h, norm_w[i], fox_in_proj, j, w_f, b_f, n_main=n_main, tm=tm,
                                           tn=tn_fox)
            cum = _fox_cum(f_logit.reshape(bsz, seq, LANES), blk=_pick(seq, 256))
            cum_k = cum[:, :, :heads].transpose(0, 2, 1).reshape(bsz, heads, seq // tq, tq)
            y = _fox_attn(proj.reshape(bsz, seq, n_main), cum_k, fox_q_norm[j], fox_k_norm[j],
                          heads=heads, tq=tq)
            h = _out_proj(y.reshape(m, e), proj, 3, h, out_proj, i, tm=tm, tn=tn)
        else:
            u, z = _norm_proj_split(h, norm_w[i], pool_in_proj, j, tm=tm, tn=tn)
            y = _pool(u.reshape(bsz, seq, e), pool_w_group, j, pool_scale[j], windows=POOL_WINDOWS)
            h = _out_proj(y.reshape(m, e), z, 0, h, out_proj, i, tm=tm, tn=tn)
    return h.reshape(bsz, seq, d)
```

```python
import functools
import math

import jax
import jax.numpy as jnp
from jax import lax
from jax.experimental import pallas as pl
from jax.experimental.pallas import tpu as pltpu

F32 = jnp.float32
BF16 = jnp.bfloat16

LANES = 128
S5_GROUP = 16
S5_STATE = 64
TILE_GROUPS = LANES // S5_GROUP
TILE_STATES = TILE_GROUPS * S5_STATE
CHUNK = 8
FOX_HEAD_DIM = 128
POOL_WINDOWS = (2, 4, 8, 16)
POOL_HALO = 16
EPS = 1e-6
NEG_BIG = -1e30
VMEM_LIMIT = 56 * 1024 * 1024


def _params(semantics):
    return pltpu.CompilerParams(dimension_semantics=semantics, vmem_limit_bytes=VMEM_LIMIT)


def _rms_rows(x, w):
    ms = jnp.mean(x * x, axis=-1, keepdims=True)
    return x * lax.rsqrt(ms + EPS) * w


def _silu(z):
    return z * jax.nn.sigmoid(z)


def _lhs_cached(j, make_lhs, lhs_ref, emit):
    @pl.when(j == 0)
    def _():
        lhs = make_lhs()
        lhs_ref[...] = lhs
        emit(lhs)

    @pl.when(j > 0)
    def _():
        emit(lhs_ref[...])


def _frozen_tile(i, j, n_tiles):
    return jnp.where(i == 0, j, n_tiles - 1)


def _norm_proj_body(x_ref, nw_ref, w_ref, *rest, has_extra):
    if has_extra:
        wf_ref, bf_ref, o_ref, of_ref, xn_ref = rest
    else:
        o_ref, xn_ref = rest
    j = pl.program_id(1)

    def emit(xn):
        o_ref[...] = jnp.dot(xn, w_ref[...].astype(BF16), preferred_element_type=F32).astype(BF16)

    @pl.when(j == 0)
    def _():
        xn = _rms_rows(x_ref[...], nw_ref[...]).astype(BF16)
        xn_ref[...] = xn
        emit(xn)
        if has_extra:
            of_ref[...] = jnp.dot(xn, wf_ref[...].astype(BF16), preferred_element_type=F32) + bf_ref[...]

    @pl.when(j > 0)
    def _():
        emit(xn_ref[...])


def _norm_proj(h2d, norm_w, w_all, layer, *, n_out, tm, tn, extra=None):
    m, d = h2d.shape
    in_specs = [pl.BlockSpec((tm, d), lambda i, j: (i, 0)),
                pl.BlockSpec((1, d), lambda i, j: (0, 0)),
                pl.BlockSpec((None, d, tn), lambda i, j: (layer, 0, j))]
    out_specs = [pl.BlockSpec((tm, tn), lambda i, j: (i, j))]
    out_shape = [jax.ShapeDtypeStruct((m, n_out), BF16)]
    args = [h2d, norm_w.reshape(1, d), w_all]
    if extra is not None:
        in_specs += [pl.BlockSpec((d, LANES), lambda i, j: (0, 0)),
                     pl.BlockSpec((1, LANES), lambda i, j: (0, 0))]
        out_specs += [pl.BlockSpec((tm, LANES), lambda i, j: (i, 0))]
        out_shape += [jax.ShapeDtypeStruct((m, LANES), F32)]
        args += list(extra)
    outs = pl.pallas_call(
        functools.partial(_norm_proj_body, has_extra=extra is not None),
        grid=(m // tm, n_out // tn),
        in_specs=in_specs,
        out_specs=out_specs,
        out_shape=out_shape,
        scratch_shapes=[pltpu.VMEM((tm, d), BF16)],
        compiler_params=_params(("parallel", "arbitrary")),
        name="norm_proj",
    )(*args)
    return outs if extra is not None else outs[0]


def _out_proj_body(y_ref, z_ref, h_ref, w_ref, o_ref, a_ref, wbf_ref):
    i, j = pl.program_id(0), pl.program_id(1)

    @pl.when(i == 0)
    def _():
        wbf_ref[j] = w_ref[...].astype(BF16)

    def emit(a):
        o_ref[...] = h_ref[...] + jnp.dot(a, wbf_ref[j], preferred_element_type=F32)

    def gate():
        return (y_ref[...].astype(F32) * _silu(z_ref[...].astype(F32))).astype(BF16)

    _lhs_cached(j, gate, a_ref, emit)


def _out_proj(y2d, z2d, z_block, h2d, w_all, layer, *, tm, tn):
    m, e = y2d.shape
    d = w_all.shape[2]
    nj = d // tn
    return pl.pallas_call(
        _out_proj_body,
        grid=(m // tm, nj),
        in_specs=[pl.BlockSpec((tm, e), lambda i, j: (i, 0)),
                  pl.BlockSpec((tm, e), lambda i, j: (i, z_block)),
                  pl.BlockSpec((tm, tn), lambda i, j: (i, j)),
                  pl.BlockSpec((None, e, tn), lambda i, j: (layer, 0, _frozen_tile(i, j, nj)))],
        out_specs=pl.BlockSpec((tm, tn), lambda i, j: (i, j)),
        out_shape=jax.ShapeDtypeStruct((m, d), F32),
        scratch_shapes=[pltpu.VMEM((tm, e), BF16),
                        pltpu.VMEM((nj, e, tn), BF16)],
        compiler_params=_params(("arbitrary", "arbitrary")),
        name="out_proj",
    )(y2d, z2d, h2d, w_all)


def _s5_prep_body(a_ref, bt_ref, c_ref, wb_ref, wct_ref, toe_ref, lam_ref, *, chunk):
    ns = TILE_STATES
    n_lt = ns // LANES

    def block_diag(x):
        x2 = jnp.concatenate([x, x], axis=1)
        grp_row = lax.broadcasted_iota(jnp.int32, (LANES, LANES), 0) // S5_GROUP
        grp_lane = lax.broadcasted_iota(jnp.int32, (LANES, LANES), 1) // S5_STATE
        per_tile = LANES // S5_STATE
        return jnp.concatenate([jnp.where(grp_row == grp_lane + per_tile * t, x2, 0.0)
                                for t in range(n_lt)], axis=1)

    ar, ai, ldt = a_ref[0:1, :], a_ref[1:2, :], a_ref[2:3, :]
    dt = jnp.exp(ldt)
    k_rows = lax.broadcasted_iota(jnp.int32, (chunk + 1, 1), 0).astype(F32)
    mag = jnp.exp(ar * dt * k_rows)
    ang = ai * dt * k_rows
    pr, pi = mag * jnp.cos(ang), mag * jnp.sin(ang)
    abar_r, abar_i = pr[1:2, :], pi[1:2, :]
    den = ar * ar + ai * ai
    xr = abar_r - 1.0
    fr = (xr * ar + abar_i * ai) / den
    fi = (abar_i * ar - xr * ai) / den
    bt_re, bt_im = block_diag(bt_ref[0]), block_diag(bt_ref[1])
    bbar_re = fr * bt_re - fi * bt_im
    bbar_im = fr * bt_im + fi * bt_re
    c_re, c_im = block_diag(c_ref[0]), block_diag(c_ref[1])
    lam_ref[:, 0:ns] = pr[chunk:chunk + 1, :]
    lam_ref[:, ns:2 * ns] = pi[chunk:chunk + 1, :]

    for tau in range(chunk):
        qr, qi = pr[tau:tau + 1, :], pi[tau:tau + 1, :]
        g_re = (bbar_re * qr - bbar_im * qi).astype(BF16)
        g_im = (bbar_re * qi + bbar_im * qr).astype(BF16)
        t = chunk - 1 - tau
        wb_ref[t * LANES:(t + 1) * LANES, 0:ns] = g_re
        wb_ref[t * LANES:(t + 1) * LANES, ns:2 * ns] = g_im
        qr, qi = pr[tau + 1:tau + 2, :], pi[tau + 1:tau + 2, :]
        wct_ref[tau * LANES:(tau + 1) * LANES, 0:ns] = (c_re * qr - c_im * qi).astype(BF16)
        wct_ref[tau * LANES:(tau + 1) * LANES, ns:2 * ns] = (-(c_re * qi + c_im * qr)).astype(BF16)

    c_cat = jnp.concatenate([c_re, -c_im], axis=1).astype(BF16)
    k_all = lax.dot_general(wb_ref[...], c_cat, (((1,), (1,)), ((), ())),
                            preferred_element_type=F32)
    zero_blk = jnp.zeros((LANES, LANES), BF16)
    for tau in range(chunk):
        t = chunk - 1 - tau
        k_tau = k_all[t * LANES:(t + 1) * LANES, :].astype(BF16)
        toe_ref[0:LANES, tau * LANES:(tau + 1) * LANES] = k_tau
        if tau + 1 < chunk:
            toe_ref[LANES:2 * LANES, (tau + 1) * LANES:(tau + 2) * LANES] = k_tau
    toe_ref[LANES:2 * LANES, 0:LANES] = zero_blk


def _s5_prep(a_re, a_im, log_dt, b_re, b_im, c_re, c_im, *, chunk):
    g, p = a_re.shape
    nt = g // TILE_GROUPS
    ns = TILE_STATES
    ldt = jnp.broadcast_to(log_dt[:, None], (g, p))
    a_rows = jnp.stack([a_re.reshape(nt, ns), a_im.reshape(nt, ns), ldt.reshape(nt, ns)], axis=1)
    bt = jnp.stack([b_re, b_im], axis=0).transpose(1, 0, 3, 2).reshape(nt, TILE_GROUPS, 2, S5_GROUP, p)
    bt = bt.transpose(0, 2, 1, 3, 4).reshape(nt, 2, LANES, p)
    cc = jnp.stack([c_re.reshape(nt, LANES, p), c_im.reshape(nt, LANES, p)], axis=1)
    kw = chunk * LANES
    return pl.pallas_call(
        functools.partial(_s5_prep_body, chunk=chunk),
        grid=(nt,),
        in_specs=[pl.BlockSpec((None, 3, ns), lambda j: (j, 0, 0)),
                  pl.BlockSpec((None, 2, LANES, p), lambda j: (j, 0, 0, 0)),
                  pl.BlockSpec((None, 2, LANES, p), lambda j: (j, 0, 0, 0))],
        out_specs=[pl.BlockSpec((None, kw, 2 * ns), lambda j: (j, 0, 0)),
                   pl.BlockSpec((None, kw, 2 * ns), lambda j: (j, 0, 0)),
                   pl.BlockSpec((None, 2 * LANES, kw), lambda j: (j, 0, 0)),
                   pl.BlockSpec((None, 1, 2 * ns), lambda j: (j, 0, 0))],
        out_shape=[jax.ShapeDtypeStruct((nt, kw, 2 * ns), BF16),
                   jax.ShapeDtypeStruct((nt, kw, 2 * ns), BF16),
                   jax.ShapeDtypeStruct((nt, 2 * LANES, kw), BF16),
                   jax.ShapeDtypeStruct((nt, 1, 2 * ns), F32)],
        compiler_params=_params(("parallel",)),
        name="s5_prep",
    )(a_rows, bt, cc)


def _s5_core_body(ubf_ref, d_ref, wb_ref, wct_ref, toe_ref, lam_ref, o_ref,
                  u_ref, ub_ref, slab_ref, hin_ref, hprev_ref, *, chunk, n_seq, n_chunks):
    ns = TILE_STATES
    nk = ns // LANES
    seq = n_chunks * chunk

    def token_rows(b, t):
        return pl.ds(b * seq + t, n_chunks, stride=chunk)

    def chunk_rows(b):
        return pl.ds(b, n_chunks, stride=n_seq)

    u_ref[...] = ubf_ref[...].astype(F32)
    for t in range(chunk):
        for b in range(n_seq):
            slab_ref[t, chunk_rows(b), :] = u_ref[token_rows(b, t), :]
        ub_ref[:, t * LANES:(t + 1) * LANES] = slab_ref[t].astype(BF16)

    h_in = jnp.dot(ub_ref[...], wb_ref[...], preferred_element_type=F32)
    for k in range(2 * nk):
        hin_ref[k] = h_in[:, k * LANES:(k + 1) * LANES]

    for a in range(chunk // 2):
        lo = 2 * a * LANES
        width = (chunk - 2 * a) * LANES
        part = jnp.dot(ub_ref[:, lo:lo + 2 * LANES], toe_ref[:, 0:width], preferred_element_type=F32)
        for i in range(chunk - 2 * a):
            if a == 0:
                slab_ref[i] = part[:, i * LANES:(i + 1) * LANES]
            else:
                slab_ref[2 * a + i] += part[:, i * LANES:(i + 1) * LANES]

    lam = [jnp.broadcast_to(lam_ref[:, k * LANES:(k + 1) * LANES], (n_seq, LANES)) for k in range(2 * nk)]
    state = [jnp.zeros((n_seq, LANES), F32) for _ in range(2 * nk)]
    for c in range(n_chunks):
        rows = pl.ds(c * n_seq, n_seq)
        new_state = []
        for k in range(nk):
            hr, hi = state[k], state[nk + k]
            hprev_ref[k, rows, :] = hr
            hprev_ref[nk + k, rows, :] = hi
            lr, li = lam[k], lam[nk + k]
            new_state.append((lr * hr - li * hi + hin_ref[k, rows, :],
                              lr * hi + li * hr + hin_ref[nk + k, rows, :]))
        state = [s[0] for s in new_state] + [s[1] for s in new_state]

    h_prev = jnp.concatenate([hprev_ref[k].astype(BF16) for k in range(2 * nk)], axis=1)
    for a in range(chunk // 2):
        lo = 2 * a * LANES
        y = lax.dot_general(h_prev, wct_ref[lo:lo + 2 * LANES, :], (((1,), (1,)), ((), ())),
                            preferred_element_type=F32)
        for i in range(2):
            t = 2 * a + i
            slab_ref[t] += y[:, i * LANES:(i + 1) * LANES]
            for b in range(n_seq):
                y_bt = slab_ref[t, chunk_rows(b), :] + d_ref[...] * u_ref[token_rows(b, t), :]
                o_ref[token_rows(b, t), :] = jax.nn.gelu(y_bt)


def _s5_core(proj2d, d_skip, wb, wct, toe, lam, *, chunk, n_seq):
    m, n2 = proj2d.shape
    e = n2 // 2
    r = m // chunk
    nt = e // LANES
    ns2 = 2 * TILE_STATES
    kw = chunk * LANES
    return pl.pallas_call(
        functools.partial(_s5_core_body, chunk=chunk, n_seq=n_seq, n_chunks=r // n_seq),
        grid=(nt,),
        in_specs=[pl.BlockSpec((m, LANES), lambda j: (0, j)),
                  pl.BlockSpec((1, LANES), lambda j: (0, j)),
                  pl.BlockSpec((None, kw, ns2), lambda j: (j, 0, 0)),
                  pl.BlockSpec((None, kw, ns2), lambda j: (j, 0, 0)),
                  pl.BlockSpec((None, 2 * LANES, kw), lambda j: (j, 0, 0)),
                  pl.BlockSpec((None, 1, ns2), lambda j: (j, 0, 0))],
        out_specs=pl.BlockSpec((m, LANES), lambda j: (0, j)),
        out_shape=jax.ShapeDtypeStruct((m, e), F32),
        scratch_shapes=[pltpu.VMEM((m, LANES), F32),
                        pltpu.VMEM((r, kw), BF16),
                        pltpu.VMEM((chunk, r, LANES), F32),
                        pltpu.VMEM((ns2 // LANES, r, LANES), F32),
                        pltpu.VMEM((ns2 // LANES, r, LANES), F32)],
        compiler_params=_params(("parallel",)),
        name="s5_core",
    )(proj2d, d_skip.reshape(1, e), wb, wct, toe, lam)


def _glu_body(g_ref, gt_ref, w_ref, b_ref, o_ref, gb_ref, wbf_ref):
    i, j = pl.program_id(0), pl.program_id(1)

    @pl.when(i == 0)
    def _():
        wbf_ref[j] = w_ref[...].astype(BF16)

    def emit(gb):
        acc = jnp.dot(gb, wbf_ref[j], preferred_element_type=F32) + b_ref[...]
        o_ref[...] = (gt_ref[...] * jax.nn.sigmoid(acc)).astype(BF16)

    _lhs_cached(j, lambda: g_ref[...].astype(BF16), gb_ref, emit)


def _glu(g2d, w_all, layer, b, *, tm, tn):
    m, e = g2d.shape
    nj = e // tn
    return pl.pallas_call(
        _glu_body,
        grid=(m // tm, nj),
        in_specs=[pl.BlockSpec((tm, e), lambda i, j: (i, 0)),
                  pl.BlockSpec((tm, tn), lambda i, j: (i, j)),
                  pl.BlockSpec((None, e, tn), lambda i, j: (layer, 0, _frozen_tile(i, j, nj))),
                  pl.BlockSpec((1, tn), lambda i, j: (0, j))],
        out_specs=pl.BlockSpec((tm, tn), lambda i, j: (i, j)),
        out_shape=jax.ShapeDtypeStruct((m, e), BF16),
        scratch_shapes=[pltpu.VMEM((tm, e), BF16),
                        pltpu.VMEM((nj, e, tn), BF16)],
        compiler_params=_params(("arbitrary", "arbitrary")),
        name="s5_glu",
    )(g2d, g2d, w_all, b.reshape(1, e))


def _fox_cum_body(f_ref, o_ref, *, blk):
    seq = f_ref.shape[0]
    row = lax.broadcasted_iota(jnp.int32, (blk, blk), 0)
    col = lax.broadcasted_iota(jnp.int32, (blk, blk), 1)
    tri = (col <= row).astype(F32)
    carry = jnp.zeros((1, LANES), F32)
    for i in range(seq // blk):
        x = f_ref[i * blk:(i + 1) * blk, :]
        ls = jnp.minimum(x, 0.0) - jnp.log1p(jnp.exp(-jnp.abs(x)))
        cum = jnp.dot(tri, ls, preferred_element_type=F32,
                      precision=lax.Precision.HIGHEST) + carry
        o_ref[i * blk:(i + 1) * blk, :] = cum
        carry = cum[blk - 1:blk, :]


def _fox_cum(f3d, *, blk):
    b, seq, _ = f3d.shape
    return pl.pallas_call(
        functools.partial(_fox_cum_body, blk=blk),
        grid=(b,),
        in_specs=[pl.BlockSpec((None, seq, LANES), lambda i: (i, 0, 0))],
        out_specs=pl.BlockSpec((None, seq, LANES), lambda i: (i, 0, 0)),
        out_shape=jax.ShapeDtypeStruct((b, seq, LANES), F32),
        compiler_params=_params(("parallel",)),
        name="fox_cum",
    )(f3d)


def _fox_attn_body(q_ref, k_ref, v_ref, ck_ref, qw_ref, kw_ref, o_ref,
                   kn_ref, m_ref, l_ref, acc_ref, *, tq, scale):
    seq = k_ref.shape[0]
    nq = seq // tq
    log2e = 1.4426950408889634

    def q_block(i):
        rows = slice(i * tq, (i + 1) * tq)
        return (_rms_rows(q_ref[rows, :].astype(F32), qw_ref[...]) * (scale * log2e)).astype(BF16)

    def k_block(j):
        rows = slice(j * tq, (j + 1) * tq)
        kn_ref[rows, :] = _rms_rows(k_ref[rows, :].astype(F32), kw_ref[...]).astype(BF16)

    pairs = [(i, j) for i in range(nq) for j in range(i + 1)]
    qn = {}

    def logits(i, j):
        if j == 0:
            qn[i] = q_block(i)
        if j == i:
            k_block(j)
        return lax.dot_general(qn[i], kn_ref[j * tq:(j + 1) * tq, :],
                               (((1,), (1,)), ((), ())), preferred_element_type=F32)

    n_lt = tq // LANES
    s_next = logits(*pairs[0])
    for idx, (i, j) in enumerate(pairs):
        s = s_next
        if idx + 1 < len(pairs):
            s_next = logits(*pairs[idx + 1])
        c0 = ck_ref[i:i + 1, 0:1]
        dk = (ck_ref[j:j + 1, :] - c0) * log2e
        s_t = [s[:, t * LANES:(t + 1) * LANES] - dk[:, t * LANES:(t + 1) * LANES] for t in range(n_lt)]
        if i == j:
            r_i = lax.broadcasted_iota(jnp.int32, (tq, LANES), 0)
            c_i = lax.broadcasted_iota(jnp.int32, (tq, LANES), 1)
            s_t = [jnp.where(c_i + t * LANES <= r_i, s_t[t], NEG_BIG) for t in range(n_lt)]
        mx = functools.reduce(jnp.maximum, s_t)
        row_max = jnp.broadcast_to(jnp.max(mx, axis=-1, keepdims=True), (tq, LANES))
        v_blk = v_ref[j * tq:(j + 1) * tq, :]
        if j == 0:
            m_new = row_max
        else:
            m_old = m_ref[...]
            m_new = jnp.maximum(m_old, row_max)
        p_t = [jnp.exp2(s_t[t] - m_new) for t in range(n_lt)]
        p_sum = functools.reduce(jnp.add, p_t)
        pv = jnp.dot(jnp.concatenate([p.astype(BF16) for p in p_t], axis=1), v_blk,
                     preferred_element_type=F32)
        if j == 0:
            l_new, acc_new = p_sum, pv
        else:
            alpha = jnp.exp2(m_old - m_new)
            l_new = alpha * l_ref[...] + p_sum
            acc_new = alpha * acc_ref[...] + pv
        if j == i:
            l_row = jnp.sum(l_new, axis=-1, keepdims=True)
            o_ref[i * tq:(i + 1) * tq, :] = (acc_new / l_row).astype(BF16)
        else:
            m_ref[...] = m_new
            l_ref[...] = l_new
            acc_ref[...] = acc_new


def _fox_attn(proj3d, cum_k, q_w, k_w, *, heads, tq):
    b, seq, _ = proj3d.shape
    dh = FOX_HEAD_DIM
    nq = seq // tq
    return pl.pallas_call(
        functools.partial(_fox_attn_body, tq=tq, scale=dh ** -0.5),
        grid=(b, heads),
        in_specs=[pl.BlockSpec((None, seq, dh), lambda bi, h: (bi, 0, h)),
                  pl.BlockSpec((None, seq, dh), lambda bi, h: (bi, 0, heads + h)),
                  pl.BlockSpec((None, seq, dh), lambda bi, h: (bi, 0, 2 * heads + h)),
                  pl.BlockSpec((None, None, nq, tq), lambda bi, h: (bi, h, 0, 0)),
                  pl.BlockSpec((1, dh), lambda bi, h: (0, 0)),
                  pl.BlockSpec((1, dh), lambda bi, h: (0, 0))],
        out_specs=pl.BlockSpec((None, seq, dh), lambda bi, h: (bi, 0, h)),
        out_shape=jax.ShapeDtypeStruct((b, seq, heads * dh), BF16),
        scratch_shapes=[pltpu.VMEM((seq, dh), BF16),
                        pltpu.VMEM((tq, LANES), F32),
                        pltpu.VMEM((tq, LANES), F32),
                        pltpu.VMEM((tq, dh), F32)],
        compiler_params=_params(("parallel", "parallel")),
        name="fox_attn",
    )(proj3d, proj3d, proj3d, cum_k, q_w.reshape(1, dh), k_w.reshape(1, dh))


def _pool_body(u_ref, w_ref, s_ref, o_ref, sa_ref, sb_ref, *, windows):
    seq, cg = u_ref.shape
    halo = POOL_HALO
    grp = pl.program_id(1)
    zeros = jnp.zeros((halo, cg), F32)
    sa_ref[0:halo, :] = zeros
    sb_ref[0:halo, :] = zeros
    t_idx = lax.broadcasted_iota(jnp.int32, (seq, 1), 0)

    for gi, win in enumerate(windows):
        @pl.when(grp == gi)
        def _(win=win):
            u = u_ref[...].astype(F32)
            sa_ref[halo:halo + seq, :] = u
            bufs = (sa_ref, sb_ref)
            s = u
            shift = 1
            level = 0
            while shift < win:
                src = bufs[level % 2]
                s = src[halo:halo + seq, :] + src[halo - shift:halo - shift + seq, :]
                shift *= 2
                level += 1
                if shift < win:
                    bufs[level % 2][halo:halo + seq, :] = s
            cnt = jnp.minimum(t_idx + 1, win).astype(F32)
            diff = (s / cnt - u).astype(BF16)
            mixed = jnp.dot(diff, w_ref[...].astype(BF16), preferred_element_type=F32)
            o_ref[...] = (mixed * s_ref[...]).astype(BF16)


def _pool(proj3d, w_group_all, layer, scale, *, windows):
    b, seq, n2 = proj3d.shape
    e = n2 // 2
    ng = len(windows)
    cg = e // ng
    return pl.pallas_call(
        functools.partial(_pool_body, windows=windows),
        grid=(b, ng),
        in_specs=[pl.BlockSpec((None, seq, cg), lambda bi, g: (bi, 0, g)),
                  pl.BlockSpec((None, None, cg, cg), lambda bi, g: (layer, g, 0, 0)),
                  pl.BlockSpec((1, cg), lambda bi, g: (0, g))],
        out_specs=pl.BlockSpec((None, seq, cg), lambda bi, g: (bi, 0, g)),
        out_shape=jax.ShapeDtypeStruct((b, seq, e), BF16),
        scratch_shapes=[pltpu.VMEM((POOL_HALO + seq, cg), F32),
                        pltpu.VMEM((POOL_HALO + seq, cg), F32)],
        compiler_params=_params(("parallel", "parallel")),
        name="pool_mix",
    )(proj3d, w_group_all, scale.reshape(1, e))


def _pick(n, pref):
    t = min(n, pref)
    while n % t:
        t //= 2
    return t


def kernel(x, norm_w, out_proj, s5_in_proj, s5_a_re, s5_a_im, s5_log_dt, s5_b_re, s5_b_im, s5_c_re, s5_c_im, s5_d, s5_w_glu, s5_b_glu, fox_in_proj, fox_q_norm, fox_k_norm, fox_f_bias, pool_in_proj, pool_w_group, pool_scale):
    bsz, seq, d = x.shape
    depth = norm_w.shape[0]
    e = s5_d.shape[1]
    heads = e // FOX_HEAD_DIM
    m = bsz * seq
    n_mixers = 3

    tm = _pick(m, 1024)
    tn_in = _pick(2 * e, 1024)
    tn_out = _pick(e, 512)
    tq = _pick(seq, 512)

    h = x.reshape(m, d)
    for i in range(depth):
        kind, j = i % n_mixers, i // n_mixers
        if kind == 0:
            wb, wct, toe, lam = _s5_prep(s5_a_re[j], s5_a_im[j], s5_log_dt[j], s5_b_re[j], s5_b_im[j],
                                         s5_c_re[j], s5_c_im[j], chunk=CHUNK)
            proj = _norm_proj(h, norm_w[i], s5_in_proj, j, n_out=2 * e, tm=tm, tn=tn_in)
            g = _s5_core(proj, s5_d[j], wb, wct, toe, lam, chunk=CHUNK, n_seq=bsz)
            y = _glu(g, s5_w_glu, j, s5_b_glu[j], tm=tm, tn=tn_out)
            h = _out_proj(y, proj, 1, h, out_proj, i, tm=tm, tn=tn_out)
        elif kind == 1:
            n_main = 4 * e
            w_f = jnp.pad(fox_in_proj[j][:, n_main:], ((0, 0), (0, LANES - heads)))
            b_f = jnp.pad(fox_f_bias[j], (0, LANES - heads)).reshape(1, LANES)
            proj, f_logit = _norm_proj(h, norm_w[i], fox_in_proj, j, n_out=n_main, tm=tm, tn=tn_in,
                                       extra=(w_f, b_f))
            cum = _fox_cum(f_logit.reshape(bsz, seq, LANES), blk=_pick(seq, 256))
            cum_k = cum[:, :, :heads].transpose(0, 2, 1).reshape(bsz, heads, seq // tq, tq)
            y = _fox_attn(proj.reshape(bsz, seq, n_main), cum_k, fox_q_norm[j], fox_k_norm[j],
                          heads=heads, tq=tq)
            h = _out_proj(y.reshape(m, e), proj, 3, h, out_proj, i, tm=tm, tn=tn_out)
        else:
            proj = _norm_proj(h, norm_w[i], pool_in_proj, j, n_out=2 * e, tm=tm, tn=tn_in)
            y = _pool(proj.reshape(bsz, seq, 2 * e), pool_w_group, j, pool_scale[j], windows=POOL_WINDOWS)
            h = _out_proj(y.reshape(m, e), proj, 1, h, out_proj, i, tm=tm, tn=tn_out)
    return h.reshape(bsz, seq, d)
```

```python
import functools
import math

import jax
import jax.numpy as jnp
from jax import lax
from jax.experimental import pallas as pl
from jax.experimental.pallas import tpu as pltpu

F32 = jnp.float32
BF16 = jnp.bfloat16

LANES = 128
S5_GROUP = 16
S5_STATE = 64
TILE_GROUPS = LANES // S5_GROUP
TILE_STATES = TILE_GROUPS * S5_STATE
CHUNK = 8
FOX_HEAD_DIM = 128
POOL_WINDOWS = (2, 4, 8, 16)
POOL_HALO = 16
EPS = 1e-6
NEG_BIG = -1e30
VMEM_LIMIT = 56 * 1024 * 1024


def _params(semantics):
    return pltpu.CompilerParams(dimension_semantics=semantics, vmem_limit_bytes=VMEM_LIMIT)


def _rms_rows(x, w):
    ms = jnp.mean(x * x, axis=-1, keepdims=True)
    return x * lax.rsqrt(ms + EPS) * w


def _silu(z):
    return z * jax.nn.sigmoid(z)


def _lhs_cached(j, make_lhs, lhs_ref, emit):
    @pl.when(j == 0)
    def _():
        lhs = make_lhs()
        lhs_ref[...] = lhs
        emit(lhs)

    @pl.when(j > 0)
    def _():
        emit(lhs_ref[...])


def _norm_proj_body(x_ref, nw_ref, w_ref, *rest, has_extra):
    if has_extra:
        wf_ref, bf_ref, o_ref, of_ref, xn_ref = rest
    else:
        o_ref, xn_ref = rest
    j = pl.program_id(1)

    def emit(xn):
        o_ref[...] = jnp.dot(xn, w_ref[...].astype(BF16), preferred_element_type=F32).astype(BF16)

    @pl.when(j == 0)
    def _():
        xn = _rms_rows(x_ref[...], nw_ref[...]).astype(BF16)
        xn_ref[...] = xn
        emit(xn)
        if has_extra:
            of_ref[...] = jnp.dot(xn, wf_ref[...].astype(BF16), preferred_element_type=F32) + bf_ref[...]

    @pl.when(j > 0)
    def _():
        emit(xn_ref[...])


def _norm_proj(h2d, norm_w, w_all, layer, *, n_out, tm, tn, extra=None):
    m, d = h2d.shape
    in_specs = [pl.BlockSpec((tm, d), lambda i, j: (i, 0)),
                pl.BlockSpec((1, d), lambda i, j: (0, 0)),
                pl.BlockSpec((None, d, tn), lambda i, j: (layer, 0, j))]
    out_specs = [pl.BlockSpec((tm, tn), lambda i, j: (i, j))]
    out_shape = [jax.ShapeDtypeStruct((m, n_out), BF16)]
    args = [h2d, norm_w.reshape(1, d), w_all]
    if extra is not None:
        in_specs += [pl.BlockSpec((d, LANES), lambda i, j: (0, 0)),
                     pl.BlockSpec((1, LANES), lambda i, j: (0, 0))]
        out_specs += [pl.BlockSpec((tm, LANES), lambda i, j: (i, 0))]
        out_shape += [jax.ShapeDtypeStruct((m, LANES), F32)]
        args += list(extra)
    outs = pl.pallas_call(
        functools.partial(_norm_proj_body, has_extra=extra is not None),
        grid=(m // tm, n_out // tn),
        in_specs=in_specs,
        out_specs=out_specs,
        out_shape=out_shape,
        scratch_shapes=[pltpu.VMEM((tm, d), BF16)],
        compiler_params=_params(("parallel", "arbitrary")),
        name="norm_proj",
    )(*args)
    return outs if extra is not None else outs[0]


def _out_proj_body(y_ref, z_ref, h_ref, w_ref, o_ref, wbf_ref):
    @pl.when(pl.program_id(0) == 0)
    def _():
        wbf_ref[...] = w_ref[...].astype(BF16)

    a = (y_ref[...].astype(F32) * _silu(z_ref[...].astype(F32))).astype(BF16)
    o_ref[...] = h_ref[...] + jnp.dot(a, wbf_ref[...], preferred_element_type=F32)


def _out_proj(y2d, z2d, z_block, h2d, w_all, layer, *, tm):
    m, e = y2d.shape
    d = w_all.shape[2]
    return pl.pallas_call(
        _out_proj_body,
        grid=(m // tm,),
        in_specs=[pl.BlockSpec((tm, e), lambda i: (i, 0)),
                  pl.BlockSpec((tm, e), lambda i: (i, z_block)),
                  pl.BlockSpec((tm, d), lambda i: (i, 0)),
                  pl.BlockSpec((None, e, d), lambda i: (layer, 0, 0), pipeline_mode=pl.Buffered(1))],
        out_specs=pl.BlockSpec((tm, d), lambda i: (i, 0)),
        out_shape=jax.ShapeDtypeStruct((m, d), F32),
        scratch_shapes=[pltpu.VMEM((e, d), BF16)],
        compiler_params=_params(("arbitrary",)),
        name="out_proj",
    )(y2d, z2d, h2d, w_all)


def _s5_prep_body(a_ref, bt_ref, c_ref, wb_ref, wct_ref, toe_ref, lam_ref, *, chunk):
    ns = TILE_STATES
    n_lt = ns // LANES

    def block_diag(x):
        x2 = jnp.concatenate([x, x], axis=1)
        grp_row = lax.broadcasted_iota(jnp.int32, (LANES, LANES), 0) // S5_GROUP
        grp_lane = lax.broadcasted_iota(jnp.int32, (LANES, LANES), 1) // S5_STATE
        per_tile = LANES // S5_STATE
        return jnp.concatenate([jnp.where(grp_row == grp_lane + per_tile * t, x2, 0.0)
                                for t in range(n_lt)], axis=1)

    ar, ai, ldt = a_ref[0:1, :], a_ref[1:2, :], a_ref[2:3, :]
    dt = jnp.exp(ldt)
    k_rows = lax.broadcasted_iota(jnp.int32, (chunk + 1, 1), 0).astype(F32)
    mag = jnp.exp(ar * dt * k_rows)
    ang = ai * dt * k_rows
    pr, pi = mag * jnp.cos(ang), mag * jnp.sin(ang)
    abar_r, abar_i = pr[1:2, :], pi[1:2, :]
    den = ar * ar + ai * ai
    xr = abar_r - 1.0
    fr = (xr * ar + abar_i * ai) / den
    fi = (abar_i * ar - xr * ai) / den
    bt_re, bt_im = block_diag(bt_ref[0]), block_diag(bt_ref[1])
    bbar_re = fr * bt_re - fi * bt_im
    bbar_im = fr * bt_im + fi * bt_re
    c_re, c_im = block_diag(c_ref[0]), block_diag(c_ref[1])
    lam_ref[:, 0:ns] = pr[chunk:chunk + 1, :]
    lam_ref[:, ns:2 * ns] = pi[chunk:chunk + 1, :]

    for tau in range(chunk):
        qr, qi = pr[tau:tau + 1, :], pi[tau:tau + 1, :]
        g_re = (bbar_re * qr - bbar_im * qi).astype(BF16)
        g_im = (bbar_re * qi + bbar_im * qr).astype(BF16)
        t = chunk - 1 - tau
        wb_ref[t * LANES:(t + 1) * LANES, 0:ns] = g_re
        wb_ref[t * LANES:(t + 1) * LANES, ns:2 * ns] = g_im
        qr, qi = pr[tau + 1:tau + 2, :], pi[tau + 1:tau + 2, :]
        wct_ref[tau * LANES:(tau + 1) * LANES, 0:ns] = (c_re * qr - c_im * qi).astype(BF16)
        wct_ref[tau * LANES:(tau + 1) * LANES, ns:2 * ns] = (-(c_re * qi + c_im * qr)).astype(BF16)

    c_cat = jnp.concatenate([c_re, -c_im], axis=1).astype(BF16)
    k_all = lax.dot_general(wb_ref[...], c_cat, (((1,), (1,)), ((), ())),
                            preferred_element_type=F32)
    zero_blk = jnp.zeros((LANES, LANES), BF16)
    for tau in range(chunk):
        t = chunk - 1 - tau
        k_tau = k_all[t * LANES:(t + 1) * LANES, :].astype(BF16)
        toe_ref[0:LANES, tau * LANES:(tau + 1) * LANES] = k_tau
        if tau + 1 < chunk:
            toe_ref[LANES:2 * LANES, (tau + 1) * LANES:(tau + 2) * LANES] = k_tau
    toe_ref[LANES:2 * LANES, 0:LANES] = zero_blk


def _s5_prep(a_re, a_im, log_dt, b_re, b_im, c_re, c_im, *, chunk):
    g, p = a_re.shape
    nt = g // TILE_GROUPS
    ns = TILE_STATES
    ldt = jnp.broadcast_to(log_dt[:, None], (g, p))
    a_rows = jnp.stack([a_re.reshape(nt, ns), a_im.reshape(nt, ns), ldt.reshape(nt, ns)], axis=1)
    bt = jnp.stack([b_re, b_im], axis=0).transpose(1, 0, 3, 2).reshape(nt, TILE_GROUPS, 2, S5_GROUP, p)
    bt = bt.transpose(0, 2, 1, 3, 4).reshape(nt, 2, LANES, p)
    cc = jnp.stack([c_re.reshape(nt, LANES, p), c_im.reshape(nt, LANES, p)], axis=1)
    kw = chunk * LANES
    return pl.pallas_call(
        functools.partial(_s5_prep_body, chunk=chunk),
        grid=(nt,),
        in_specs=[pl.BlockSpec((None, 3, ns), lambda j: (j, 0, 0)),
                  pl.BlockSpec((None, 2, LANES, p), lambda j: (j, 0, 0, 0)),
                  pl.BlockSpec((None, 2, LANES, p), lambda j: (j, 0, 0, 0))],
        out_specs=[pl.BlockSpec((None, kw, 2 * ns), lambda j: (j, 0, 0)),
                   pl.BlockSpec((None, kw, 2 * ns), lambda j: (j, 0, 0)),
                   pl.BlockSpec((None, 2 * LANES, kw), lambda j: (j, 0, 0)),
                   pl.BlockSpec((None, 1, 2 * ns), lambda j: (j, 0, 0))],
        out_shape=[jax.ShapeDtypeStruct((nt, kw, 2 * ns), BF16),
                   jax.ShapeDtypeStruct((nt, kw, 2 * ns), BF16),
                   jax.ShapeDtypeStruct((nt, 2 * LANES, kw), BF16),
                   jax.ShapeDtypeStruct((nt, 1, 2 * ns), F32)],
        compiler_params=_params(("parallel",)),
        name="s5_prep",
    )(a_rows, bt, cc)


def _s5_core_body(ubf_ref, d_ref, wb_ref, wct_ref, toe_ref, lam_ref, o_ref,
                  u_ref, ub_ref, slab_ref, hin_ref, hprev_ref, *, chunk, n_seq, n_chunks):
    ns = TILE_STATES
    nk = ns // LANES
    seq = n_chunks * chunk

    def token_rows(b, t):
        return pl.ds(b * seq + t, n_chunks, stride=chunk)

    def chunk_rows(b):
        return pl.ds(b, n_chunks, stride=n_seq)

    u_ref[...] = ubf_ref[...].astype(F32)
    for t in range(chunk):
        for b in range(n_seq):
            slab_ref[t, chunk_rows(b), :] = u_ref[token_rows(b, t), :]
        ub_ref[:, t * LANES:(t + 1) * LANES] = slab_ref[t].astype(BF16)

    h_in = jnp.dot(ub_ref[...], wb_ref[...], preferred_element_type=F32)
    for k in range(2 * nk):
        hin_ref[k] = h_in[:, k * LANES:(k + 1) * LANES]

    for a in range(chunk // 2):
        lo = 2 * a * LANES
        width = (chunk - 2 * a) * LANES
        part = jnp.dot(ub_ref[:, lo:lo + 2 * LANES], toe_ref[:, 0:width], preferred_element_type=F32)
        for i in range(chunk - 2 * a):
            if a == 0:
                slab_ref[i] = part[:, i * LANES:(i + 1) * LANES]
            else:
                slab_ref[2 * a + i] += part[:, i * LANES:(i + 1) * LANES]

    lam = [jnp.broadcast_to(lam_ref[:, k * LANES:(k + 1) * LANES], (n_seq, LANES)) for k in range(2 * nk)]
    state = [jnp.zeros((n_seq, LANES), F32) for _ in range(2 * nk)]
    for c in range(n_chunks):
        rows = pl.ds(c * n_seq, n_seq)
        new_state = []
        for k in range(nk):
            hr, hi = state[k], state[nk + k]
            hprev_ref[k, rows, :] = hr
            hprev_ref[nk + k, rows, :] = hi
            lr, li = lam[k], lam[nk + k]
            new_state.append((lr * hr - li * hi + hin_ref[k, rows, :],
                              lr * hi + li * hr + hin_ref[nk + k, rows, :]))
        state = [s[0] for s in new_state] + [s[1] for s in new_state]

    h_prev = jnp.concatenate([hprev_ref[k].astype(BF16) for k in range(2 * nk)], axis=1)
    for a in range(chunk // 2):
        lo = 2 * a * LANES
        y = lax.dot_general(h_prev, wct_ref[lo:lo + 2 * LANES, :], (((1,), (1,)), ((), ())),
                            preferred_element_type=F32)
        for i in range(2):
            t = 2 * a + i
            slab_ref[t] += y[:, i * LANES:(i + 1) * LANES]
            for b in range(n_seq):
                y_bt = slab_ref[t, chunk_rows(b), :] + d_ref[...] * u_ref[token_rows(b, t), :]
                o_ref[token_rows(b, t), :] = y_bt


def _s5_core(proj2d, d_skip, wb, wct, toe, lam, *, chunk, n_seq):
    m, n2 = proj2d.shape
    e = n2 // 2
    r = m // chunk
    nt = e // LANES
    ns2 = 2 * TILE_STATES
    kw = chunk * LANES
    return pl.pallas_call(
        functools.partial(_s5_core_body, chunk=chunk, n_seq=n_seq, n_chunks=r // n_seq),
        grid=(nt,),
        in_specs=[pl.BlockSpec((m, LANES), lambda j: (0, j)),
                  pl.BlockSpec((1, LANES), lambda j: (0, j)),
                  pl.BlockSpec((None, kw, ns2), lambda j: (j, 0, 0)),
                  pl.BlockSpec((None, kw, ns2), lambda j: (j, 0, 0)),
                  pl.BlockSpec((None, 2 * LANES, kw), lambda j: (j, 0, 0)),
                  pl.BlockSpec((None, 1, ns2), lambda j: (j, 0, 0))],
        out_specs=pl.BlockSpec((m, LANES), lambda j: (0, j)),
        out_shape=jax.ShapeDtypeStruct((m, e), F32),
        scratch_shapes=[pltpu.VMEM((m, LANES), F32),
                        pltpu.VMEM((r, kw), BF16),
                        pltpu.VMEM((chunk, r, LANES), F32),
                        pltpu.VMEM((ns2 // LANES, r, LANES), F32),
                        pltpu.VMEM((ns2 // LANES, r, LANES), F32)],
        compiler_params=_params(("parallel",)),
        name="s5_core",
    )(proj2d, d_skip.reshape(1, e), wb, wct, toe, lam)


def _glu_body(y_ref, w_ref, b_ref, o_ref, wbf_ref):
    @pl.when(pl.program_id(0) == 0)
    def _():
        wbf_ref[...] = w_ref[...].astype(BF16)

    g = jax.nn.gelu(y_ref[...])
    acc = jnp.dot(g.astype(BF16), wbf_ref[...], preferred_element_type=F32) + b_ref[...]
    o_ref[...] = (g * jax.nn.sigmoid(acc)).astype(BF16)


def _glu(y2d, w_all, layer, b, *, tm):
    m, e = y2d.shape
    return pl.pallas_call(
        _glu_body,
        grid=(m // tm,),
        in_specs=[pl.BlockSpec((tm, e), lambda i: (i, 0)),
                  pl.BlockSpec((None, e, e), lambda i: (layer, 0, 0), pipeline_mode=pl.Buffered(1)),
                  pl.BlockSpec((1, e), lambda i: (0, 0))],
        out_specs=pl.BlockSpec((tm, e), lambda i: (i, 0)),
        out_shape=jax.ShapeDtypeStruct((m, e), BF16),
        scratch_shapes=[pltpu.VMEM((e, e), BF16)],
        compiler_params=_params(("arbitrary",)),
        name="s5_glu",
    )(y2d, w_all, b.reshape(1, e))


def _fox_cum_body(f_ref, o_ref, *, blk):
    seq = f_ref.shape[0]
    row = lax.broadcasted_iota(jnp.int32, (blk, blk), 0)
    col = lax.broadcasted_iota(jnp.int32, (blk, blk), 1)
    tri = (col <= row).astype(F32)
    carry = jnp.zeros((1, LANES), F32)
    for i in range(seq // blk):
        x = f_ref[i * blk:(i + 1) * blk, :]
        ls = jnp.minimum(x, 0.0) - jnp.log1p(jnp.exp(-jnp.abs(x)))
        cum = jnp.dot(tri, ls, preferred_element_type=F32,
                      precision=lax.Precision.HIGHEST) + carry
        o_ref[i * blk:(i + 1) * blk, :] = cum
        carry = cum[blk - 1:blk, :]


def _fox_cum(f3d, *, blk):
    b, seq, _ = f3d.shape
    return pl.pallas_call(
        functools.partial(_fox_cum_body, blk=blk),
        grid=(b,),
        in_specs=[pl.BlockSpec((None, seq, LANES), lambda i: (i, 0, 0))],
        out_specs=pl.BlockSpec((None, seq, LANES), lambda i: (i, 0, 0)),
        out_shape=jax.ShapeDtypeStruct((b, seq, LANES), F32),
        compiler_params=_params(("parallel",)),
        name="fox_cum",
    )(f3d)


def _fox_attn_body(q_ref, k_ref, v_ref, ck_ref, qw_ref, kw_ref, o_ref,
                   kn_ref, m_ref, l_ref, acc_ref, *, tq, scale):
    seq = k_ref.shape[0]
    nq = seq // tq
    log2e = 1.4426950408889634

    def q_block(i):
        rows = slice(i * tq, (i + 1) * tq)
        return (_rms_rows(q_ref[rows, :].astype(F32), qw_ref[...]) * (scale * log2e)).astype(BF16)

    def k_block(j):
        rows = slice(j * tq, (j + 1) * tq)
        kn_ref[rows, :] = _rms_rows(k_ref[rows, :].astype(F32), kw_ref[...]).astype(BF16)

    pairs = [(i, j) for i in range(nq) for j in range(i + 1)]
    qn = {}

    def logits(i, j):
        if j == 0:
            qn[i] = q_block(i)
        if j == i:
            k_block(j)
        return lax.dot_general(qn[i], kn_ref[j * tq:(j + 1) * tq, :],
                               (((1,), (1,)), ((), ())), preferred_element_type=F32)

    n_lt = tq // LANES
    s_next = logits(*pairs[0])
    for idx, (i, j) in enumerate(pairs):
        s = s_next
        if idx + 1 < len(pairs):
            s_next = logits(*pairs[idx + 1])
        c0 = ck_ref[i:i + 1, 0:1]
        dk = (ck_ref[j:j + 1, :] - c0) * log2e
        s_t = [s[:, t * LANES:(t + 1) * LANES] - dk[:, t * LANES:(t + 1) * LANES] for t in range(n_lt)]
        if i == j:
            r_i = lax.broadcasted_iota(jnp.int32, (tq, LANES), 0)
            c_i = lax.broadcasted_iota(jnp.int32, (tq, LANES), 1)
            s_t = [jnp.where(c_i + t * LANES <= r_i, s_t[t], NEG_BIG) for t in range(n_lt)]
        mx = functools.reduce(jnp.maximum, s_t)
        row_max = jnp.broadcast_to(jnp.max(mx, axis=-1, keepdims=True), (tq, LANES))
        v_blk = v_ref[j * tq:(j + 1) * tq, :]
        if j == 0:
            m_new = row_max
        else:
            m_old = m_ref[...]
            m_new = jnp.maximum(m_old, row_max)
        p_t = [jnp.exp2(s_t[t] - m_new) for t in range(n_lt)]
        p_sum = functools.reduce(jnp.add, p_t)
        pv = jnp.dot(jnp.concatenate([p.astype(BF16) for p in p_t], axis=1), v_blk,
                     preferred_element_type=F32)
        if j == 0:
            l_new, acc_new = p_sum, pv
        else:
            alpha = jnp.exp2(m_old - m_new)
            l_new = alpha * l_ref[...] + p_sum
            acc_new = alpha * acc_ref[...] + pv
        if j == i:
            l_row = jnp.sum(l_new, axis=-1, keepdims=True)
            o_ref[i * tq:(i + 1) * tq, :] = (acc_new / l_row).astype(BF16)
        else:
            m_ref[...] = m_new
            l_ref[...] = l_new
            acc_ref[...] = acc_new


def _fox_attn(proj3d, cum_k, q_w, k_w, *, heads, tq):
    b, seq, _ = proj3d.shape
    dh = FOX_HEAD_DIM
    nq = seq // tq
    return pl.pallas_call(
        functools.partial(_fox_attn_body, tq=tq, scale=dh ** -0.5),
        grid=(b, heads),
        in_specs=[pl.BlockSpec((None, seq, dh), lambda bi, h: (bi, 0, h)),
                  pl.BlockSpec((None, seq, dh), lambda bi, h: (bi, 0, heads + h)),
                  pl.BlockSpec((None, seq, dh), lambda bi, h: (bi, 0, 2 * heads + h)),
                  pl.BlockSpec((None, None, nq, tq), lambda bi, h: (bi, h, 0, 0)),
                  pl.BlockSpec((1, dh), lambda bi, h: (0, 0)),
                  pl.BlockSpec((1, dh), lambda bi, h: (0, 0))],
        out_specs=pl.BlockSpec((None, seq, dh), lambda bi, h: (bi, 0, h)),
        out_shape=jax.ShapeDtypeStruct((b, seq, heads * dh), BF16),
        scratch_shapes=[pltpu.VMEM((seq, dh), BF16),
                        pltpu.VMEM((tq, LANES), F32),
                        pltpu.VMEM((tq, LANES), F32),
                        pltpu.VMEM((tq, dh), F32)],
        compiler_params=_params(("parallel", "parallel")),
        name="fox_attn",
    )(proj3d, proj3d, proj3d, cum_k, q_w.reshape(1, dh), k_w.reshape(1, dh))


def _pool_body(u_ref, w_ref, s_ref, o_ref, sa_ref, sb_ref, *, windows):
    seq, cg = u_ref.shape
    halo = POOL_HALO
    grp = pl.program_id(1)
    zeros = jnp.zeros((halo, cg), F32)
    sa_ref[0:halo, :] = zeros
    sb_ref[0:halo, :] = zeros
    t_idx = lax.broadcasted_iota(jnp.int32, (seq, 1), 0)

    for gi, win in enumerate(windows):
        @pl.when(grp == gi)
        def _(win=win):
            u = u_ref[...].astype(F32)
            sa_ref[halo:halo + seq, :] = u
            bufs = (sa_ref, sb_ref)
            s = u
            shift = 1
            level = 0
            while shift < win:
                src = bufs[level % 2]
                s = src[halo:halo + seq, :] + src[halo - shift:halo - shift + seq, :]
                shift *= 2
                level += 1
                if shift < win:
                    bufs[level % 2][halo:halo + seq, :] = s
            cnt = jnp.minimum(t_idx + 1, win).astype(F32)
            diff = (s / cnt - u).astype(BF16)
            mixed = jnp.dot(diff, w_ref[...].astype(BF16), preferred_element_type=F32)
            o_ref[...] = (mixed * s_ref[...]).astype(BF16)


def _pool(proj3d, w_group_all, layer, scale, *, windows):
    b, seq, n2 = proj3d.shape
    e = n2 // 2
    ng = len(windows)
    cg = e // ng
    return pl.pallas_call(
        functools.partial(_pool_body, windows=windows),
        grid=(b, ng),
        in_specs=[pl.BlockSpec((None, seq, cg), lambda bi, g: (bi, 0, g)),
                  pl.BlockSpec((None, None, cg, cg), lambda bi, g: (layer, g, 0, 0)),
                  pl.BlockSpec((1, cg), lambda bi, g: (0, g))],
        out_specs=pl.BlockSpec((None, seq, cg), lambda bi, g: (bi, 0, g)),
        out_shape=jax.ShapeDtypeStruct((b, seq, e), BF16),
        scratch_shapes=[pltpu.VMEM((POOL_HALO + seq, cg), F32),
                        pltpu.VMEM((POOL_HALO + seq, cg), F32)],
        compiler_params=_params(("parallel", "parallel")),
        name="pool_mix",
    )(proj3d, w_group_all, scale.reshape(1, e))


def _pick(n, pref):
    t = min(n, pref)
    while n % t:
        t //= 2
    return t


def kernel(x, norm_w, out_proj, s5_in_proj, s5_a_re, s5_a_im, s5_log_dt, s5_b_re, s5_b_im, s5_c_re, s5_c_im, s5_d, s5_w_glu, s5_b_glu, fox_in_proj, fox_q_norm, fox_k_norm, fox_f_bias, pool_in_proj, pool_w_group, pool_scale):
    bsz, seq, d = x.shape
    depth = norm_w.shape[0]
    e = s5_d.shape[1]
    heads = e // FOX_HEAD_DIM
    m = bsz * seq
    n_mixers = 3

    tm = _pick(m, 1024)
    tn_in = _pick(2 * e, 1024)
    tm_out = _pick(m, 256)
    tm_glu = _pick(m, 512)
    tq = _pick(seq, 512)

    h = x.reshape(m, d)
    for i in range(depth):
        kind, j = i % n_mixers, i // n_mixers
        if kind == 0:
            wb, wct, toe, lam = _s5_prep(s5_a_re[j], s5_a_im[j], s5_log_dt[j], s5_b_re[j], s5_b_im[j],
                                         s5_c_re[j], s5_c_im[j], chunk=CHUNK)
            proj = _norm_proj(h, norm_w[i], s5_in_proj, j, n_out=2 * e, tm=tm, tn=tn_in)
            g = _s5_core(proj, s5_d[j], wb, wct, toe, lam, chunk=CHUNK, n_seq=bsz)
            y = _glu(g, s5_w_glu, j, s5_b_glu[j], tm=tm_glu)
            h = _out_proj(y, proj, 1, h, out_proj, i, tm=tm_out)
        elif kind == 1:
            n_main = 4 * e
            w_f = jnp.pad(fox_in_proj[j][:, n_main:], ((0, 0), (0, LANES - heads)))
            b_f = jnp.pad(fox_f_bias[j], (0, LANES - heads)).reshape(1, LANES)
            proj, f_logit = _norm_proj(h, norm_w[i], fox_in_proj, j, n_out=n_main, tm=tm, tn=tn_in,
                                       extra=(w_f, b_f))
            cum = _fox_cum(f_logit.reshape(bsz, seq, LANES), blk=_pick(seq, 256))
            cum_k = cum[:, :, :heads].transpose(0, 2, 1).reshape(bsz, heads, seq // tq, tq)
            y = _fox_attn(proj.reshape(bsz, seq, n_main), cum_k, fox_q_norm[j], fox_k_norm[j],
                          heads=heads, tq=tq)
            h = _out_proj(y.reshape(m, e), proj, 3, h, out_proj, i, tm=tm_out)
        else:
            proj = _norm_proj(h, norm_w[i], pool_in_proj, j, n_out=2 * e, tm=tm, tn=tn_in)
            y = _pool(proj.reshape(bsz, seq, 2 * e), pool_w_group, j, pool_scale[j], windows=POOL_WINDOWS)
            h = _out_proj(y.reshape(m, e), proj, 1, h, out_proj, i, tm=tm_out)
    return h.reshape(bsz, seq, d)
```

```python
import functools
import math

import jax
import jax.numpy as jnp
from jax import lax
from jax.experimental import pallas as pl
from jax.experimental.pallas import tpu as pltpu

F32 = jnp.float32
BF16 = jnp.bfloat16

LANES = 128
S5_GROUP = 16
S5_STATE = 64
TILE_GROUPS = LANES // S5_GROUP
TILE_STATES = TILE_GROUPS * S5_STATE
CHUNK = 8
FOX_HEAD_DIM = 128
POOL_WINDOWS = (2, 4, 8, 16)
POOL_HALO = 16
EPS = 1e-6
NEG_BIG = -1e30
VMEM_LIMIT = 56 * 1024 * 1024


def _params(semantics):
    return pltpu.CompilerParams(dimension_semantics=semantics, vmem_limit_bytes=VMEM_LIMIT)


def _rms_rows(x, w):
    ms = jnp.mean(x * x, axis=-1, keepdims=True)
    return x * lax.rsqrt(ms + EPS) * w


def _silu(z):
    return z * jax.nn.sigmoid(z)


def _lhs_cached(j, make_lhs, lhs_ref, emit):
    @pl.when(j == 0)
    def _():
        lhs = make_lhs()
        lhs_ref[...] = lhs
        emit(lhs)

    @pl.when(j > 0)
    def _():
        emit(lhs_ref[...])


def _norm_proj_body(x_ref, nw_ref, w_ref, *rest, w_transposed, n_extra):
    if n_extra:
        wf_ref, bf_ref, o_ref, of_ref, xn_ref = rest
    else:
        o_ref, xn_ref = rest
    j = pl.program_id(1)
    w_contract = 1 if w_transposed else 0

    def matmul(xn, w):
        return lax.dot_general(xn, w, (((1,), (w_contract,)), ((), ())), preferred_element_type=F32)

    def emit(xn):
        o_ref[...] = matmul(xn, w_ref[...].astype(BF16)).astype(BF16)

    @pl.when(j == 0)
    def _():
        xn = _rms_rows(x_ref[...], nw_ref[...]).astype(BF16)
        xn_ref[...] = xn
        emit(xn)
        if n_extra:
            idx = lax.broadcasted_iota(jnp.int32, wf_ref.shape, 0 if w_transposed else 1)
            wf = jnp.where(idx < n_extra, wf_ref[...], 0.0).astype(BF16)
            of_ref[...] = matmul(xn, wf) + bf_ref[...]

    @pl.when(j > 0)
    def _():
        emit(xn_ref[...])


def _norm_proj(h2d, norm_w, w_all, layer, *, n_out, tm, tn, w_transposed=False, extra_bias=None):
    m, d = h2d.shape
    if w_transposed:
        w_spec = pl.BlockSpec((None, tn, d), lambda i, j: (layer, j, 0))
        wf_spec = pl.BlockSpec((None, LANES, d), lambda i, j: (layer, n_out // LANES, 0))
        n_extra = w_all.shape[1] - n_out
    else:
        w_spec = pl.BlockSpec((None, d, tn), lambda i, j: (layer, 0, j))
        wf_spec = pl.BlockSpec((None, d, LANES), lambda i, j: (layer, 0, n_out // LANES))
        n_extra = w_all.shape[2] - n_out
    in_specs = [pl.BlockSpec((tm, d), lambda i, j: (i, 0)),
                pl.BlockSpec((1, d), lambda i, j: (0, 0)),
                w_spec]
    out_specs = [pl.BlockSpec((tm, tn), lambda i, j: (i, j))]
    out_shape = [jax.ShapeDtypeStruct((m, n_out), BF16)]
    args = [h2d, norm_w.reshape(1, d), w_all]
    if extra_bias is None:
        n_extra = 0
    else:
        in_specs += [wf_spec, pl.BlockSpec((1, LANES), lambda i, j: (0, 0))]
        out_specs += [pl.BlockSpec((tm, LANES), lambda i, j: (i, 0))]
        out_shape += [jax.ShapeDtypeStruct((m, LANES), F32)]
        args += [w_all, extra_bias]
    outs = pl.pallas_call(
        functools.partial(_norm_proj_body, w_transposed=w_transposed, n_extra=n_extra),
        grid=(m // tm, n_out // tn),
        in_specs=in_specs,
        out_specs=out_specs,
        out_shape=out_shape,
        scratch_shapes=[pltpu.VMEM((tm, d), BF16)],
        compiler_params=_params(("parallel", "arbitrary")),
        name="norm_proj",
    )(*args)
    return outs if n_extra else outs[0]


def _out_proj_body(y_ref, z_ref, h_ref, w_ref, o_ref, wbf_ref):
    @pl.when(pl.program_id(0) == 0)
    def _():
        wbf_ref[...] = w_ref[...].astype(BF16)

    a = (y_ref[...].astype(F32) * _silu(z_ref[...].astype(F32))).astype(BF16)
    o_ref[...] = h_ref[...] + jnp.dot(a, wbf_ref[...], preferred_element_type=F32)


def _out_proj(y2d, z2d, z_block, h2d, w_all, layer, *, tm):
    m, e = y2d.shape
    d = w_all.shape[2]
    return pl.pallas_call(
        _out_proj_body,
        grid=(m // tm,),
        in_specs=[pl.BlockSpec((tm, e), lambda i: (i, 0)),
                  pl.BlockSpec((tm, e), lambda i: (i, z_block)),
                  pl.BlockSpec((tm, d), lambda i: (i, 0)),
                  pl.BlockSpec((None, e, d), lambda i: (layer, 0, 0), pipeline_mode=pl.Buffered(1))],
        out_specs=pl.BlockSpec((tm, d), lambda i: (i, 0)),
        out_shape=jax.ShapeDtypeStruct((m, d), F32),
        scratch_shapes=[pltpu.VMEM((e, d), BF16)],
        compiler_params=_params(("arbitrary",)),
        name="out_proj",
    )(y2d, z2d, h2d, w_all)


def _s5_prep_body(a_ref, bt_ref, c_ref, wb_ref, wct_ref, toe_ref, lam_ref, *, chunk):
    ns = TILE_STATES
    n_lt = ns // LANES

    def block_diag(x):
        x2 = jnp.concatenate([x, x], axis=1)
        grp_row = lax.broadcasted_iota(jnp.int32, (LANES, LANES), 0) // S5_GROUP
        grp_lane = lax.broadcasted_iota(jnp.int32, (LANES, LANES), 1) // S5_STATE
        per_tile = LANES // S5_STATE
        return jnp.concatenate([jnp.where(grp_row == grp_lane + per_tile * t, x2, 0.0)
                                for t in range(n_lt)], axis=1)

    ar, ai, ldt = a_ref[0:1, :], a_ref[1:2, :], a_ref[2:3, :]
    dt = jnp.exp(ldt)
    k_rows = lax.broadcasted_iota(jnp.int32, (chunk + 1, 1), 0).astype(F32)
    mag = jnp.exp(ar * dt * k_rows)
    ang = ai * dt * k_rows
    pr, pi = mag * jnp.cos(ang), mag * jnp.sin(ang)
    abar_r, abar_i = pr[1:2, :], pi[1:2, :]
    den = ar * ar + ai * ai
    xr = abar_r - 1.0
    fr = (xr * ar + abar_i * ai) / den
    fi = (abar_i * ar - xr * ai) / den
    bt_re, bt_im = block_diag(bt_ref[0]), block_diag(bt_ref[1])
    bbar_re = fr * bt_re - fi * bt_im
    bbar_im = fr * bt_im + fi * bt_re
    c_re, c_im = block_diag(c_ref[0]), block_diag(c_ref[1])
    lam_ref[:, 0:ns] = pr[chunk:chunk + 1, :]
    lam_ref[:, ns:2 * ns] = pi[chunk:chunk + 1, :]

    for tau in range(chunk):
        qr, qi = pr[tau:tau + 1, :], pi[tau:tau + 1, :]
        g_re = (bbar_re * qr - bbar_im * qi).astype(BF16)
        g_im = (bbar_re * qi + bbar_im * qr).astype(BF16)
        t = chunk - 1 - tau
        wb_ref[t * LANES:(t + 1) * LANES, 0:ns] = g_re
        wb_ref[t * LANES:(t + 1) * LANES, ns:2 * ns] = g_im
        qr, qi = pr[tau + 1:tau + 2, :], pi[tau + 1:tau + 2, :]
        wct_ref[tau * LANES:(tau + 1) * LANES, 0:ns] = (c_re * qr - c_im * qi).astype(BF16)
        wct_ref[tau * LANES:(tau + 1) * LANES, ns:2 * ns] = (-(c_re * qi + c_im * qr)).astype(BF16)

    c_cat = jnp.concatenate([c_re, -c_im], axis=1).astype(BF16)
    k_all = lax.dot_general(wb_ref[...], c_cat, (((1,), (1,)), ((), ())),
                            preferred_element_type=F32)
    zero_blk = jnp.zeros((LANES, LANES), BF16)
    for tau in range(chunk):
        t = chunk - 1 - tau
        k_tau = k_all[t * LANES:(t + 1) * LANES, :].astype(BF16)
        toe_ref[0:LANES, tau * LANES:(tau + 1) * LANES] = k_tau
        if tau + 1 < chunk:
            toe_ref[LANES:2 * LANES, (tau + 1) * LANES:(tau + 2) * LANES] = k_tau
    toe_ref[LANES:2 * LANES, 0:LANES] = zero_blk


def _s5_prep(a_re, a_im, log_dt, b_re, b_im, c_re, c_im, *, chunk):
    g, p = a_re.shape
    nt = g // TILE_GROUPS
    ns = TILE_STATES
    ldt = jnp.broadcast_to(log_dt[:, None], (g, p))
    a_rows = jnp.stack([a_re.reshape(nt, ns), a_im.reshape(nt, ns), ldt.reshape(nt, ns)], axis=1)
    bt = jnp.stack([b_re, b_im], axis=0).transpose(1, 0, 3, 2).reshape(nt, TILE_GROUPS, 2, S5_GROUP, p)
    bt = bt.transpose(0, 2, 1, 3, 4).reshape(nt, 2, LANES, p)
    cc = jnp.stack([c_re.reshape(nt, LANES, p), c_im.reshape(nt, LANES, p)], axis=1)
    kw = chunk * LANES
    return pl.pallas_call(
        functools.partial(_s5_prep_body, chunk=chunk),
        grid=(nt,),
        in_specs=[pl.BlockSpec((None, 3, ns), lambda j: (j, 0, 0)),
                  pl.BlockSpec((None, 2, LANES, p), lambda j: (j, 0, 0, 0)),
                  pl.BlockSpec((None, 2, LANES, p), lambda j: (j, 0, 0, 0))],
        out_specs=[pl.BlockSpec((None, kw, 2 * ns), lambda j: (j, 0, 0)),
                   pl.BlockSpec((None, kw, 2 * ns), lambda j: (j, 0, 0)),
                   pl.BlockSpec((None, 2 * LANES, kw), lambda j: (j, 0, 0)),
                   pl.BlockSpec((None, 1, 2 * ns), lambda j: (j, 0, 0))],
        out_shape=[jax.ShapeDtypeStruct((nt, kw, 2 * ns), BF16),
                   jax.ShapeDtypeStruct((nt, kw, 2 * ns), BF16),
                   jax.ShapeDtypeStruct((nt, 2 * LANES, kw), BF16),
                   jax.ShapeDtypeStruct((nt, 1, 2 * ns), F32)],
        compiler_params=_params(("parallel",)),
        name="s5_prep",
    )(a_rows, bt, cc)


def _s5_core_body(ubf_ref, d_ref, wb_ref, wct_ref, toe_ref, lam_ref, o_ref,
                  u_ref, ub_ref, slab_ref, hin_ref, hprev_ref, *, chunk, n_seq, n_chunks):
    ns = TILE_STATES
    nk = ns // LANES
    seq = n_chunks * chunk

    def token_rows(b, t):
        return pl.ds(b * seq + t, n_chunks, stride=chunk)

    def chunk_rows(b):
        return pl.ds(b, n_chunks, stride=n_seq)

    u_ref[...] = ubf_ref[...].astype(F32)
    for t in range(chunk):
        for b in range(n_seq):
            slab_ref[t, chunk_rows(b), :] = u_ref[token_rows(b, t), :]
        ub_ref[:, t * LANES:(t + 1) * LANES] = slab_ref[t].astype(BF16)

    h_in = jnp.dot(ub_ref[...], wb_ref[...], preferred_element_type=F32)
    for k in range(2 * nk):
        hin_ref[k] = h_in[:, k * LANES:(k + 1) * LANES]

    for a in range(chunk // 2):
        lo = 2 * a * LANES
        width = (chunk - 2 * a) * LANES
        part = jnp.dot(ub_ref[:, lo:lo + 2 * LANES], toe_ref[:, 0:width], preferred_element_type=F32)
        for i in range(chunk - 2 * a):
            if a == 0:
                slab_ref[i] = part[:, i * LANES:(i + 1) * LANES]
            else:
                slab_ref[2 * a + i] += part[:, i * LANES:(i + 1) * LANES]

    lam = [jnp.broadcast_to(lam_ref[:, k * LANES:(k + 1) * LANES], (n_seq, LANES)) for k in range(2 * nk)]
    state = [jnp.zeros((n_seq, LANES), F32) for _ in range(2 * nk)]
    for c in range(n_chunks):
        rows = pl.ds(c * n_seq, n_seq)
        new_state = []
        for k in range(nk):
            hr, hi = state[k], state[nk + k]
            hprev_ref[k, rows, :] = hr
            hprev_ref[nk + k, rows, :] = hi
            lr, li = lam[k], lam[nk + k]
            new_state.append((lr * hr - li * hi + hin_ref[k, rows, :],
                              lr * hi + li * hr + hin_ref[nk + k, rows, :]))
        state = [s[0] for s in new_state] + [s[1] for s in new_state]

    h_prev = jnp.concatenate([hprev_ref[k].astype(BF16) for k in range(2 * nk)], axis=1)
    for a in range(chunk // 2):
        lo = 2 * a * LANES
        y = lax.dot_general(h_prev, wct_ref[lo:lo + 2 * LANES, :], (((1,), (1,)), ((), ())),
                            preferred_element_type=F32)
        for i in range(2):
            t = 2 * a + i
            slab_ref[t] += y[:, i * LANES:(i + 1) * LANES]
            for b in range(n_seq):
                y_bt = slab_ref[t, chunk_rows(b), :] + d_ref[...] * u_ref[token_rows(b, t), :]
                o_ref[token_rows(b, t), :] = y_bt


def _s5_core(proj2d, d_skip, wb, wct, toe, lam, *, chunk, n_seq):
    m, n2 = proj2d.shape
    e = n2 // 2
    r = m // chunk
    nt = e // LANES
    ns2 = 2 * TILE_STATES
    kw = chunk * LANES
    return pl.pallas_call(
        functools.partial(_s5_core_body, chunk=chunk, n_seq=n_seq, n_chunks=r // n_seq),
        grid=(nt,),
        in_specs=[pl.BlockSpec((m, LANES), lambda j: (0, j)),
                  pl.BlockSpec((1, LANES), lambda j: (0, j)),
                  pl.BlockSpec((None, kw, ns2), lambda j: (j, 0, 0)),
                  pl.BlockSpec((None, kw, ns2), lambda j: (j, 0, 0)),
                  pl.BlockSpec((None, 2 * LANES, kw), lambda j: (j, 0, 0)),
                  pl.BlockSpec((None, 1, ns2), lambda j: (j, 0, 0))],
        out_specs=pl.BlockSpec((m, LANES), lambda j: (0, j)),
        out_shape=jax.ShapeDtypeStruct((m, e), F32),
        scratch_shapes=[pltpu.VMEM((m, LANES), F32),
                        pltpu.VMEM((r, kw), BF16),
                        pltpu.VMEM((chunk, r, LANES), F32),
                        pltpu.VMEM((ns2 // LANES, r, LANES), F32),
                        pltpu.VMEM((ns2 // LANES, r, LANES), F32)],
        compiler_params=_params(("parallel",)),
        name="s5_core",
    )(proj2d, d_skip.reshape(1, e), wb, wct, toe, lam)


def _glu_body(y_ref, w_ref, b_ref, o_ref, wbf_ref):
    @pl.when(pl.program_id(0) == 0)
    def _():
        wbf_ref[...] = w_ref[...].astype(BF16)

    g = jax.nn.gelu(y_ref[...])
    acc = jnp.dot(g.astype(BF16), wbf_ref[...], preferred_element_type=F32) + b_ref[...]
    o_ref[...] = (g * jax.nn.sigmoid(acc)).astype(BF16)


def _glu(y2d, w_all, layer, b, *, tm):
    m, e = y2d.shape
    return pl.pallas_call(
        _glu_body,
        grid=(m // tm,),
        in_specs=[pl.BlockSpec((tm, e), lambda i: (i, 0)),
                  pl.BlockSpec((None, e, e), lambda i: (layer, 0, 0), pipeline_mode=pl.Buffered(1)),
                  pl.BlockSpec((1, e), lambda i: (0, 0))],
        out_specs=pl.BlockSpec((tm, e), lambda i: (i, 0)),
        out_shape=jax.ShapeDtypeStruct((m, e), BF16),
        scratch_shapes=[pltpu.VMEM((e, e), BF16)],
        compiler_params=_params(("arbitrary",)),
        name="s5_glu",
    )(y2d, w_all, b.reshape(1, e))


def _fox_cum_body(f_ref, o_ref, *, blk):
    seq = f_ref.shape[0]
    row = lax.broadcasted_iota(jnp.int32, (blk, blk), 0)
    col = lax.broadcasted_iota(jnp.int32, (blk, blk), 1)
    tri = (col <= row).astype(F32)
    carry = jnp.zeros((1, LANES), F32)
    for i in range(seq // blk):
        x = f_ref[i * blk:(i + 1) * blk, :]
        ls = jnp.minimum(x, 0.0) - jnp.log1p(jnp.exp(-jnp.abs(x)))
        cum = jnp.dot(tri, ls, preferred_element_type=F32,
                      precision=lax.Precision.HIGHEST) + carry
        o_ref[i * blk:(i + 1) * blk, :] = cum
        carry = cum[blk - 1:blk, :]


def _fox_cum(f3d, *, blk):
    b, seq, _ = f3d.shape
    return pl.pallas_call(
        functools.partial(_fox_cum_body, blk=blk),
        grid=(b,),
        in_specs=[pl.BlockSpec((None, seq, LANES), lambda i: (i, 0, 0))],
        out_specs=pl.BlockSpec((None, seq, LANES), lambda i: (i, 0, 0)),
        out_shape=jax.ShapeDtypeStruct((b, seq, LANES), F32),
        compiler_params=_params(("parallel",)),
        name="fox_cum",
    )(f3d)


def _fox_attn_body(q_ref, k_ref, v_ref, ck_ref, qw_ref, kw_ref, o_ref,
                   kn_ref, m_ref, l_ref, acc_ref, *, tq, scale):
    seq = k_ref.shape[0]
    nq = seq // tq
    log2e = 1.4426950408889634

    def q_block(i):
        rows = slice(i * tq, (i + 1) * tq)
        return (_rms_rows(q_ref[rows, :].astype(F32), qw_ref[...]) * (scale * log2e)).astype(BF16)

    def k_block(j):
        rows = slice(j * tq, (j + 1) * tq)
        kn_ref[rows, :] = _rms_rows(k_ref[rows, :].astype(F32), kw_ref[...]).astype(BF16)

    pairs = [(i, j) for i in range(nq) for j in range(i + 1)]
    qn = {}

    def logits(i, j):
        if j == 0:
            qn[i] = q_block(i)
        if j == i:
            k_block(j)
        return lax.dot_general(qn[i], kn_ref[j * tq:(j + 1) * tq, :],
                               (((1,), (1,)), ((), ())), preferred_element_type=F32)

    n_lt = tq // LANES
    s_next = logits(*pairs[0])
    for idx, (i, j) in enumerate(pairs):
        s = s_next
        if idx + 1 < len(pairs):
            s_next = logits(*pairs[idx + 1])
        c0 = ck_ref[i:i + 1, 0:1]
        dk = (ck_ref[j:j + 1, :] - c0) * log2e
        s_t = [s[:, t * LANES:(t + 1) * LANES] - dk[:, t * LANES:(t + 1) * LANES] for t in range(n_lt)]
        if i == j:
            r_i = lax.broadcasted_iota(jnp.int32, (tq, LANES), 0)
            c_i = lax.broadcasted_iota(jnp.int32, (tq, LANES), 1)
            s_t = [jnp.where(c_i + t * LANES <= r_i, s_t[t], NEG_BIG) for t in range(n_lt)]
        mx = functools.reduce(jnp.maximum, s_t)
        row_max = jnp.broadcast_to(jnp.max(mx, axis=-1, keepdims=True), (tq, LANES))
        v_blk = v_ref[j * tq:(j + 1) * tq, :]
        if j == 0:
            m_new = row_max
        else:
            m_old = m_ref[...]
            m_new = jnp.maximum(m_old, row_max)
        p_t = [jnp.exp2(s_t[t] - m_new) for t in range(n_lt)]
        p_sum = functools.reduce(jnp.add, p_t)
        pv = jnp.dot(jnp.concatenate([p.astype(BF16) for p in p_t], axis=1), v_blk,
                     preferred_element_type=F32)
        if j == 0:
            l_new, acc_new = p_sum, pv
        else:
            alpha = jnp.exp2(m_old - m_new)
            l_new = alpha * l_ref[...] + p_sum
            acc_new = alpha * acc_ref[...] + pv
        if j == i:
            l_row = jnp.sum(l_new, axis=-1, keepdims=True)
            o_ref[i * tq:(i + 1) * tq, :] = (acc_new / l_row).astype(BF16)
        else:
            m_ref[...] = m_new
            l_ref[...] = l_new
            acc_ref[...] = acc_new


def _fox_attn(proj3d, cum_k, q_w, k_w, *, heads, tq):
    b, seq, _ = proj3d.shape
    dh = FOX_HEAD_DIM
    nq = seq // tq
    return pl.pallas_call(
        functools.partial(_fox_attn_body, tq=tq, scale=dh ** -0.5),
        grid=(b, heads),
        in_specs=[pl.BlockSpec((None, seq, dh), lambda bi, h: (bi, 0, h)),
                  pl.BlockSpec((None, seq, dh), lambda bi, h: (bi, 0, heads + h)),
                  pl.BlockSpec((None, seq, dh), lambda bi, h: (bi, 0, 2 * heads + h)),
                  pl.BlockSpec((None, None, nq, tq), lambda bi, h: (bi, h, 0, 0)),
                  pl.BlockSpec((1, dh), lambda bi, h: (0, 0)),
                  pl.BlockSpec((1, dh), lambda bi, h: (0, 0))],
        out_specs=pl.BlockSpec((None, seq, dh), lambda bi, h: (bi, 0, h)),
        out_shape=jax.ShapeDtypeStruct((b, seq, heads * dh), BF16),
        scratch_shapes=[pltpu.VMEM((seq, dh), BF16),
                        pltpu.VMEM((tq, LANES), F32),
                        pltpu.VMEM((tq, LANES), F32),
                        pltpu.VMEM((tq, dh), F32)],
        compiler_params=_params(("parallel", "parallel")),
        name="fox_attn",
    )(proj3d, proj3d, proj3d, cum_k, q_w.reshape(1, dh), k_w.reshape(1, dh))


def _pool_body(u_ref, w_ref, s_ref, o_ref, sa_ref, sb_ref, *, windows):
    seq, cg = u_ref.shape
    halo = POOL_HALO
    grp = pl.program_id(1)
    zeros = jnp.zeros((halo, cg), F32)
    sa_ref[0:halo, :] = zeros
    sb_ref[0:halo, :] = zeros
    t_idx = lax.broadcasted_iota(jnp.int32, (seq, 1), 0)

    for gi, win in enumerate(windows):
        @pl.when(grp == gi)
        def _(win=win):
            u = u_ref[...].astype(F32)
            sa_ref[halo:halo + seq, :] = u
            bufs = (sa_ref, sb_ref)
            s = u
            shift = 1
            level = 0
            while shift < win:
                src = bufs[level % 2]
                s = src[halo:halo + seq, :] + src[halo - shift:halo - shift + seq, :]
                shift *= 2
                level += 1
                if shift < win:
                    bufs[level % 2][halo:halo + seq, :] = s
            cnt = jnp.minimum(t_idx + 1, win).astype(F32)
            diff = (s / cnt - u).astype(BF16)
            mixed = jnp.dot(diff, w_ref[...].astype(BF16), preferred_element_type=F32)
            o_ref[...] = (mixed * s_ref[...]).astype(BF16)


def _pool(proj3d, w_group_all, layer, scale, *, windows):
    b, seq, n2 = proj3d.shape
    e = n2 // 2
    ng = len(windows)
    cg = e // ng
    return pl.pallas_call(
        functools.partial(_pool_body, windows=windows),
        grid=(b, ng),
        in_specs=[pl.BlockSpec((None, seq, cg), lambda bi, g: (bi, 0, g)),
                  pl.BlockSpec((None, None, cg, cg), lambda bi, g: (layer, g, 0, 0)),
                  pl.BlockSpec((1, cg), lambda bi, g: (0, g))],
        out_specs=pl.BlockSpec((None, seq, cg), lambda bi, g: (bi, 0, g)),
        out_shape=jax.ShapeDtypeStruct((b, seq, e), BF16),
        scratch_shapes=[pltpu.VMEM((POOL_HALO + seq, cg), F32),
                        pltpu.VMEM((POOL_HALO + seq, cg), F32)],
        compiler_params=_params(("parallel", "parallel")),
        name="pool_mix",
    )(proj3d, w_group_all, scale.reshape(1, e))


def _pick(n, pref):
    t = min(n, pref)
    while n % t:
        t //= 2
    return t


def kernel(x, norm_w, out_proj, s5_in_proj, s5_a_re, s5_a_im, s5_log_dt, s5_b_re, s5_b_im, s5_c_re, s5_c_im, s5_d, s5_w_glu, s5_b_glu, fox_in_proj, fox_q_norm, fox_k_norm, fox_f_bias, pool_in_proj, pool_w_group, pool_scale):
    bsz, seq, d = x.shape
    depth = norm_w.shape[0]
    e = s5_d.shape[1]
    heads = e // FOX_HEAD_DIM
    m = bsz * seq
    n_mixers = 3

    tm = _pick(m, 1024)
    tn_in = _pick(2 * e, 1024)
    tm_out = _pick(m, 256)
    tm_glu = _pick(m, 512)
    tq = _pick(seq, 512)

    h = x.reshape(m, d)
    for i in range(depth):
        kind, j = i % n_mixers, i // n_mixers
        if kind == 0:
            wb, wct, toe, lam = _s5_prep(s5_a_re[j], s5_a_im[j], s5_log_dt[j], s5_b_re[j], s5_b_im[j],
                                         s5_c_re[j], s5_c_im[j], chunk=CHUNK)
            proj = _norm_proj(h, norm_w[i], s5_in_proj, j, n_out=2 * e, tm=tm, tn=tn_in)
            g = _s5_core(proj, s5_d[j], wb, wct, toe, lam, chunk=CHUNK, n_seq=bsz)
            y = _glu(g, s5_w_glu, j, s5_b_glu[j], tm=tm_glu)
            h = _out_proj(y, proj, 1, h, out_proj, i, tm=tm_out)
        elif kind == 1:
            n_main = 4 * e
            b_f = jnp.pad(fox_f_bias[j], (0, LANES - heads)).reshape(1, LANES)
            proj, f_logit = _norm_proj(h, norm_w[i], jnp.swapaxes(fox_in_proj, 1, 2), j, n_out=n_main,
                                       tm=tm, tn=tn_in, w_transposed=True, extra_bias=b_f)
            cum = _fox_cum(f_logit.reshape(bsz, seq, LANES), blk=_pick(seq, 256))
            cum_k = cum[:, :, :heads].transpose(0, 2, 1).reshape(bsz, heads, seq // tq, tq)
            y = _fox_attn(proj.reshape(bsz, seq, n_main), cum_k, fox_q_norm[j], fox_k_norm[j],
                          heads=heads, tq=tq)
            h = _out_proj(y.reshape(m, e), proj, 3, h, out_proj, i, tm=tm_out)
        else:
            proj = _norm_proj(h, norm_w[i], pool_in_proj, j, n_out=2 * e, tm=tm, tn=tn_in)
            y = _pool(proj.reshape(bsz, seq, 2 * e), pool_w_group, j, pool_scale[j], windows=POOL_WINDOWS)
            h = _out_proj(y.reshape(m, e), proj, 1, h, out_proj, i, tm=tm_out)
    return h.reshape(bsz, seq, d)
```

```python
import functools
import math

import jax
import jax.numpy as jnp
from jax import lax
from jax.experimental import pallas as pl
from jax.experimental.pallas import tpu as pltpu

F32 = jnp.float32
BF16 = jnp.bfloat16

LANES = 128
S5_GROUP = 16
S5_STATE = 64
TILE_GROUPS = LANES // S5_GROUP
TILE_STATES = TILE_GROUPS * S5_STATE
CHUNK = 8
FOX_HEAD_DIM = 128
POOL_WINDOWS = (2, 4, 8, 16)
POOL_HALO = 16
EPS = 1e-6
NEG_BIG = -1e30
VMEM_LIMIT = 56 * 1024 * 1024


def _params(semantics):
    return pltpu.CompilerParams(dimension_semantics=semantics, vmem_limit_bytes=VMEM_LIMIT)


def _rms_rows(x, w):
    ms = jnp.mean(x * x, axis=-1, keepdims=True)
    return x * lax.rsqrt(ms + EPS) * w


def _silu(z):
    return z * jax.nn.sigmoid(z)


def _lhs_cached(j, make_lhs, lhs_ref, emit):
    @pl.when(j == 0)
    def _():
        lhs = make_lhs()
        lhs_ref[...] = lhs
        emit(lhs)

    @pl.when(j > 0)
    def _():
        emit(lhs_ref[...])


def _norm_proj_body(x_ref, nw_ref, w_ref, wo_ref, *rest, w_transposed, n_extra):
    if n_extra:
        wf_ref, bf_ref, o_ref, wo_bf_ref, of_ref, xn_ref = rest
    else:
        o_ref, wo_bf_ref, xn_ref = rest
    j = pl.program_id(1)
    wo_bf_ref[...] = wo_ref[...].astype(BF16)
    w_contract = 1 if w_transposed else 0

    def matmul(xn, w):
        return lax.dot_general(xn, w, (((1,), (w_contract,)), ((), ())), preferred_element_type=F32)

    def emit(xn):
        o_ref[...] = matmul(xn, w_ref[...].astype(BF16)).astype(BF16)

    @pl.when(j == 0)
    def _():
        xn = _rms_rows(x_ref[...], nw_ref[...]).astype(BF16)
        xn_ref[...] = xn
        emit(xn)
        if n_extra:
            idx = lax.broadcasted_iota(jnp.int32, wf_ref.shape, 0 if w_transposed else 1)
            wf = jnp.where(idx < n_extra, wf_ref[...], 0.0).astype(BF16)
            of_ref[...] = matmul(xn, wf) + bf_ref[...]

    @pl.when(j > 0)
    def _():
        emit(xn_ref[...])


def _norm_proj(h2d, norm_w, w_all, layer, w_out_all, layer_out, *, n_out, tm, tn, w_transposed=False,
               extra_bias=None):
    m, d = h2d.shape
    nj = n_out // tn
    e_out, d_out = w_out_all.shape[1:]
    wo_rows = e_out // ((m // tm) * nj)
    if w_transposed:
        w_spec = pl.BlockSpec((None, tn, d), lambda i, j: (layer, j, 0))
        wf_spec = pl.BlockSpec((None, LANES, d), lambda i, j: (layer, n_out // LANES, 0))
        n_extra = w_all.shape[1] - n_out
    else:
        w_spec = pl.BlockSpec((None, d, tn), lambda i, j: (layer, 0, j))
        wf_spec = pl.BlockSpec((None, d, LANES), lambda i, j: (layer, 0, n_out // LANES))
        n_extra = w_all.shape[2] - n_out
    in_specs = [pl.BlockSpec((tm, d), lambda i, j: (i, 0)),
                pl.BlockSpec((1, d), lambda i, j: (0, 0)),
                w_spec,
                pl.BlockSpec((None, wo_rows, d_out), lambda i, j: (layer_out, i * nj + j, 0))]
    out_specs = [pl.BlockSpec((tm, tn), lambda i, j: (i, j)),
                 pl.BlockSpec((wo_rows, d_out), lambda i, j: (i * nj + j, 0))]
    out_shape = [jax.ShapeDtypeStruct((m, n_out), BF16),
                 jax.ShapeDtypeStruct((e_out, d_out), BF16)]
    args = [h2d, norm_w.reshape(1, d), w_all, w_out_all]
    if extra_bias is None:
        n_extra = 0
    else:
        in_specs += [wf_spec, pl.BlockSpec((1, LANES), lambda i, j: (0, 0))]
        out_specs += [pl.BlockSpec((tm, LANES), lambda i, j: (i, 0))]
        out_shape += [jax.ShapeDtypeStruct((m, LANES), F32)]
        args += [w_all, extra_bias]
    outs = pl.pallas_call(
        functools.partial(_norm_proj_body, w_transposed=w_transposed, n_extra=n_extra),
        grid=(m // tm, n_out // tn),
        in_specs=in_specs,
        out_specs=out_specs,
        out_shape=out_shape,
        scratch_shapes=[pltpu.VMEM((tm, d), BF16)],
        compiler_params=_params(("parallel", "arbitrary")),
        name="norm_proj",
    )(*args)
    return outs


def _out_proj_body(y_ref, z_ref, h_ref, w_ref, o_ref):
    a = (y_ref[...].astype(F32) * _silu(z_ref[...].astype(F32))).astype(BF16)
    o_ref[...] = h_ref[...] + jnp.dot(a, w_ref[...], preferred_element_type=F32)


def _out_proj(y2d, z2d, z_block, h2d, w_bf16, *, tm):
    m, e = y2d.shape
    d = w_bf16.shape[1]
    return pl.pallas_call(
        _out_proj_body,
        grid=(m // tm,),
        in_specs=[pl.BlockSpec((tm, e), lambda i: (i, 0)),
                  pl.BlockSpec((tm, e), lambda i: (i, z_block)),
                  pl.BlockSpec((tm, d), lambda i: (i, 0)),
                  pl.BlockSpec((e, d), lambda i: (0, 0), pipeline_mode=pl.Buffered(1))],
        out_specs=pl.BlockSpec((tm, d), lambda i: (i, 0)),
        out_shape=jax.ShapeDtypeStruct((m, d), F32),
        compiler_params=_params(("parallel",)),
        name="out_proj",
    )(y2d, z2d, h2d, w_bf16)


def _s5_prep_body(a_ref, bt_ref, c_ref, wb_ref, wct_ref, toe_ref, lam_ref, *, chunk):
    ns = TILE_STATES
    n_lt = ns // LANES

    def block_diag(x):
        x2 = jnp.concatenate([x, x], axis=1)
        grp_row = lax.broadcasted_iota(jnp.int32, (LANES, LANES), 0) // S5_GROUP
        grp_lane = lax.broadcasted_iota(jnp.int32, (LANES, LANES), 1) // S5_STATE
        per_tile = LANES // S5_STATE
        return jnp.concatenate([jnp.where(grp_row == grp_lane + per_tile * t, x2, 0.0)
                                for t in range(n_lt)], axis=1)

    ar, ai, ldt = a_ref[0:1, :], a_ref[1:2, :], a_ref[2:3, :]
    dt = jnp.exp(ldt)
    k_rows = lax.broadcasted_iota(jnp.int32, (chunk + 1, 1), 0).astype(F32)
    mag = jnp.exp(ar * dt * k_rows)
    ang = ai * dt * k_rows
    pr, pi = mag * jnp.cos(ang), mag * jnp.sin(ang)
    abar_r, abar_i = pr[1:2, :], pi[1:2, :]
    den = ar * ar + ai * ai
    xr = abar_r - 1.0
    fr = (xr * ar + abar_i * ai) / den
    fi = (abar_i * ar - xr * ai) / den
    bt_re, bt_im = block_diag(bt_ref[0]), block_diag(bt_ref[1])
    bbar_re = fr * bt_re - fi * bt_im
    bbar_im = fr * bt_im + fi * bt_re
    c_re, c_im = block_diag(c_ref[0]), block_diag(c_ref[1])
    lam_ref[:, 0:ns] = pr[chunk:chunk + 1, :]
    lam_ref[:, ns:2 * ns] = pi[chunk:chunk + 1, :]

    for tau in range(chunk):
        qr, qi = pr[tau:tau + 1, :], pi[tau:tau + 1, :]
        g_re = (bbar_re * qr - bbar_im * qi).astype(BF16)
        g_im = (bbar_re * qi + bbar_im * qr).astype(BF16)
        t = chunk - 1 - tau
        wb_ref[t * LANES:(t + 1) * LANES, 0:ns] = g_re
        wb_ref[t * LANES:(t + 1) * LANES, ns:2 * ns] = g_im
        qr, qi = pr[tau + 1:tau + 2, :], pi[tau + 1:tau + 2, :]
        wct_ref[tau * LANES:(tau + 1) * LANES, 0:ns] = (c_re * qr - c_im * qi).astype(BF16)
        wct_ref[tau * LANES:(tau + 1) * LANES, ns:2 * ns] = (-(c_re * qi + c_im * qr)).astype(BF16)

    c_cat = jnp.concatenate([c_re, -c_im], axis=1).astype(BF16)
    k_all = lax.dot_general(wb_ref[...], c_cat, (((1,), (1,)), ((), ())),
                            preferred_element_type=F32)
    zero_blk = jnp.zeros((LANES, LANES), BF16)
    for tau in range(chunk):
        t = chunk - 1 - tau
        k_tau = k_all[t * LANES:(t + 1) * LANES, :].astype(BF16)
        toe_ref[0:LANES, tau * LANES:(tau + 1) * LANES] = k_tau
        if tau + 1 < chunk:
            toe_ref[LANES:2 * LANES, (tau + 1) * LANES:(tau + 2) * LANES] = k_tau
    toe_ref[LANES:2 * LANES, 0:LANES] = zero_blk


def _s5_prep(a_re, a_im, log_dt, b_re, b_im, c_re, c_im, *, chunk):
    g, p = a_re.shape
    nt = g // TILE_GROUPS
    ns = TILE_STATES
    ldt = jnp.broadcast_to(log_dt[:, None], (g, p))
    a_rows = jnp.stack([a_re.reshape(nt, ns), a_im.reshape(nt, ns), ldt.reshape(nt, ns)], axis=1)
    bt = jnp.stack([b_re, b_im], axis=0).transpose(1, 0, 3, 2).reshape(nt, TILE_GROUPS, 2, S5_GROUP, p)
    bt = bt.transpose(0, 2, 1, 3, 4).reshape(nt, 2, LANES, p)
    cc = jnp.stack([c_re.reshape(nt, LANES, p), c_im.reshape(nt, LANES, p)], axis=1)
    kw = chunk * LANES
    return pl.pallas_call(
        functools.partial(_s5_prep_body, chunk=chunk),
        grid=(nt,),
        in_specs=[pl.BlockSpec((None, 3, ns), lambda j: (j, 0, 0)),
                  pl.BlockSpec((None, 2, LANES, p), lambda j: (j, 0, 0, 0)),
                  pl.BlockSpec((None, 2, LANES, p), lambda j: (j, 0, 0, 0))],
        out_specs=[pl.BlockSpec((None, kw, 2 * ns), lambda j: (j, 0, 0)),
                   pl.BlockSpec((None, kw, 2 * ns), lambda j: (j, 0, 0)),
                   pl.BlockSpec((None, 2 * LANES, kw), lambda j: (j, 0, 0)),
                   pl.BlockSpec((None, 1, 2 * ns), lambda j: (j, 0, 0))],
        out_shape=[jax.ShapeDtypeStruct((nt, kw, 2 * ns), BF16),
                   jax.ShapeDtypeStruct((nt, kw, 2 * ns), BF16),
                   jax.ShapeDtypeStruct((nt, 2 * LANES, kw), BF16),
                   jax.ShapeDtypeStruct((nt, 1, 2 * ns), F32)],
        compiler_params=_params(("parallel",)),
        name="s5_prep",
    )(a_rows, bt, cc)


def _s5_core_body(ubf_ref, d_ref, wb_ref, wct_ref, toe_ref, lam_ref, o_ref,
                  u_ref, ub_ref, slab_ref, hin_ref, hprev_ref, *, chunk, n_seq, n_chunks):
    ns = TILE_STATES
    nk = ns // LANES
    seq = n_chunks * chunk

    def token_rows(b, t):
        return pl.ds(b * seq + t, n_chunks, stride=chunk)

    def chunk_rows(b):
        return pl.ds(b, n_chunks, stride=n_seq)

    u_ref[...] = ubf_ref[...].astype(F32)
    for t in range(chunk):
        for b in range(n_seq):
            slab_ref[t, chunk_rows(b), :] = u_ref[token_rows(b, t), :]
        ub_ref[:, t * LANES:(t + 1) * LANES] = slab_ref[t].astype(BF16)

    h_in = jnp.dot(ub_ref[...], wb_ref[...], preferred_element_type=F32)
    for k in range(2 * nk):
        hin_ref[k] = h_in[:, k * LANES:(k + 1) * LANES]

    for a in range(chunk // 2):
        lo = 2 * a * LANES
        width = (chunk - 2 * a) * LANES
        part = jnp.dot(ub_ref[:, lo:lo + 2 * LANES], toe_ref[:, 0:width], preferred_element_type=F32)
        for i in range(chunk - 2 * a):
            if a == 0:
                slab_ref[i] = part[:, i * LANES:(i + 1) * LANES]
            else:
                slab_ref[2 * a + i] += part[:, i * LANES:(i + 1) * LANES]

    lam = [jnp.broadcast_to(lam_ref[:, k * LANES:(k + 1) * LANES], (n_seq, LANES)) for k in range(2 * nk)]
    state = [jnp.zeros((n_seq, LANES), F32) for _ in range(2 * nk)]
    for c in range(n_chunks):
        rows = pl.ds(c * n_seq, n_seq)
        new_state = []
        for k in range(nk):
            hr, hi = state[k], state[nk + k]
            hprev_ref[k, rows, :] = hr
            hprev_ref[nk + k, rows, :] = hi
            lr, li = lam[k], lam[nk + k]
            new_state.append((lr * hr - li * hi + hin_ref[k, rows, :],
                              lr * hi + li * hr + hin_ref[nk + k, rows, :]))
        state = [s[0] for s in new_state] + [s[1] for s in new_state]

    h_prev = jnp.concatenate([hprev_ref[k].astype(BF16) for k in range(2 * nk)], axis=1)
    for a in range(chunk // 2):
        lo = 2 * a * LANES
        y = lax.dot_general(h_prev, wct_ref[lo:lo + 2 * LANES, :], (((1,), (1,)), ((), ())),
                            preferred_element_type=F32)
        for i in range(2):
            t = 2 * a + i
            slab_ref[t] += y[:, i * LANES:(i + 1) * LANES]
            for b in range(n_seq):
                y_bt = slab_ref[t, chunk_rows(b), :] + d_ref[...] * u_ref[token_rows(b, t), :]
                o_ref[token_rows(b, t), :] = y_bt


def _s5_core(proj2d, d_skip, wb, wct, toe, lam, *, chunk, n_seq):
    m, n2 = proj2d.shape
    e = n2 // 2
    r = m // chunk
    nt = e // LANES
    ns2 = 2 * TILE_STATES
    kw = chunk * LANES
    return pl.pallas_call(
        functools.partial(_s5_core_body, chunk=chunk, n_seq=n_seq, n_chunks=r // n_seq),
        grid=(nt,),
        in_specs=[pl.BlockSpec((m, LANES), lambda j: (0, j)),
                  pl.BlockSpec((1, LANES), lambda j: (0, j)),
                  pl.BlockSpec((None, kw, ns2), lambda j: (j, 0, 0)),
                  pl.BlockSpec((None, kw, ns2), lambda j: (j, 0, 0)),
                  pl.BlockSpec((None, 2 * LANES, kw), lambda j: (j, 0, 0)),
                  pl.BlockSpec((None, 1, ns2), lambda j: (j, 0, 0))],
        out_specs=pl.BlockSpec((m, LANES), lambda j: (0, j)),
        out_shape=jax.ShapeDtypeStruct((m, e), F32),
        scratch_shapes=[pltpu.VMEM((m, LANES), F32),
                        pltpu.VMEM((r, kw), BF16),
                        pltpu.VMEM((chunk, r, LANES), F32),
                        pltpu.VMEM((ns2 // LANES, r, LANES), F32),
                        pltpu.VMEM((ns2 // LANES, r, LANES), F32)],
        compiler_params=_params(("parallel",)),
        name="s5_core",
    )(proj2d, d_skip.reshape(1, e), wb, wct, toe, lam)


def _glu_body(y_ref, w_ref, b_ref, o_ref, wbf_ref):
    @pl.when(pl.program_id(0) == 0)
    def _():
        wbf_ref[...] = w_ref[...].astype(BF16)

    g = jax.nn.gelu(y_ref[...])
    acc = jnp.dot(g.astype(BF16), wbf_ref[...], preferred_element_type=F32) + b_ref[...]
    o_ref[...] = (g * jax.nn.sigmoid(acc)).astype(BF16)


def _glu(y2d, w_all, layer, b, *, tm):
    m, e = y2d.shape
    return pl.pallas_call(
        _glu_body,
        grid=(m // tm,),
        in_specs=[pl.BlockSpec((tm, e), lambda i: (i, 0)),
                  pl.BlockSpec((None, e, e), lambda i: (layer, 0, 0), pipeline_mode=pl.Buffered(1)),
                  pl.BlockSpec((1, e), lambda i: (0, 0))],
        out_specs=pl.BlockSpec((tm, e), lambda i: (i, 0)),
        out_shape=jax.ShapeDtypeStruct((m, e), BF16),
        scratch_shapes=[pltpu.VMEM((e, e), BF16)],
        compiler_params=_params(("arbitrary",)),
        name="s5_glu",
    )(y2d, w_all, b.reshape(1, e))


def _fox_cum_body(f_ref, o_ref, *, blk):
    seq = f_ref.shape[0]
    row = lax.broadcasted_iota(jnp.int32, (blk, blk), 0)
    col = lax.broadcasted_iota(jnp.int32, (blk, blk), 1)
    tri = (col <= row).astype(F32)
    carry = jnp.zeros((1, LANES), F32)
    for i in range(seq // blk):
        x = f_ref[i * blk:(i + 1) * blk, :]
        ls = jnp.minimum(x, 0.0) - jnp.log1p(jnp.exp(-jnp.abs(x)))
        cum = jnp.dot(tri, ls, preferred_element_type=F32,
                      precision=lax.Precision.HIGHEST) + carry
        o_ref[i * blk:(i + 1) * blk, :] = cum
        carry = cum[blk - 1:blk, :]


def _fox_cum(f3d, *, blk):
    b, seq, _ = f3d.shape
    return pl.pallas_call(
        functools.partial(_fox_cum_body, blk=blk),
        grid=(b,),
        in_specs=[pl.BlockSpec((None, seq, LANES), lambda i: (i, 0, 0))],
        out_specs=pl.BlockSpec((None, seq, LANES), lambda i: (i, 0, 0)),
        out_shape=jax.ShapeDtypeStruct((b, seq, LANES), F32),
        compiler_params=_params(("parallel",)),
        name="fox_cum",
    )(f3d)


def _fox_attn_body(q_ref, k_ref, v_ref, ck_ref, qw_ref, kw_ref, o_ref,
                   kn_ref, m_ref, acc_ref, *, tq, scale):
    seq = k_ref.shape[0]
    nq = seq // tq
    log2e = 1.4426950408889634

    def q_block(i):
        rows = slice(i * tq, (i + 1) * tq)
        return (_rms_rows(q_ref[rows, :].astype(F32), qw_ref[...]) * (scale * log2e)).astype(BF16)

    def k_block(j):
        rows = slice(j * tq, (j + 1) * tq)
        kn_ref[rows, :] = _rms_rows(k_ref[rows, :].astype(F32), kw_ref[...]).astype(BF16)

    pairs = [(i, j) for i in range(nq) for j in range(i + 1)]
    qn = {}

    def logits(i, j):
        if j == 0:
            qn[i] = q_block(i)
        if j == i:
            k_block(j)
        return lax.dot_general(qn[i], kn_ref[j * tq:(j + 1) * tq, :],
                               (((1,), (1,)), ((), ())), preferred_element_type=F32)

    n_lt = tq // LANES
    ones_blk = jnp.ones((tq, LANES), BF16)
    s_next = logits(*pairs[0])
    for idx, (i, j) in enumerate(pairs):
        s = s_next
        if idx + 1 < len(pairs):
            s_next = logits(*pairs[idx + 1])
        c0 = ck_ref[i:i + 1, 0:1]
        dk = (ck_ref[j:j + 1, :] - c0) * log2e
        s_t = [s[:, t * LANES:(t + 1) * LANES] - dk[:, t * LANES:(t + 1) * LANES] for t in range(n_lt)]
        if i == j:
            r_i = lax.broadcasted_iota(jnp.int32, (tq, LANES), 0)
            c_i = lax.broadcasted_iota(jnp.int32, (tq, LANES), 1)
            s_t = [jnp.where(c_i + t * LANES <= r_i, s_t[t], NEG_BIG) for t in range(n_lt)]
        mx = functools.reduce(jnp.maximum, s_t)
        row_max = jnp.broadcast_to(jnp.max(mx, axis=-1, keepdims=True), (tq, LANES))
        v_blk = v_ref[j * tq:(j + 1) * tq, :]
        if j == 0:
            m_new = row_max
        else:
            m_old = m_ref[...]
            m_new = jnp.maximum(m_old, row_max)
        p = jnp.concatenate([jnp.exp2(s_t[t] - m_new).astype(BF16) for t in range(n_lt)], axis=1)
        pv = jnp.dot(p, jnp.concatenate([v_blk, ones_blk], axis=1), preferred_element_type=F32)
        if j == 0:
            acc_new = pv
        else:
            alpha = jnp.exp2(m_old - m_new)
            acc_new = jnp.concatenate([alpha, alpha], axis=1) * acc_ref[...] + pv
        if j == i:
            o_ref[i * tq:(i + 1) * tq, :] = (acc_new[:, 0:LANES] / acc_new[:, LANES:2 * LANES]).astype(BF16)
        else:
            m_ref[...] = m_new
            acc_ref[...] = acc_new


def _fox_attn(proj3d, cum_k, q_w, k_w, *, heads, tq):
    b, seq, _ = proj3d.shape
    dh = FOX_HEAD_DIM
    nq = seq // tq
    return pl.pallas_call(
        functools.partial(_fox_attn_body, tq=tq, scale=dh ** -0.5),
        grid=(b, heads),
        in_specs=[pl.BlockSpec((None, seq, dh), lambda bi, h: (bi, 0, h)),
                  pl.BlockSpec((None, seq, dh), lambda bi, h: (bi, 0, heads + h)),
                  pl.BlockSpec((None, seq, dh), lambda bi, h: (bi, 0, 2 * heads + h)),
                  pl.BlockSpec((None, None, nq, tq), lambda bi, h: (bi, h, 0, 0)),
                  pl.BlockSpec((1, dh), lambda bi, h: (0, 0)),
                  pl.BlockSpec((1, dh), lambda bi, h: (0, 0))],
        out_specs=pl.BlockSpec((None, seq, dh), lambda bi, h: (bi, 0, h)),
        out_shape=jax.ShapeDtypeStruct((b, seq, heads * dh), BF16),
        scratch_shapes=[pltpu.VMEM((seq, dh), BF16),
                        pltpu.VMEM((tq, LANES), F32),
                        pltpu.VMEM((tq, dh + LANES), F32)],
        compiler_params=_params(("parallel", "parallel")),
        name="fox_attn",
    )(proj3d, proj3d, proj3d, cum_k, q_w.reshape(1, dh), k_w.reshape(1, dh))


def _pool_body(u_ref, w_ref, s_ref, o_ref, sa_ref, sb_ref, *, windows):
    seq, cg = u_ref.shape
    halo = POOL_HALO
    grp = pl.program_id(1)
    zeros = jnp.zeros((halo, cg), F32)
    sa_ref[0:halo, :] = zeros
    sb_ref[0:halo, :] = zeros
    t_idx = lax.broadcasted_iota(jnp.int32, (seq, 1), 0)

    for gi, win in enumerate(windows):
        @pl.when(grp == gi)
        def _(win=win):
            u = u_ref[...].astype(F32)
            sa_ref[halo:halo + seq, :] = u
            bufs = (sa_ref, sb_ref)
            s = u
            shift = 1
            level = 0
            while shift < win:
                src = bufs[level % 2]
                s = src[halo:halo + seq, :] + src[halo - shift:halo - shift + seq, :]
                shift *= 2
                level += 1
                if shift < win:
                    bufs[level % 2][halo:halo + seq, :] = s
            cnt = jnp.minimum(t_idx + 1, win).astype(F32)
            diff = (s / cnt - u).astype(BF16)
            mixed = jnp.dot(diff, w_ref[...].astype(BF16), preferred_element_type=F32)
            o_ref[...] = (mixed * s_ref[...]).astype(BF16)


def _pool(proj3d, w_group_all, layer, scale, *, windows):
    b, seq, n2 = proj3d.shape
    e = n2 // 2
    ng = len(windows)
    cg = e // ng
    return pl.pallas_call(
        functools.partial(_pool_body, windows=windows),
        grid=(b, ng),
        in_specs=[pl.BlockSpec((None, seq, cg), lambda bi, g: (bi, 0, g)),
                  pl.BlockSpec((None, None, cg, cg), lambda bi, g: (layer, g, 0, 0)),
                  pl.BlockSpec((1, cg), lambda bi, g: (0, g))],
        out_specs=pl.BlockSpec((None, seq, cg), lambda bi, g: (bi, 0, g)),
        out_shape=jax.ShapeDtypeStruct((b, seq, e), BF16),
        scratch_shapes=[pltpu.VMEM((POOL_HALO + seq, cg), F32),
                        pltpu.VMEM((POOL_HALO + seq, cg), F32)],
        compiler_params=_params(("parallel", "parallel")),
        name="pool_mix",
    )(proj3d, w_group_all, scale.reshape(1, e))


def _pick(n, pref):
    t = min(n, pref)
    while n % t:
        t //= 2
    return t


def kernel(x, norm_w, out_proj, s5_in_proj, s5_a_re, s5_a_im, s5_log_dt, s5_b_re, s5_b_im, s5_c_re, s5_c_im, s5_d, s5_w_glu, s5_b_glu, fox_in_proj, fox_q_norm, fox_k_norm, fox_f_bias, pool_in_proj, pool_w_group, pool_scale):
    bsz, seq, d = x.shape
    depth = norm_w.shape[0]
    e = s5_d.shape[1]
    heads = e // FOX_HEAD_DIM
    m = bsz * seq
    n_mixers = 3

    tm = _pick(m, 1024)
    tn_in = _pick(2 * e, 1024)
    tm_out = _pick(m, 512)
    tm_glu = _pick(m, 512)
    tq = _pick(seq, 512)

    h = x.reshape(m, d)
    for i in range(depth):
        kind, j = i % n_mixers, i // n_mixers
        if kind == 0:
            wb, wct, toe, lam = _s5_prep(s5_a_re[j], s5_a_im[j], s5_log_dt[j], s5_b_re[j], s5_b_im[j],
                                         s5_c_re[j], s5_c_im[j], chunk=CHUNK)
            proj, w_out = _norm_proj(h, norm_w[i], s5_in_proj, j, out_proj, i, n_out=2 * e, tm=tm, tn=tn_in)
            g = _s5_core(proj, s5_d[j], wb, wct, toe, lam, chunk=CHUNK, n_seq=bsz)
            y = _glu(g, s5_w_glu, j, s5_b_glu[j], tm=tm_glu)
            h = _out_proj(y, proj, 1, h, w_out, tm=tm_out)
        elif kind == 1:
            n_main = 4 * e
            b_f = jnp.pad(fox_f_bias[j], (0, LANES - heads)).reshape(1, LANES)
            proj, w_out, f_logit = _norm_proj(h, norm_w[i], jnp.swapaxes(fox_in_proj, 1, 2), j, out_proj, i,
                                              n_out=n_main, tm=tm, tn=tn_in, w_transposed=True, extra_bias=b_f)
            cum = _fox_cum(f_logit.reshape(bsz, seq, LANES), blk=_pick(seq, 256))
            cum_k = cum[:, :, :heads].transpose(0, 2, 1).reshape(bsz, heads, seq // tq, tq)
            y = _fox_attn(proj.reshape(bsz, seq, n_main), cum_k, fox_q_norm[j], fox_k_norm[j],
                          heads=heads, tq=tq)
            h = _out_proj(y.reshape(m, e), proj, 3, h, w_out, tm=tm_out)
        else:
            proj, w_out = _norm_proj(h, norm_w[i], pool_in_proj, j, out_proj, i, n_out=2 * e, tm=tm, tn=tn_in)
            y = _pool(proj.reshape(bsz, seq, 2 * e), pool_w_group, j, pool_scale[j], windows=POOL_WINDOWS)
            h = _out_proj(y.reshape(m, e), proj, 1, h, w_out, tm=tm_out)
    return h.reshape(bsz, seq, d)
```

```python
import functools

import jax
import jax.numpy as jnp
from jax import lax
from jax.experimental import pallas as pl
from jax.experimental.pallas import tpu as pltpu

F32 = jnp.float32
BF16 = jnp.bfloat16

LANES = 128
S5_GROUP = 16
S5_STATE = 64
TILE_GROUPS = LANES // S5_GROUP
TILE_STATES = TILE_GROUPS * S5_STATE
CHUNK = 8
FOX_HEAD_DIM = 128
POOL_WINDOWS = (2, 4, 8, 16)
POOL_HALO = 16
EPS = 1e-6
NEG_BIG = -1e30
VMEM_LIMIT = 56 * 1024 * 1024


def _params(semantics):
    return pltpu.CompilerParams(dimension_semantics=semantics, vmem_limit_bytes=VMEM_LIMIT)


def _rms_rows(x, w):
    ms = jnp.mean(x * x, axis=-1, keepdims=True)
    return x * lax.rsqrt(ms + EPS) * w


def _silu(z):
    return z * jax.nn.sigmoid(z)


def _norm_proj_body(x_ref, nw_ref, w_ref, wo_ref, *rest, w_transposed, n_extra):
    if n_extra:
        wf_ref, bf_ref, o_ref, wo_bf_ref, of_ref, xn_ref = rest
    else:
        o_ref, wo_bf_ref, xn_ref = rest
    j = pl.program_id(1)
    wo_bf_ref[...] = wo_ref[...].astype(BF16)
    w_contract = 1 if w_transposed else 0

    def matmul(xn, w):
        return lax.dot_general(xn, w, (((1,), (w_contract,)), ((), ())), preferred_element_type=F32)

    def emit(xn):
        o_ref[...] = matmul(xn, w_ref[...].astype(BF16)).astype(BF16)

    @pl.when(j == 0)
    def _():
        xn = _rms_rows(x_ref[...], nw_ref[...]).astype(BF16)
        xn_ref[...] = xn
        emit(xn)
        if n_extra:
            idx = lax.broadcasted_iota(jnp.int32, wf_ref.shape, 0 if w_transposed else 1)
            wf = jnp.where(idx < n_extra, wf_ref[...], 0.0).astype(BF16)
            of_ref[...] = matmul(xn, wf) + bf_ref[...]

    @pl.when(j > 0)
    def _():
        emit(xn_ref[...])


def _norm_proj(h2d, norm_w, w_all, layer, w_out_all, layer_out, *, n_out, tm, tn, w_transposed=False,
               extra_bias=None):
    m, d = h2d.shape
    nj = n_out // tn
    e_out, d_out = w_out_all.shape[1:]
    wo_rows = e_out // ((m // tm) * nj)
    if w_transposed:
        w_spec = pl.BlockSpec((None, tn, d), lambda i, j: (layer, j, 0))
        wf_spec = pl.BlockSpec((None, LANES, d), lambda i, j: (layer, n_out // LANES, 0))
        n_extra = w_all.shape[1] - n_out
    else:
        w_spec = pl.BlockSpec((None, d, tn), lambda i, j: (layer, 0, j))
        wf_spec = pl.BlockSpec((None, d, LANES), lambda i, j: (layer, 0, n_out // LANES))
        n_extra = w_all.shape[2] - n_out
    in_specs = [pl.BlockSpec((tm, d), lambda i, j: (i, 0)),
                pl.BlockSpec((1, d), lambda i, j: (0, 0)),
                w_spec,
                pl.BlockSpec((None, wo_rows, d_out), lambda i, j: (layer_out, i * nj + j, 0))]
    out_specs = [pl.BlockSpec((tm, tn), lambda i, j: (i, j)),
                 pl.BlockSpec((wo_rows, d_out), lambda i, j: (i * nj + j, 0))]
    out_shape = [jax.ShapeDtypeStruct((m, n_out), BF16),
                 jax.ShapeDtypeStruct((e_out, d_out), BF16)]
    args = [h2d, norm_w.reshape(1, d), w_all, w_out_all]
    if extra_bias is None:
        n_extra = 0
    else:
        in_specs += [wf_spec, pl.BlockSpec((1, LANES), lambda i, j: (0, 0))]
        out_specs += [pl.BlockSpec((tm, LANES), lambda i, j: (i, 0))]
        out_shape += [jax.ShapeDtypeStruct((m, LANES), F32)]
        args += [w_all, extra_bias]
    outs = pl.pallas_call(
        functools.partial(_norm_proj_body, w_transposed=w_transposed, n_extra=n_extra),
        grid=(m // tm, n_out // tn),
        in_specs=in_specs,
        out_specs=out_specs,
        out_shape=out_shape,
        scratch_shapes=[pltpu.VMEM((tm, d), BF16)],
        compiler_params=_params(("parallel", "arbitrary")),
        name="norm_proj",
    )(*args)
    return outs


def _out_proj_body(y_ref, z_ref, h_ref, w_ref, o_ref):
    a = (y_ref[...].astype(F32) * _silu(z_ref[...].astype(F32))).astype(BF16)
    o_ref[...] = h_ref[...] + jnp.dot(a, w_ref[...], preferred_element_type=F32)


def _out_proj(y2d, z2d, z_block, h2d, w_bf16, *, tm):
    m, e = y2d.shape
    d = w_bf16.shape[1]
    return pl.pallas_call(
        _out_proj_body,
        grid=(m // tm,),
        in_specs=[pl.BlockSpec((tm, e), lambda i: (i, 0)),
                  pl.BlockSpec((tm, e), lambda i: (i, z_block)),
                  pl.BlockSpec((tm, d), lambda i: (i, 0)),
                  pl.BlockSpec((e, d), lambda i: (0, 0), pipeline_mode=pl.Buffered(1))],
        out_specs=pl.BlockSpec((tm, d), lambda i: (i, 0)),
        out_shape=jax.ShapeDtypeStruct((m, d), F32),
        compiler_params=_params(("parallel",)),
        name="out_proj",
    )(y2d, z2d, h2d, w_bf16)


def _s5_prep_body(a_ref, bt_ref, c_ref, wb_ref, wct_ref, toe_ref, lam_ref, *, chunk):
    ns = TILE_STATES
    n_lt = ns // LANES

    def block_diag(x):
        x2 = jnp.concatenate([x, x], axis=1)
        grp_row = lax.broadcasted_iota(jnp.int32, (LANES, LANES), 0) // S5_GROUP
        grp_lane = lax.broadcasted_iota(jnp.int32, (LANES, LANES), 1) // S5_STATE
        per_tile = LANES // S5_STATE
        return jnp.concatenate([jnp.where(grp_row == grp_lane + per_tile * t, x2, 0.0)
                                for t in range(n_lt)], axis=1)

    ar, ai, ldt = a_ref[0:1, :], a_ref[1:2, :], a_ref[2:3, :]
    dt = jnp.exp(ldt)
    k_rows = lax.broadcasted_iota(jnp.int32, (chunk + 1, 1), 0).astype(F32)
    mag = jnp.exp(ar * dt * k_rows)
    ang = ai * dt * k_rows
    pr, pi = mag * jnp.cos(ang), mag * jnp.sin(ang)
    abar_r, abar_i = pr[1:2, :], pi[1:2, :]
    den = ar * ar + ai * ai
    xr = abar_r - 1.0
    fr = (xr * ar + abar_i * ai) / den
    fi = (abar_i * ar - xr * ai) / den
    bt_re, bt_im = block_diag(bt_ref[0]), block_diag(bt_ref[1])
    bbar_re = fr * bt_re - fi * bt_im
    bbar_im = fr * bt_im + fi * bt_re
    c_re, c_im = block_diag(c_ref[0]), block_diag(c_ref[1])
    lam_ref[:, 0:ns] = pr[chunk:chunk + 1, :]
    lam_ref[:, ns:2 * ns] = pi[chunk:chunk + 1, :]

    for tau in range(chunk):
        qr, qi = pr[tau:tau + 1, :], pi[tau:tau + 1, :]
        g_re = (bbar_re * qr - bbar_im * qi).astype(BF16)
        g_im = (bbar_re * qi + bbar_im * qr).astype(BF16)
        t = chunk - 1 - tau
        wb_ref[t * LANES:(t + 1) * LANES, 0:ns] = g_re
        wb_ref[t * LANES:(t + 1) * LANES, ns:2 * ns] = g_im
        qr, qi = pr[tau + 1:tau + 2, :], pi[tau + 1:tau + 2, :]
        wct_ref[tau * LANES:(tau + 1) * LANES, 0:ns] = (c_re * qr - c_im * qi).astype(BF16)
        wct_ref[tau * LANES:(tau + 1) * LANES, ns:2 * ns] = (-(c_re * qi + c_im * qr)).astype(BF16)

    c_cat = jnp.concatenate([c_re, -c_im], axis=1).astype(BF16)
    k_all = lax.dot_general(wb_ref[...], c_cat, (((1,), (1,)), ((), ())),
                            preferred_element_type=F32)
    zero_blk = jnp.zeros((LANES, LANES), BF16)
    for tau in range(chunk):
        t = chunk - 1 - tau
        k_tau = k_all[t * LANES:(t + 1) * LANES, :].astype(BF16)
        toe_ref[0:LANES, tau * LANES:(tau + 1) * LANES] = k_tau
        if tau + 1 < chunk:
            toe_ref[LANES:2 * LANES, (tau + 1) * LANES:(tau + 2) * LANES] = k_tau
    toe_ref[LANES:2 * LANES, 0:LANES] = zero_blk


def _s5_prep(a_re, a_im, log_dt, b_re, b_im, c_re, c_im, *, chunk):
    g, p = a_re.shape
    nt = g // TILE_GROUPS
    ns = TILE_STATES
    ldt = jnp.broadcast_to(log_dt[:, None], (g, p))
    a_rows = jnp.stack([a_re.reshape(nt, ns), a_im.reshape(nt, ns), ldt.reshape(nt, ns)], axis=1)
    bt = jnp.stack([b_re, b_im], axis=0).transpose(1, 0, 3, 2).reshape(nt, TILE_GROUPS, 2, S5_GROUP, p)
    bt = bt.transpose(0, 2, 1, 3, 4).reshape(nt, 2, LANES, p)
    cc = jnp.stack([c_re.reshape(nt, LANES, p), c_im.reshape(nt, LANES, p)], axis=1)
    kw = chunk * LANES
    return pl.pallas_call(
        functools.partial(_s5_prep_body, chunk=chunk),
        grid=(nt,),
        in_specs=[pl.BlockSpec((None, 3, ns), lambda j: (j, 0, 0)),
                  pl.BlockSpec((None, 2, LANES, p), lambda j: (j, 0, 0, 0)),
                  pl.BlockSpec((None, 2, LANES, p), lambda j: (j, 0, 0, 0))],
        out_specs=[pl.BlockSpec((None, kw, 2 * ns), lambda j: (j, 0, 0)),
                   pl.BlockSpec((None, kw, 2 * ns), lambda j: (j, 0, 0)),
                   pl.BlockSpec((None, 2 * LANES, kw), lambda j: (j, 0, 0)),
                   pl.BlockSpec((None, 1, 2 * ns), lambda j: (j, 0, 0))],
        out_shape=[jax.ShapeDtypeStruct((nt, kw, 2 * ns), BF16),
                   jax.ShapeDtypeStruct((nt, kw, 2 * ns), BF16),
                   jax.ShapeDtypeStruct((nt, 2 * LANES, kw), BF16),
                   jax.ShapeDtypeStruct((nt, 1, 2 * ns), F32)],
        compiler_params=_params(("parallel",)),
        name="s5_prep",
    )(a_rows, bt, cc)


def _s5_core_body(ubf_ref, d_ref, wb_ref, wct_ref, toe_ref, lam_ref, o_ref,
                  u_ref, ub_ref, slab_ref, hin_ref, hprev_ref, *, chunk, n_seq, n_chunks):
    ns = TILE_STATES
    nk = ns // LANES
    seq = n_chunks * chunk

    def token_rows(b, t):
        return pl.ds(b * seq + t, n_chunks, stride=chunk)

    def chunk_rows(b):
        return pl.ds(b, n_chunks, stride=n_seq)

    u_ref[...] = ubf_ref[...].astype(F32)
    for t in range(chunk):
        for b in range(n_seq):
            slab_ref[t, chunk_rows(b), :] = u_ref[token_rows(b, t), :]
        ub_ref[:, t * LANES:(t + 1) * LANES] = slab_ref[t].astype(BF16)

    h_in = jnp.dot(ub_ref[...], wb_ref[...], preferred_element_type=F32)
    for k in range(2 * nk):
        hin_ref[k] = h_in[:, k * LANES:(k + 1) * LANES]

    for a in range(chunk // 2):
        lo = 2 * a * LANES
        width = (chunk - 2 * a) * LANES
        part = jnp.dot(ub_ref[:, lo:lo + 2 * LANES], toe_ref[:, 0:width], preferred_element_type=F32)
        for i in range(chunk - 2 * a):
            if a == 0:
                slab_ref[i] = part[:, i * LANES:(i + 1) * LANES]
            else:
                slab_ref[2 * a + i] += part[:, i * LANES:(i + 1) * LANES]

    lam = [jnp.broadcast_to(lam_ref[:, k * LANES:(k + 1) * LANES], (n_seq, LANES)) for k in range(2 * nk)]
    state = [jnp.zeros((n_seq, LANES), F32) for _ in range(2 * nk)]
    for c in range(n_chunks):
        rows = pl.ds(c * n_seq, n_seq)
        new_state = []
        for k in range(nk):
            hr, hi = state[k], state[nk + k]
            hprev_ref[k, rows, :] = hr
            hprev_ref[nk + k, rows, :] = hi
            lr, li = lam[k], lam[nk + k]
            new_state.append((lr * hr - li * hi + hin_ref[k, rows, :],
                              lr * hi + li * hr + hin_ref[nk + k, rows, :]))
        state = [s[0] for s in new_state] + [s[1] for s in new_state]

    h_prev = jnp.concatenate([hprev_ref[k].astype(BF16) for k in range(2 * nk)], axis=1)
    for a in range(chunk // 2):
        lo = 2 * a * LANES
        y = lax.dot_general(h_prev, wct_ref[lo:lo + 2 * LANES, :], (((1,), (1,)), ((), ())),
                            preferred_element_type=F32)
        for i in range(2):
            t = 2 * a + i
            slab_ref[t] += y[:, i * LANES:(i + 1) * LANES]
            for b in range(n_seq):
                y_bt = slab_ref[t, chunk_rows(b), :] + d_ref[...] * u_ref[token_rows(b, t), :]
                o_ref[token_rows(b, t), :] = y_bt


def _s5_core(proj2d, d_skip, wb, wct, toe, lam, *, chunk, n_seq):
    m, n2 = proj2d.shape
    e = n2 // 2
    r = m // chunk
    nt = e // LANES
    ns2 = 2 * TILE_STATES
    kw = chunk * LANES
    return pl.pallas_call(
        functools.partial(_s5_core_body, chunk=chunk, n_seq=n_seq, n_chunks=r // n_seq),
        grid=(nt,),
        in_specs=[pl.BlockSpec((m, LANES), lambda j: (0, j)),
                  pl.BlockSpec((1, LANES), lambda j: (0, j)),
                  pl.BlockSpec((None, kw, ns2), lambda j: (j, 0, 0)),
                  pl.BlockSpec((None, kw, ns2), lambda j: (j, 0, 0)),
                  pl.BlockSpec((None, 2 * LANES, kw), lambda j: (j, 0, 0)),
                  pl.BlockSpec((None, 1, ns2), lambda j: (j, 0, 0))],
        out_specs=pl.BlockSpec((m, LANES), lambda j: (0, j)),
        out_shape=jax.ShapeDtypeStruct((m, e), F32),
        scratch_shapes=[pltpu.VMEM((m, LANES), F32),
                        pltpu.VMEM((r, kw), BF16),
                        pltpu.VMEM((chunk, r, LANES), F32),
                        pltpu.VMEM((ns2 // LANES, r, LANES), F32),
                        pltpu.VMEM((ns2 // LANES, r, LANES), F32)],
        compiler_params=_params(("parallel",)),
        name="s5_core",
    )(proj2d, d_skip.reshape(1, e), wb, wct, toe, lam)


def _glu_body(y_ref, w_ref, b_ref, o_ref, wbf_ref):
    @pl.when(pl.program_id(0) == 0)
    def _():
        wbf_ref[...] = w_ref[...].astype(BF16)

    g = jax.nn.gelu(y_ref[...])
    acc = jnp.dot(g.astype(BF16), wbf_ref[...], preferred_element_type=F32) + b_ref[...]
    o_ref[...] = (g * jax.nn.sigmoid(acc)).astype(BF16)


def _glu(y2d, w_all, layer, b, *, tm):
    m, e = y2d.shape
    return pl.pallas_call(
        _glu_body,
        grid=(m // tm,),
        in_specs=[pl.BlockSpec((tm, e), lambda i: (i, 0)),
                  pl.BlockSpec((None, e, e), lambda i: (layer, 0, 0), pipeline_mode=pl.Buffered(1)),
                  pl.BlockSpec((1, e), lambda i: (0, 0))],
        out_specs=pl.BlockSpec((tm, e), lambda i: (i, 0)),
        out_shape=jax.ShapeDtypeStruct((m, e), BF16),
        scratch_shapes=[pltpu.VMEM((e, e), BF16)],
        compiler_params=_params(("arbitrary",)),
        name="s5_glu",
    )(y2d, w_all, b.reshape(1, e))


def _fox_cum_body(f_ref, o_ref, *, blk):
    seq = f_ref.shape[0]
    row = lax.broadcasted_iota(jnp.int32, (blk, blk), 0)
    col = lax.broadcasted_iota(jnp.int32, (blk, blk), 1)
    tri = (col <= row).astype(BF16)
    carry = jnp.zeros((1, LANES), F32)
    for i in range(seq // blk):
        x = f_ref[i * blk:(i + 1) * blk, :]
        ls = jnp.minimum(x, 0.0) - jnp.log1p(jnp.exp(-jnp.abs(x)))
        cum = carry
        rest = ls
        for _ in range(3):
            part = rest.astype(BF16)
            cum = cum + jnp.dot(tri, part, preferred_element_type=F32)
            rest = rest - part.astype(F32)
        o_ref[i * blk:(i + 1) * blk, :] = cum
        carry = cum[blk - 1:blk, :]


def _fox_cum(f3d, *, blk):
    b, seq, _ = f3d.shape
    return pl.pallas_call(
        functools.partial(_fox_cum_body, blk=blk),
        grid=(b,),
        in_specs=[pl.BlockSpec((None, seq, LANES), lambda i: (i, 0, 0))],
        out_specs=pl.BlockSpec((None, seq, LANES), lambda i: (i, 0, 0)),
        out_shape=jax.ShapeDtypeStruct((b, seq, LANES), F32),
        compiler_params=_params(("parallel",)),
        name="fox_cum",
    )(f3d)


def _fox_attn_body(q_ref, k_ref, v_ref, ck_ref, qw_ref, kw_ref, o_ref,
                   kn_ref, m_ref, acc_ref, *, tq, scale):
    seq = k_ref.shape[0]
    nq = seq // tq
    log2e = 1.4426950408889634

    def q_block(i):
        rows = slice(i * tq, (i + 1) * tq)
        return (_rms_rows(q_ref[rows, :].astype(F32), qw_ref[...]) * (scale * log2e)).astype(BF16)

    def k_block(j):
        rows = slice(j * tq, (j + 1) * tq)
        kn_ref[rows, :] = _rms_rows(k_ref[rows, :].astype(F32), kw_ref[...]).astype(BF16)

    pairs = [(i, j) for i in range(nq) for j in range(i + 1)]
    qn = {}

    def logits(i, j):
        if j == 0:
            qn[i] = q_block(i)
        if j == i:
            k_block(j)
        return lax.dot_general(qn[i], kn_ref[j * tq:(j + 1) * tq, :],
                               (((1,), (1,)), ((), ())), preferred_element_type=F32)

    n_lt = tq // LANES
    ones_blk = jnp.ones((tq, LANES), BF16)
    s_next = logits(*pairs[0])
    for idx, (i, j) in enumerate(pairs):
        s = s_next
        if idx + 1 < len(pairs):
            s_next = logits(*pairs[idx + 1])
        c0 = ck_ref[i:i + 1, 0:1]
        dk = (ck_ref[j:j + 1, :] - c0) * log2e
        s_t = [s[:, t * LANES:(t + 1) * LANES] - dk[:, t * LANES:(t + 1) * LANES] for t in range(n_lt)]
        if i == j:
            r_i = lax.broadcasted_iota(jnp.int32, (tq, LANES), 0)
            c_i = lax.broadcasted_iota(jnp.int32, (tq, LANES), 1)
            s_t = [jnp.where(c_i + t * LANES <= r_i, s_t[t], NEG_BIG) for t in range(n_lt)]
        mx = functools.reduce(jnp.maximum, s_t)
        row_max = jnp.broadcast_to(jnp.max(mx, axis=-1, keepdims=True), (tq, LANES))
        v_blk = v_ref[j * tq:(j + 1) * tq, :]
        if j == 0:
            m_new = row_max
        else:
            m_old = m_ref[...]
            m_new = jnp.maximum(m_old, row_max)
        p = jnp.concatenate([jnp.exp2(s_t[t] - m_new).astype(BF16) for t in range(n_lt)], axis=1)
        pv = jnp.dot(p, jnp.concatenate([v_blk, ones_blk], axis=1), preferred_element_type=F32)
        if j == 0:
            acc_new = pv
        else:
            alpha = jnp.exp2(m_old - m_new)
            acc_new = jnp.concatenate([alpha, alpha], axis=1) * acc_ref[...] + pv
        if j == i:
            o_ref[i * tq:(i + 1) * tq, :] = (acc_new[:, 0:LANES] / acc_new[:, LANES:2 * LANES]).astype(BF16)
        else:
            m_ref[...] = m_new
            acc_ref[...] = acc_new


def _fox_attn(proj3d, cum_k, q_w, k_w, *, heads, tq):
    b, seq, _ = proj3d.shape
    dh = FOX_HEAD_DIM
    nq = seq // tq
    return pl.pallas_call(
        functools.partial(_fox_attn_body, tq=tq, scale=dh ** -0.5),
        grid=(b, heads),
        in_specs=[pl.BlockSpec((None, seq, dh), lambda bi, h: (bi, 0, h)),
                  pl.BlockSpec((None, seq, dh), lambda bi, h: (bi, 0, heads + h)),
                  pl.BlockSpec((None, seq, dh), lambda bi, h: (bi, 0, 2 * heads + h)),
                  pl.BlockSpec((None, None, nq, tq), lambda bi, h: (bi, h, 0, 0)),
                  pl.BlockSpec((1, dh), lambda bi, h: (0, 0)),
                  pl.BlockSpec((1, dh), lambda bi, h: (0, 0))],
        out_specs=pl.BlockSpec((None, seq, dh), lambda bi, h: (bi, 0, h)),
        out_shape=jax.ShapeDtypeStruct((b, seq, heads * dh), BF16),
        scratch_shapes=[pltpu.VMEM((seq, dh), BF16),
                        pltpu.VMEM((tq, LANES), F32),
                        pltpu.VMEM((tq, dh + LANES), F32)],
        compiler_params=_params(("parallel", "parallel")),
        name="fox_attn",
    )(proj3d, proj3d, proj3d, cum_k, q_w.reshape(1, dh), k_w.reshape(1, dh))


def _pool_body(u_ref, w_ref, s_ref, o_ref, sa_ref, sb_ref, *, windows):
    seq, cg = u_ref.shape
    halo = POOL_HALO
    grp = pl.program_id(1)
    zeros = jnp.zeros((halo, cg), F32)
    sa_ref[0:halo, :] = zeros
    sb_ref[0:halo, :] = zeros
    t_idx = lax.broadcasted_iota(jnp.int32, (seq, 1), 0)

    for gi, win in enumerate(windows):
        @pl.when(grp == gi)
        def _(win=win):
            u = u_ref[...].astype(F32)
            sa_ref[halo:halo + seq, :] = u
            bufs = (sa_ref, sb_ref)
            s = u
            shift = 1
            level = 0
            while shift < win:
                src = bufs[level % 2]
                s = src[halo:halo + seq, :] + src[halo - shift:halo - shift + seq, :]
                shift *= 2
                level += 1
                if shift < win:
                    bufs[level % 2][halo:halo + seq, :] = s
            cnt = jnp.minimum(t_idx + 1, win).astype(F32)
            diff = (s / cnt - u).astype(BF16)
            mixed = jnp.dot(diff, w_ref[...].astype(BF16), preferred_element_type=F32)
            o_ref[...] = (mixed * s_ref[...]).astype(BF16)


def _pool(proj3d, w_group_all, layer, scale, *, windows):
    b, seq, n2 = proj3d.shape
    e = n2 // 2
    ng = len(windows)
    cg = e // ng
    return pl.pallas_call(
        functools.partial(_pool_body, windows=windows),
        grid=(b, ng),
        in_specs=[pl.BlockSpec((None, seq, cg), lambda bi, g: (bi, 0, g)),
                  pl.BlockSpec((None, None, cg, cg), lambda bi, g: (layer, g, 0, 0)),
                  pl.BlockSpec((1, cg), lambda bi, g: (0, g))],
        out_specs=pl.BlockSpec((None, seq, cg), lambda bi, g: (bi, 0, g)),
        out_shape=jax.ShapeDtypeStruct((b, seq, e), BF16),
        scratch_shapes=[pltpu.VMEM((POOL_HALO + seq, cg), F32),
                        pltpu.VMEM((POOL_HALO + seq, cg), F32)],
        compiler_params=_params(("parallel", "parallel")),
        name="pool_mix",
    )(proj3d, w_group_all, scale.reshape(1, e))


def _pick(n, pref):
    t = min(n, pref)
    while n % t:
        t //= 2
    return t


def kernel(x, norm_w, out_proj, s5_in_proj, s5_a_re, s5_a_im, s5_log_dt, s5_b_re, s5_b_im, s5_c_re, s5_c_im, s5_d, s5_w_glu, s5_b_glu, fox_in_proj, fox_q_norm, fox_k_norm, fox_f_bias, pool_in_proj, pool_w_group, pool_scale):
    bsz, seq, d = x.shape
    depth = norm_w.shape[0]
    e = s5_d.shape[1]
    heads = e // FOX_HEAD_DIM
    m = bsz * seq
    n_mixers = 3

    assert e % (LANES * len(POOL_WINDOWS)) == 0 and e % FOX_HEAD_DIM == 0 and heads <= LANES
    assert seq % CHUNK == 0 and s5_a_re.shape[1:] == (e // S5_GROUP, S5_STATE)
    assert fox_in_proj.shape[2] == 4 * e + heads and out_proj.shape[1:] == (e, d)

    tm = _pick(m, 1024)
    tn_in = _pick(2 * e, 1024)
    tm_out = _pick(m, 512)
    tm_glu = _pick(m, 512)
    tq = _pick(seq, 512)

    h = x.reshape(m, d)
    for i in range(depth):
        kind, j = i % n_mixers, i // n_mixers
        if kind == 0:
            wb, wct, toe, lam = _s5_prep(s5_a_re[j], s5_a_im[j], s5_log_dt[j], s5_b_re[j], s5_b_im[j],
                                         s5_c_re[j], s5_c_im[j], chunk=CHUNK)
            proj, w_out = _norm_proj(h, norm_w[i], s5_in_proj, j, out_proj, i, n_out=2 * e, tm=tm, tn=tn_in)
            g = _s5_core(proj, s5_d[j], wb, wct, toe, lam, chunk=CHUNK, n_seq=bsz)
            y = _glu(g, s5_w_glu, j, s5_b_glu[j], tm=tm_glu)
            h = _out_proj(y, proj, 1, h, w_out, tm=tm_out)
        elif kind == 1:
            n_main = 4 * e
            b_f = jnp.pad(fox_f_bias[j], (0, LANES - heads)).reshape(1, LANES)
            proj, w_out, f_logit = _norm_proj(h, norm_w[i], jnp.swapaxes(fox_in_proj, 1, 2), j, out_proj, i,
                                              n_out=n_main, tm=tm, tn=tn_in, w_transposed=True, extra_bias=b_f)
            cum = _fox_cum(f_logit.reshape(bsz, seq, LANES), blk=_pick(seq, 256))
            cum_k = cum[:, :, :heads].transpose(0, 2, 1).reshape(bsz, heads, seq // tq, tq)
            y = _fox_attn(proj.reshape(bsz, seq, n_main), cum_k, fox_q_norm[j], fox_k_norm[j],
                          heads=heads, tq=tq)
            h = _out_proj(y.reshape(m, e), proj, 3, h, w_out, tm=tm_out)
        else:
            proj, w_out = _norm_proj(h, norm_w[i], pool_in_proj, j, out_proj, i, n_out=2 * e, tm=tm, tn=tn_in)
            y = _pool(proj.reshape(bsz, seq, 2 * e), pool_w_group, j, pool_scale[j], windows=POOL_WINDOWS)
            h = _out_proj(y.reshape(m, e), proj, 1, h, w_out, tm=tm_out)
    return h.reshape(bsz, seq, d)
```

```python
import functools

import jax
import jax.numpy as jnp
from jax import lax
from jax.experimental import pallas as pl
from jax.experimental.pallas import tpu as pltpu

F32 = jnp.float32
BF16 = jnp.bfloat16

LANES = 128
S5_GROUP = 16
S5_STATE = 64
TILE_GROUPS = LANES // S5_GROUP
TILE_STATES = TILE_GROUPS * S5_STATE
CHUNK = 8
FOX_HEAD_DIM = 128
POOL_WINDOWS = (2, 4, 8, 16)
POOL_HALO = 16
EPS = 1e-6
NEG_BIG = -1e30
VMEM_LIMIT = 56 * 1024 * 1024


def _params(semantics):
    return pltpu.CompilerParams(dimension_semantics=semantics, vmem_limit_bytes=VMEM_LIMIT)


def _rms_rows(x, w):
    ms = jnp.mean(x * x, axis=-1, keepdims=True)
    return x * lax.rsqrt(ms + EPS) * w


def _silu(z):
    return z * jax.nn.sigmoid(z)


def _norm_proj_body(x_ref, nw_ref, w_ref, wo_ref, *rest, w_transposed, n_extra):
    if n_extra:
        wf_ref, bf_ref, o_ref, wo_bf_ref, of_ref, xn_ref = rest
    else:
        o_ref, wo_bf_ref, xn_ref = rest
    j = pl.program_id(1)
    wo_bf_ref[...] = wo_ref[...].astype(BF16)
    w_contract = 1 if w_transposed else 0

    def matmul(xn, w):
        return lax.dot_general(xn, w, (((1,), (w_contract,)), ((), ())), preferred_element_type=F32)

    def emit(xn):
        o_ref[...] = matmul(xn, w_ref[...].astype(BF16)).astype(BF16)

    @pl.when(j == 0)
    def _():
        xn = _rms_rows(x_ref[...], nw_ref[...]).astype(BF16)
        xn_ref[...] = xn
        emit(xn)
        if n_extra:
            idx = lax.broadcasted_iota(jnp.int32, wf_ref.shape, 0 if w_transposed else 1)
            wf = jnp.where(idx < n_extra, wf_ref[...], 0.0).astype(BF16)
            of_ref[...] = matmul(xn, wf) + bf_ref[...]

    @pl.when(j > 0)
    def _():
        emit(xn_ref[...])


def _norm_proj(h2d, norm_w, w_all, layer, w_out_all, layer_out, *, n_out, tm, tn, w_transposed=False,
               extra_bias=None):
    m, d = h2d.shape
    nj = n_out // tn
    e_out, d_out = w_out_all.shape[1:]
    wo_rows = e_out // ((m // tm) * nj)
    if w_transposed:
        w_spec = pl.BlockSpec((None, tn, d), lambda i, j: (layer, j, 0))
        wf_spec = pl.BlockSpec((None, LANES, d), lambda i, j: (layer, n_out // LANES, 0))
        n_extra = w_all.shape[1] - n_out
    else:
        w_spec = pl.BlockSpec((None, d, tn), lambda i, j: (layer, 0, j))
        wf_spec = pl.BlockSpec((None, d, LANES), lambda i, j: (layer, 0, n_out // LANES))
        n_extra = w_all.shape[2] - n_out
    in_specs = [pl.BlockSpec((tm, d), lambda i, j: (i, 0)),
                pl.BlockSpec((1, d), lambda i, j: (0, 0)),
                w_spec,
                pl.BlockSpec((None, wo_rows, d_out), lambda i, j: (layer_out, i * nj + j, 0))]
    out_specs = [pl.BlockSpec((tm, tn), lambda i, j: (i, j)),
                 pl.BlockSpec((wo_rows, d_out), lambda i, j: (i * nj + j, 0))]
    out_shape = [jax.ShapeDtypeStruct((m, n_out), BF16),
                 jax.ShapeDtypeStruct((e_out, d_out), BF16)]
    args = [h2d, norm_w.reshape(1, d), w_all, w_out_all]
    if extra_bias is None:
        n_extra = 0
    else:
        in_specs += [wf_spec, pl.BlockSpec((1, LANES), lambda i, j: (0, 0))]
        out_specs += [pl.BlockSpec((tm, LANES), lambda i, j: (i, 0))]
        out_shape += [jax.ShapeDtypeStruct((m, LANES), F32)]
        args += [w_all, extra_bias]
    outs = pl.pallas_call(
        functools.partial(_norm_proj_body, w_transposed=w_transposed, n_extra=n_extra),
        grid=(m // tm, n_out // tn),
        in_specs=in_specs,
        out_specs=out_specs,
        out_shape=out_shape,
        scratch_shapes=[pltpu.VMEM((tm, d), BF16)],
        compiler_params=_params(("parallel", "arbitrary")),
        name="norm_proj",
    )(*args)
    return outs


def _out_proj_body(y_ref, z_ref, h_ref, w_ref, o_ref):
    a = (y_ref[...].astype(F32) * _silu(z_ref[...].astype(F32))).astype(BF16)
    o_ref[...] = h_ref[...] + jnp.dot(a, w_ref[...], preferred_element_type=F32)


def _out_proj(y2d, z2d, z_block, h2d, w_bf16, *, tm):
    m, e = y2d.shape
    d = w_bf16.shape[1]
    return pl.pallas_call(
        _out_proj_body,
        grid=(m // tm,),
        in_specs=[pl.BlockSpec((tm, e), lambda i: (i, 0)),
                  pl.BlockSpec((tm, e), lambda i: (i, z_block)),
                  pl.BlockSpec((tm, d), lambda i: (i, 0)),
                  pl.BlockSpec((e, d), lambda i: (0, 0), pipeline_mode=pl.Buffered(1))],
        out_specs=pl.BlockSpec((tm, d), lambda i: (i, 0)),
        out_shape=jax.ShapeDtypeStruct((m, d), F32),
        compiler_params=_params(("parallel",)),
        name="out_proj",
    )(y2d, z2d, h2d, w_bf16)


def _s5_prep_body(a_ref, bt_ref, c_ref, d_ref, wb_ref, wct_ref, toe_ref, lam_ref, *, chunk):
    ns = TILE_STATES
    n_lt = ns // LANES

    def block_diag(x):
        x2 = jnp.concatenate([x, x], axis=1)
        grp_row = lax.broadcasted_iota(jnp.int32, (LANES, LANES), 0) // S5_GROUP
        grp_lane = lax.broadcasted_iota(jnp.int32, (LANES, LANES), 1) // S5_STATE
        per_tile = LANES // S5_STATE
        return jnp.concatenate([jnp.where(grp_row == grp_lane + per_tile * t, x2, 0.0)
                                for t in range(n_lt)], axis=1)

    ar, ai, ldt = a_ref[0:1, :], a_ref[1:2, :], a_ref[2:3, :]
    dt = jnp.exp(ldt)
    k_rows = lax.broadcasted_iota(jnp.int32, (chunk + 1, 1), 0).astype(F32)
    mag = jnp.exp(ar * dt * k_rows)
    ang = ai * dt * k_rows
    pr, pi = mag * jnp.cos(ang), mag * jnp.sin(ang)
    abar_r, abar_i = pr[1:2, :], pi[1:2, :]
    den = ar * ar + ai * ai
    xr = abar_r - 1.0
    fr = (xr * ar + abar_i * ai) / den
    fi = (abar_i * ar - xr * ai) / den
    bt_re, bt_im = block_diag(bt_ref[0]), block_diag(bt_ref[1])
    bbar_re = fr * bt_re - fi * bt_im
    bbar_im = fr * bt_im + fi * bt_re
    c_re, c_im = block_diag(c_ref[0]), block_diag(c_ref[1])
    lam_ref[:, 0:ns] = pr[chunk:chunk + 1, :]
    lam_ref[:, ns:2 * ns] = pi[chunk:chunk + 1, :]

    bb_re, bb_im = bbar_re.astype(BF16), bbar_im.astype(BF16)
    cb_re, cb_im = c_re.astype(BF16), c_im.astype(BF16)
    for tau in range(chunk):
        qr, qi = pr[tau:tau + 1, :].astype(BF16), pi[tau:tau + 1, :].astype(BF16)
        t = chunk - 1 - tau
        wb_ref[t * LANES:(t + 1) * LANES, 0:ns] = bb_re * qr - bb_im * qi
        wb_ref[t * LANES:(t + 1) * LANES, ns:2 * ns] = bb_re * qi + bb_im * qr
        qr, qi = pr[tau + 1:tau + 2, :].astype(BF16), pi[tau + 1:tau + 2, :].astype(BF16)
        wct_ref[tau * LANES:(tau + 1) * LANES, 0:ns] = cb_re * qr - cb_im * qi
        wct_ref[tau * LANES:(tau + 1) * LANES, ns:2 * ns] = -(cb_re * qi + cb_im * qr)

    c_cat = jnp.concatenate([cb_re, -cb_im], axis=1)
    k_all = lax.dot_general(wb_ref[...], c_cat, (((1,), (1,)), ((), ())),
                            preferred_element_type=F32)
    zero_blk = jnp.zeros((LANES, LANES), BF16)
    eye = lax.broadcasted_iota(jnp.int32, (LANES, LANES), 0) == lax.broadcasted_iota(jnp.int32, (LANES, LANES), 1)
    for tau in range(chunk):
        t = chunk - 1 - tau
        k_tau = k_all[t * LANES:(t + 1) * LANES, :]
        if tau == 0:
            k_tau = k_tau + jnp.where(eye, d_ref[...], 0.0)
        k_tau = k_tau.astype(BF16)
        toe_ref[0:LANES, tau * LANES:(tau + 1) * LANES] = k_tau
        if tau + 1 < chunk:
            toe_ref[LANES:2 * LANES, (tau + 1) * LANES:(tau + 2) * LANES] = k_tau
    toe_ref[LANES:2 * LANES, 0:LANES] = zero_blk


def _s5_prep(a_re, a_im, log_dt, b_re, b_im, c_re, c_im, d_skip, *, chunk):
    g, p = a_re.shape
    nt = g // TILE_GROUPS
    ns = TILE_STATES
    ldt = jnp.broadcast_to(log_dt[:, None], (g, p))
    a_rows = jnp.stack([a_re.reshape(nt, ns), a_im.reshape(nt, ns), ldt.reshape(nt, ns)], axis=1)
    bt = jnp.stack([b_re, b_im], axis=0).transpose(1, 0, 3, 2).reshape(nt, TILE_GROUPS, 2, S5_GROUP, p)
    bt = bt.transpose(0, 2, 1, 3, 4).reshape(nt, 2, LANES, p)
    cc = jnp.stack([c_re.reshape(nt, LANES, p), c_im.reshape(nt, LANES, p)], axis=1)
    kw = chunk * LANES
    return pl.pallas_call(
        functools.partial(_s5_prep_body, chunk=chunk),
        grid=(nt,),
        in_specs=[pl.BlockSpec((None, 3, ns), lambda j: (j, 0, 0)),
                  pl.BlockSpec((None, 2, LANES, p), lambda j: (j, 0, 0, 0)),
                  pl.BlockSpec((None, 2, LANES, p), lambda j: (j, 0, 0, 0)),
                  pl.BlockSpec((1, LANES), lambda j: (0, j))],
        out_specs=[pl.BlockSpec((None, kw, 2 * ns), lambda j: (j, 0, 0)),
                   pl.BlockSpec((None, kw, 2 * ns), lambda j: (j, 0, 0)),
                   pl.BlockSpec((None, 2 * LANES, kw), lambda j: (j, 0, 0)),
                   pl.BlockSpec((None, 1, 2 * ns), lambda j: (j, 0, 0))],
        out_shape=[jax.ShapeDtypeStruct((nt, kw, 2 * ns), BF16),
                   jax.ShapeDtypeStruct((nt, kw, 2 * ns), BF16),
                   jax.ShapeDtypeStruct((nt, 2 * LANES, kw), BF16),
                   jax.ShapeDtypeStruct((nt, 1, 2 * ns), F32)],
        compiler_params=_params(("parallel",)),
        name="s5_prep",
    )(a_rows, bt, cc, d_skip.reshape(1, g * S5_GROUP))


def _s5_core_body(ubf_ref, wb_ref, wct_ref, toe_ref, lam_ref, o_ref,
                  u_ref, ub_ref, slab_ref, hin_ref, hprev_ref, *, chunk, n_seq, n_chunks):
    ns = TILE_STATES
    nk = ns // LANES
    seq = n_chunks * chunk

    def token_rows(b, t):
        return pl.ds(b * seq + t, n_chunks, stride=chunk)

    def chunk_rows(b):
        return pl.ds(b, n_chunks, stride=n_seq)

    u_ref[...] = ubf_ref[...].astype(F32)
    for t in range(chunk):
        for b in range(n_seq):
            slab_ref[t, chunk_rows(b), :] = u_ref[token_rows(b, t), :]
        ub_ref[:, t * LANES:(t + 1) * LANES] = slab_ref[t].astype(BF16)

    h_in = jnp.dot(ub_ref[...], wb_ref[...], preferred_element_type=F32)
    for k in range(2 * nk):
        hin_ref[k] = h_in[:, k * LANES:(k + 1) * LANES]

    for a in range(chunk // 2):
        lo = 2 * a * LANES
        width = (chunk - 2 * a) * LANES
        part = jnp.dot(ub_ref[:, lo:lo + 2 * LANES], toe_ref[:, 0:width], preferred_element_type=F32)
        for i in range(chunk - 2 * a):
            if a == 0:
                slab_ref[i] = part[:, i * LANES:(i + 1) * LANES]
            else:
                slab_ref[2 * a + i] += part[:, i * LANES:(i + 1) * LANES]

    lam = [jnp.broadcast_to(lam_ref[:, k * LANES:(k + 1) * LANES], (n_seq, LANES)) for k in range(2 * nk)]
    state = [jnp.zeros((n_seq, LANES), F32) for _ in range(2 * nk)]
    for c in range(n_chunks):
        rows = pl.ds(c * n_seq, n_seq)
        new_state = []
        for k in range(nk):
            hr, hi = state[k], state[nk + k]
            hprev_ref[k, rows, :] = hr
            hprev_ref[nk + k, rows, :] = hi
            lr, li = lam[k], lam[nk + k]
            new_state.append((lr * hr - li * hi + hin_ref[k, rows, :],
                              lr * hi + li * hr + hin_ref[nk + k, rows, :]))
        state = [s[0] for s in new_state] + [s[1] for s in new_state]

    h_prev = jnp.concatenate([hprev_ref[k].astype(BF16) for k in range(2 * nk)], axis=1)
    for a in range(chunk // 2):
        lo = 2 * a * LANES
        y = lax.dot_general(h_prev, wct_ref[lo:lo + 2 * LANES, :], (((1,), (1,)), ((), ())),
                            preferred_element_type=F32)
        for i in range(2):
            t = 2 * a + i
            slab_ref[t] += y[:, i * LANES:(i + 1) * LANES]
            for b in range(n_seq):
                o_ref[token_rows(b, t), :] = slab_ref[t, chunk_rows(b), :]


def _s5_core(proj2d, wb, wct, toe, lam, *, chunk, n_seq):
    m, n2 = proj2d.shape
    e = n2 // 2
    r = m // chunk
    nt = e // LANES
    ns2 = 2 * TILE_STATES
    kw = chunk * LANES
    return pl.pallas_call(
        functools.partial(_s5_core_body, chunk=chunk, n_seq=n_seq, n_chunks=r // n_seq),
        grid=(nt,),
        in_specs=[pl.BlockSpec((m, LANES), lambda j: (0, j)),
                  pl.BlockSpec((None, kw, ns2), lambda j: (j, 0, 0)),
                  pl.BlockSpec((None, kw, ns2), lambda j: (j, 0, 0)),
                  pl.BlockSpec((None, 2 * LANES, kw), lambda j: (j, 0, 0)),
                  pl.BlockSpec((None, 1, ns2), lambda j: (j, 0, 0))],
        out_specs=pl.BlockSpec((m, LANES), lambda j: (0, j)),
        out_shape=jax.ShapeDtypeStruct((m, e), F32),
        scratch_shapes=[pltpu.VMEM((m, LANES), F32),
                        pltpu.VMEM((r, kw), BF16),
                        pltpu.VMEM((chunk, r, LANES), F32),
                        pltpu.VMEM((ns2 // LANES, r, LANES), F32),
                        pltpu.VMEM((ns2 // LANES, r, LANES), F32)],
        compiler_params=_params(("parallel",)),
        name="s5_core",
    )(proj2d, wb, wct, toe, lam)


def _glu_body(y_ref, w_ref, b_ref, o_ref, wbf_ref):
    @pl.when(pl.program_id(0) == 0)
    def _():
        wbf_ref[...] = w_ref[...].astype(BF16)

    g = jax.nn.gelu(y_ref[...])
    acc = jnp.dot(g.astype(BF16), wbf_ref[...], preferred_element_type=F32) + b_ref[...]
    o_ref[...] = (g * jax.nn.sigmoid(acc)).astype(BF16)


def _glu(y2d, w_all, layer, b, *, tm):
    m, e = y2d.shape
    return pl.pallas_call(
        _glu_body,
        grid=(m // tm,),
        in_specs=[pl.BlockSpec((tm, e), lambda i: (i, 0)),
                  pl.BlockSpec((None, e, e), lambda i: (layer, 0, 0), pipeline_mode=pl.Buffered(1)),
                  pl.BlockSpec((1, e), lambda i: (0, 0))],
        out_specs=pl.BlockSpec((tm, e), lambda i: (i, 0)),
        out_shape=jax.ShapeDtypeStruct((m, e), BF16),
        scratch_shapes=[pltpu.VMEM((e, e), BF16)],
        compiler_params=_params(("arbitrary",)),
        name="s5_glu",
    )(y2d, w_all, b.reshape(1, e))


def _fox_cum_body(f_ref, o_ref, *, blk):
    seq = f_ref.shape[0]
    row = lax.broadcasted_iota(jnp.int32, (blk, blk), 0)
    col = lax.broadcasted_iota(jnp.int32, (blk, blk), 1)
    tri = (col <= row).astype(BF16)
    carry = jnp.zeros((1, LANES), F32)
    for i in range(seq // blk):
        x = f_ref[i * blk:(i + 1) * blk, :]
        ls = jnp.minimum(x, 0.0) - jnp.log1p(jnp.exp(-jnp.abs(x)))
        cum = carry
        rest = ls
        for _ in range(3):
            part = rest.astype(BF16)
            cum = cum + jnp.dot(tri, part, preferred_element_type=F32)
            rest = rest - part.astype(F32)
        o_ref[i * blk:(i + 1) * blk, :] = cum
        carry = cum[blk - 1:blk, :]


def _fox_cum(f3d, *, blk):
    b, seq, _ = f3d.shape
    return pl.pallas_call(
        functools.partial(_fox_cum_body, blk=blk),
        grid=(b,),
        in_specs=[pl.BlockSpec((None, seq, LANES), lambda i: (i, 0, 0))],
        out_specs=pl.BlockSpec((None, seq, LANES), lambda i: (i, 0, 0)),
        out_shape=jax.ShapeDtypeStruct((b, seq, LANES), F32),
        compiler_params=_params(("parallel",)),
        name="fox_cum",
    )(f3d)


def _fox_attn_body(q_ref, k_ref, v_ref, ck_ref, qw_ref, kw_ref, o_ref,
                   kn_ref, m_ref, acc_ref, *, tq, scale):
    seq = k_ref.shape[0]
    nq = seq // tq
    log2e = 1.4426950408889634

    def q_block(i):
        rows = slice(i * tq, (i + 1) * tq)
        return (_rms_rows(q_ref[rows, :].astype(F32), qw_ref[...]) * (scale * log2e)).astype(BF16)

    def k_block(j):
        rows = slice(j * tq, (j + 1) * tq)
        kn_ref[rows, :] = _rms_rows(k_ref[rows, :].astype(F32), kw_ref[...]).astype(BF16)

    pairs = [(i, j) for i in range(nq) for j in range(i + 1)]
    qn = {}

    def logits(i, j):
        if j == 0:
            qn[i] = q_block(i)
        if j == i:
            k_block(j)
        return lax.dot_general(qn[i], kn_ref[j * tq:(j + 1) * tq, :],
                               (((1,), (1,)), ((), ())), preferred_element_type=F32)

    n_lt = tq // LANES
    ones_blk = jnp.ones((tq, LANES), BF16)
    s_next = logits(*pairs[0])
    for idx, (i, j) in enumerate(pairs):
        s = s_next
        if idx + 1 < len(pairs):
            s_next = logits(*pairs[idx + 1])
        c0 = ck_ref[i:i + 1, 0:1]
        dk = (ck_ref[j:j + 1, :] - c0) * log2e
        s_t = [s[:, t * LANES:(t + 1) * LANES] - dk[:, t * LANES:(t + 1) * LANES] for t in range(n_lt)]
        if i == j:
            r_i = lax.broadcasted_iota(jnp.int32, (tq, LANES), 0)
            c_i = lax.broadcasted_iota(jnp.int32, (tq, LANES), 1)
            s_t = [jnp.where(c_i + t * LANES <= r_i, s_t[t], NEG_BIG) for t in range(n_lt)]
        mx = functools.reduce(jnp.maximum, s_t)
        row_max = jnp.broadcast_to(jnp.max(mx, axis=-1, keepdims=True), (tq, LANES))
        v_blk = v_ref[j * tq:(j + 1) * tq, :]
        if j == 0:
            m_new = row_max
        else:
            m_old = m_ref[...]
            m_new = jnp.maximum(m_old, row_max)
        p = jnp.concatenate([jnp.exp2(s_t[t] - m_new).astype(BF16) for t in range(n_lt)], axis=1)
        pv = jnp.dot(p, jnp.concatenate([v_blk, ones_blk], axis=1), preferred_element_type=F32)
        if j == 0:
            acc_new = pv
        else:
            alpha = jnp.exp2(m_old - m_new)
            acc_new = jnp.concatenate([alpha, alpha], axis=1) * acc_ref[...] + pv
        if j == i:
            o_ref[i * tq:(i + 1) * tq, :] = (acc_new[:, 0:LANES] / acc_new[:, LANES:2 * LANES]).astype(BF16)
        else:
            m_ref[...] = m_new
            acc_ref[...] = acc_new


def _fox_attn(proj3d, cum_k, q_w, k_w, *, heads, tq):
    b, seq, _ = proj3d.shape
    dh = FOX_HEAD_DIM
    nq = seq // tq
    return pl.pallas_call(
        functools.partial(_fox_attn_body, tq=tq, scale=dh ** -0.5),
        grid=(b, heads),
        in_specs=[pl.BlockSpec((None, seq, dh), lambda bi, h: (bi, 0, h)),
                  pl.BlockSpec((None, seq, dh), lambda bi, h: (bi, 0, heads + h)),
                  pl.BlockSpec((None, seq, dh), lambda bi, h: (bi, 0, 2 * heads + h)),
                  pl.BlockSpec((None, None, nq, tq), lambda bi, h: (bi, h, 0, 0)),
                  pl.BlockSpec((1, dh), lambda bi, h: (0, 0)),
                  pl.BlockSpec((1, dh), lambda bi, h: (0, 0))],
        out_specs=pl.BlockSpec((None, seq, dh), lambda bi, h: (bi, 0, h)),
        out_shape=jax.ShapeDtypeStruct((b, seq, heads * dh), BF16),
        scratch_shapes=[pltpu.VMEM((seq, dh), BF16),
                        pltpu.VMEM((tq, LANES), F32),
                        pltpu.VMEM((tq, dh + LANES), F32)],
        compiler_params=_params(("parallel", "parallel")),
        name="fox_attn",
    )(proj3d, proj3d, proj3d, cum_k, q_w.reshape(1, dh), k_w.reshape(1, dh))


def _pool_body(u_ref, w_ref, s_ref, o_ref, sa_ref, sb_ref, *, windows):
    seq, cg = u_ref.shape
    halo = POOL_HALO
    grp = pl.program_id(1)
    zeros = jnp.zeros((halo, cg), F32)
    sa_ref[0:halo, :] = zeros
    sb_ref[0:halo, :] = zeros
    t_idx = lax.broadcasted_iota(jnp.int32, (seq, 1), 0)

    for gi, win in enumerate(windows):
        @pl.when(grp == gi)
        def _(win=win):
            u = u_ref[...].astype(F32)
            sa_ref[halo:halo + seq, :] = u
            bufs = (sa_ref, sb_ref)
            s = u
            shift = 1
            level = 0
            while shift < win:
                src = bufs[level % 2]
                s = src[halo:halo + seq, :] + src[halo - shift:halo - shift + seq, :]
                shift *= 2
                level += 1
                if shift < win:
                    bufs[level % 2][halo:halo + seq, :] = s
            cnt = jnp.minimum(t_idx + 1, win).astype(F32)
            diff = (s / cnt - u).astype(BF16)
            mixed = jnp.dot(diff, w_ref[...].astype(BF16), preferred_element_type=F32)
            o_ref[...] = (mixed * s_ref[...]).astype(BF16)


def _pool(proj3d, w_group_all, layer, scale, *, windows):
    b, seq, n2 = proj3d.shape
    e = n2 // 2
    ng = len(windows)
    cg = e // ng
    return pl.pallas_call(
        functools.partial(_pool_body, windows=windows),
        grid=(b, ng),
        in_specs=[pl.BlockSpec((None, seq, cg), lambda bi, g: (bi, 0, g)),
                  pl.BlockSpec((None, None, cg, cg), lambda bi, g: (layer, g, 0, 0)),
                  pl.BlockSpec((1, cg), lambda bi, g: (0, g))],
        out_specs=pl.BlockSpec((None, seq, cg), lambda bi, g: (bi, 0, g)),
        out_shape=jax.ShapeDtypeStruct((b, seq, e), BF16),
        scratch_shapes=[pltpu.VMEM((POOL_HALO + seq, cg), F32),
                        pltpu.VMEM((POOL_HALO + seq, cg), F32)],
        compiler_params=_params(("parallel", "parallel")),
        name="pool_mix",
    )(proj3d, w_group_all, scale.reshape(1, e))


def _pick(n, pref):
    t = min(n, pref)
    while n % t:
        t //= 2
    return t


def kernel(x, norm_w, out_proj, s5_in_proj, s5_a_re, s5_a_im, s5_log_dt, s5_b_re, s5_b_im, s5_c_re, s5_c_im, s5_d, s5_w_glu, s5_b_glu, fox_in_proj, fox_q_norm, fox_k_norm, fox_f_bias, pool_in_proj, pool_w_group, pool_scale):
    bsz, seq, d = x.shape
    depth = norm_w.shape[0]
    e = s5_d.shape[1]
    heads = e // FOX_HEAD_DIM
    m = bsz * seq
    n_mixers = 3

    assert e % (LANES * len(POOL_WINDOWS)) == 0 and e % FOX_HEAD_DIM == 0 and heads <= LANES
    assert seq % CHUNK == 0 and s5_a_re.shape[1:] == (e // S5_GROUP, S5_STATE)
    assert fox_in_proj.shape[2] == 4 * e + heads and out_proj.shape[1:] == (e, d)

    tm = _pick(m, 1024)
    tn_in = _pick(2 * e, 1024)
    tm_out = _pick(m, 512)
    tm_glu = _pick(m, 512)
    tq = _pick(seq, 512)

    h = x.reshape(m, d)
    for i in range(depth):
        kind, j = i % n_mixers, i // n_mixers
        if kind == 0:
            wb, wct, toe, lam = _s5_prep(s5_a_re[j], s5_a_im[j], s5_log_dt[j], s5_b_re[j], s5_b_im[j],
                                         s5_c_re[j], s5_c_im[j], s5_d[j], chunk=CHUNK)
            proj, w_out = _norm_proj(h, norm_w[i], s5_in_proj, j, out_proj, i, n_out=2 * e, tm=tm, tn=tn_in)
            g = _s5_core(proj, wb, wct, toe, lam, chunk=CHUNK, n_seq=bsz)
            y = _glu(g, s5_w_glu, j, s5_b_glu[j], tm=tm_glu)
            h = _out_proj(y, proj, 1, h, w_out, tm=tm_out)
        elif kind == 1:
            n_main = 4 * e
            b_f = jnp.pad(fox_f_bias[j], (0, LANES - heads)).reshape(1, LANES)
            proj, w_out, f_logit = _norm_proj(h, norm_w[i], jnp.swapaxes(fox_in_proj, 1, 2), j, out_proj, i,
                                              n_out=n_main, tm=tm, tn=tn_in, w_transposed=True, extra_bias=b_f)
            cum = _fox_cum(f_logit.reshape(bsz, seq, LANES), blk=_pick(seq, 256))
            cum_k = cum[:, :, :heads].transpose(0, 2, 1).reshape(bsz, heads, seq // tq, tq)
            y = _fox_attn(proj.reshape(bsz, seq, n_main), cum_k, fox_q_norm[j], fox_k_norm[j],
                          heads=heads, tq=tq)
            h = _out_proj(y.reshape(m, e), proj, 3, h, w_out, tm=tm_out)
        else:
            proj, w_out = _norm_proj(h, norm_w[i], pool_in_proj, j, out_proj, i, n_out=2 * e, tm=tm, tn=tn_in)
            y = _pool(proj.reshape(bsz, seq, 2 * e), pool_w_group, j, pool_scale[j], windows=POOL_WINDOWS)
            h = _out_proj(y.reshape(m, e), proj, 1, h, w_out, tm=tm_out)
    return h.reshape(bsz, seq, d)
```

```python
import functools

import jax
import jax.numpy as jnp
from jax import lax
from jax.experimental import pallas as pl
from jax.experimental.pallas import tpu as pltpu

F32 = jnp.float32
BF16 = jnp.bfloat16

LANES = 128
S5_GROUP = 16
S5_STATE = 64
TILE_GROUPS = LANES // S5_GROUP
TILE_STATES = TILE_GROUPS * S5_STATE
CHUNK = 8
FOX_HEAD_DIM = 128
POOL_WINDOWS = (2, 4, 8, 16)
POOL_HALO = 16
EPS = 1e-6
NEG_BIG = -1e30
VMEM_LIMIT = 56 * 1024 * 1024


def _params(semantics):
    return pltpu.CompilerParams(dimension_semantics=semantics, vmem_limit_bytes=VMEM_LIMIT)


def _rms_rows(x, w):
    ms = jnp.mean(x * x, axis=-1, keepdims=True)
    return x * lax.rsqrt(ms + EPS) * w


def _silu(z):
    return z * jax.nn.sigmoid(z)


def _norm_proj_body(x_ref, nw_ref, w_ref, wo_ref, *rest, w_transposed, n_extra):
    if n_extra:
        wf_ref, bf_ref, o_ref, wo_bf_ref, of_ref, xn_ref = rest
    else:
        o_ref, wo_bf_ref, xn_ref = rest
    j = pl.program_id(1)
    w_contract = 1 if w_transposed else 0

    def matmul(xn, w):
        return lax.dot_general(xn, w, (((1,), (w_contract,)), ((), ())), preferred_element_type=F32)

    def emit(xn):
        o_ref[...] = matmul(xn, w_ref[...].astype(BF16)).astype(BF16)

    @pl.when(j == 0)
    def _():
        xn = _rms_rows(x_ref[...], nw_ref[...]).astype(BF16)
        xn_ref[...] = xn
        emit(xn)
        wo_bf_ref[...] = wo_ref[...].astype(BF16)
        if n_extra:
            idx = lax.broadcasted_iota(jnp.int32, wf_ref.shape, 0 if w_transposed else 1)
            wf = jnp.where(idx < n_extra, wf_ref[...], 0.0).astype(BF16)
            of_ref[...] = matmul(xn, wf) + bf_ref[...]

    @pl.when(j > 0)
    def _():
        emit(xn_ref[...])


def _norm_proj(h2d, norm_w, w_all, layer, w_out_all, layer_out, *, n_out, tm, tn, w_transposed=False,
               extra_bias=None):
    m, d = h2d.shape
    nj = n_out // tn
    e_out, d_out = w_out_all.shape[1:]
    wo_rows = e_out // (m // tm)
    if w_transposed:
        w_spec = pl.BlockSpec((None, tn, d), lambda i, j: (layer, j, 0))
        wf_spec = pl.BlockSpec((None, LANES, d), lambda i, j: (layer, n_out // LANES, 0))
        n_extra = w_all.shape[1] - n_out
    else:
        w_spec = pl.BlockSpec((None, d, tn), lambda i, j: (layer, 0, j))
        wf_spec = pl.BlockSpec((None, d, LANES), lambda i, j: (layer, 0, n_out // LANES))
        n_extra = w_all.shape[2] - n_out
    in_specs = [pl.BlockSpec((tm, d), lambda i, j: (i, 0)),
                pl.BlockSpec((1, d), lambda i, j: (0, 0)),
                w_spec,
                pl.BlockSpec((None, wo_rows, d_out), lambda i, j: (layer_out, i, 0))]
    out_specs = [pl.BlockSpec((tm, tn), lambda i, j: (i, j)),
                 pl.BlockSpec((wo_rows, d_out), lambda i, j: (i, 0))]
    out_shape = [jax.ShapeDtypeStruct((m, n_out), BF16),
                 jax.ShapeDtypeStruct((e_out, d_out), BF16)]
    args = [h2d, norm_w.reshape(1, d), w_all, w_out_all]
    if extra_bias is None:
        n_extra = 0
    else:
        in_specs += [wf_spec, pl.BlockSpec((1, LANES), lambda i, j: (0, 0))]
        out_specs += [pl.BlockSpec((tm, LANES), lambda i, j: (i, 0))]
        out_shape += [jax.ShapeDtypeStruct((m, LANES), F32)]
        args += [w_all, extra_bias]
    outs = pl.pallas_call(
        functools.partial(_norm_proj_body, w_transposed=w_transposed, n_extra=n_extra),
        grid=(m // tm, n_out // tn),
        in_specs=in_specs,
        out_specs=out_specs,
        out_shape=out_shape,
        scratch_shapes=[pltpu.VMEM((tm, d), BF16)],
        compiler_params=_params(("parallel", "arbitrary")),
        name="norm_proj",
    )(*args)
    return outs


def _out_proj_body(y_ref, z_ref, h_ref, w_ref, o_ref):
    a = (y_ref[...].astype(F32) * _silu(z_ref[...].astype(F32))).astype(BF16)
    o_ref[...] = h_ref[...] + jnp.dot(a, w_ref[...], preferred_element_type=F32)


def _out_proj(y2d, z2d, z_block, h2d, w_bf16, *, tm):
    m, e = y2d.shape
    d = w_bf16.shape[1]
    return pl.pallas_call(
        _out_proj_body,
        grid=(m // tm,),
        in_specs=[pl.BlockSpec((tm, e), lambda i: (i, 0)),
                  pl.BlockSpec((tm, e), lambda i: (i, z_block)),
                  pl.BlockSpec((tm, d), lambda i: (i, 0)),
                  pl.BlockSpec((e, d), lambda i: (0, 0), pipeline_mode=pl.Buffered(1))],
        out_specs=pl.BlockSpec((tm, d), lambda i: (i, 0)),
        out_shape=jax.ShapeDtypeStruct((m, d), F32),
        compiler_params=_params(("parallel",)),
        name="out_proj",
    )(y2d, z2d, h2d, w_bf16)


def _s5_prep_body(a_ref, bt_ref, c_ref, d_ref, wb_ref, wct_ref, toe_ref, lam_ref, *, chunk):
    ns = TILE_STATES
    n_lt = ns // LANES

    def block_diag(x):
        x2 = jnp.concatenate([x, x], axis=1)
        grp_row = lax.broadcasted_iota(jnp.int32, (LANES, LANES), 0) // S5_GROUP
        grp_lane = lax.broadcasted_iota(jnp.int32, (LANES, LANES), 1) // S5_STATE
        per_tile = LANES // S5_STATE
        return jnp.concatenate([jnp.where(grp_row == grp_lane + per_tile * t, x2, 0.0)
                                for t in range(n_lt)], axis=1)

    ar, ai, ldt = a_ref[0:1, :], a_ref[1:2, :], a_ref[2:3, :]
    dt = jnp.exp(ldt)
    k_rows = lax.broadcasted_iota(jnp.int32, (chunk + 1, 1), 0).astype(F32)
    mag = jnp.exp(ar * dt * k_rows)
    ang = ai * dt * k_rows
    pr, pi = mag * jnp.cos(ang), mag * jnp.sin(ang)
    abar_r, abar_i = pr[1:2, :], pi[1:2, :]
    den = ar * ar + ai * ai
    xr = abar_r - 1.0
    fr = (xr * ar + abar_i * ai) / den
    fi = (abar_i * ar - xr * ai) / den
    bt_re, bt_im = block_diag(bt_ref[0]), block_diag(bt_ref[1])
    bbar_re = fr * bt_re - fi * bt_im
    bbar_im = fr * bt_im + fi * bt_re
    c_re, c_im = block_diag(c_ref[0]), block_diag(c_ref[1])
    lam_ref[:, 0:ns] = pr[chunk:chunk + 1, :]
    lam_ref[:, ns:2 * ns] = pi[chunk:chunk + 1, :]

    bb_re, bb_im = bbar_re.astype(BF16), bbar_im.astype(BF16)
    cb_re, cb_im = c_re.astype(BF16), c_im.astype(BF16)
    for tau in range(chunk):
        qr, qi = pr[tau:tau + 1, :].astype(BF16), pi[tau:tau + 1, :].astype(BF16)
        t = chunk - 1 - tau
        wb_ref[t * LANES:(t + 1) * LANES, 0:ns] = bb_re * qr - bb_im * qi
        wb_ref[t * LANES:(t + 1) * LANES, ns:2 * ns] = bb_re * qi + bb_im * qr
        qr, qi = pr[tau + 1:tau + 2, :].astype(BF16), pi[tau + 1:tau + 2, :].astype(BF16)
        wct_ref[tau * LANES:(tau + 1) * LANES, 0:ns] = cb_re * qr - cb_im * qi
        wct_ref[tau * LANES:(tau + 1) * LANES, ns:2 * ns] = -(cb_re * qi + cb_im * qr)

    c_cat = jnp.concatenate([cb_re, -cb_im], axis=1)
    k_all = lax.dot_general(wb_ref[...], c_cat, (((1,), (1,)), ((), ())),
                            preferred_element_type=F32)
    zero_blk = jnp.zeros((LANES, LANES), BF16)
    eye = lax.broadcasted_iota(jnp.int32, (LANES, LANES), 0) == lax.broadcasted_iota(jnp.int32, (LANES, LANES), 1)
    for tau in range(chunk):
        t = chunk - 1 - tau
        k_tau = k_all[t * LANES:(t + 1) * LANES, :]
        if tau == 0:
            k_tau = k_tau + jnp.where(eye, d_ref[...], 0.0)
        k_tau = k_tau.astype(BF16)
        toe_ref[0:LANES, tau * LANES:(tau + 1) * LANES] = k_tau
        if tau + 1 < chunk:
            toe_ref[LANES:2 * LANES, (tau + 1) * LANES:(tau + 2) * LANES] = k_tau
    toe_ref[LANES:2 * LANES, 0:LANES] = zero_blk


def _s5_prep(a_re, a_im, log_dt, b_re, b_im, c_re, c_im, d_skip, *, chunk):
    g, p = a_re.shape
    nt = g // TILE_GROUPS
    ns = TILE_STATES
    ldt = jnp.broadcast_to(log_dt[:, None], (g, p))
    a_rows = jnp.stack([a_re.reshape(nt, ns), a_im.reshape(nt, ns), ldt.reshape(nt, ns)], axis=1)
    bt = jnp.stack([b_re, b_im], axis=0).transpose(1, 0, 3, 2).reshape(nt, TILE_GROUPS, 2, S5_GROUP, p)
    bt = bt.transpose(0, 2, 1, 3, 4).reshape(nt, 2, LANES, p)
    cc = jnp.stack([c_re.reshape(nt, LANES, p), c_im.reshape(nt, LANES, p)], axis=1)
    kw = chunk * LANES
    return pl.pallas_call(
        functools.partial(_s5_prep_body, chunk=chunk),
        grid=(nt,),
        in_specs=[pl.BlockSpec((None, 3, ns), lambda j: (j, 0, 0)),
                  pl.BlockSpec((None, 2, LANES, p), lambda j: (j, 0, 0, 0)),
                  pl.BlockSpec((None, 2, LANES, p), lambda j: (j, 0, 0, 0)),
                  pl.BlockSpec((1, LANES), lambda j: (0, j))],
        out_specs=[pl.BlockSpec((None, kw, 2 * ns), lambda j: (j, 0, 0)),
                   pl.BlockSpec((None, kw, 2 * ns), lambda j: (j, 0, 0)),
                   pl.BlockSpec((None, 2 * LANES, kw), lambda j: (j, 0, 0)),
                   pl.BlockSpec((None, 1, 2 * ns), lambda j: (j, 0, 0))],
        out_shape=[jax.ShapeDtypeStruct((nt, kw, 2 * ns), BF16),
                   jax.ShapeDtypeStruct((nt, kw, 2 * ns), BF16),
                   jax.ShapeDtypeStruct((nt, 2 * LANES, kw), BF16),
                   jax.ShapeDtypeStruct((nt, 1, 2 * ns), F32)],
        compiler_params=_params(("parallel",)),
        name="s5_prep",
    )(a_rows, bt, cc, d_skip.reshape(1, g * S5_GROUP))


def _s5_core_body(ubf_ref, wb_ref, wct_ref, toe_ref, lam_ref, o_ref,
                  u_ref, ub_ref, slab_ref, hin_ref, hprev_ref, *, chunk, n_seq, n_chunks):
    ns = TILE_STATES
    nk = ns // LANES
    seq = n_chunks * chunk

    def token_rows(b, t):
        return pl.ds(b * seq + t, n_chunks, stride=chunk)

    def chunk_rows(b):
        return pl.ds(b, n_chunks, stride=n_seq)

    u_ref[...] = ubf_ref[...].astype(F32)
    for t in range(chunk):
        for b in range(n_seq):
            slab_ref[t, chunk_rows(b), :] = u_ref[token_rows(b, t), :]
        ub_ref[:, t * LANES:(t + 1) * LANES] = slab_ref[t].astype(BF16)

    h_in = jnp.dot(ub_ref[...], wb_ref[...], preferred_element_type=F32)
    for k in range(2 * nk):
        hin_ref[k] = h_in[:, k * LANES:(k + 1) * LANES]

    for a in range(chunk // 2):
        lo = 2 * a * LANES
        width = (chunk - 2 * a) * LANES
        part = jnp.dot(ub_ref[:, lo:lo + 2 * LANES], toe_ref[:, 0:width], preferred_element_type=F32)
        for i in range(chunk - 2 * a):
            if a == 0:
                slab_ref[i] = part[:, i * LANES:(i + 1) * LANES]
            else:
                slab_ref[2 * a + i] += part[:, i * LANES:(i + 1) * LANES]

    lam = [jnp.broadcast_to(lam_ref[:, k * LANES:(k + 1) * LANES], (n_seq, LANES)) for k in range(2 * nk)]
    state = [jnp.zeros((n_seq, LANES), F32) for _ in range(2 * nk)]
    for c in range(n_chunks):
        rows = pl.ds(c * n_seq, n_seq)
        new_state = []
        for k in range(nk):
            hr, hi = state[k], state[nk + k]
            hprev_ref[k, rows, :] = hr
            hprev_ref[nk + k, rows, :] = hi
            lr, li = lam[k], lam[nk + k]
            new_state.append((lr * hr - li * hi + hin_ref[k, rows, :],
                              lr * hi + li * hr + hin_ref[nk + k, rows, :]))
        state = [s[0] for s in new_state] + [s[1] for s in new_state]

    h_prev = jnp.concatenate([hprev_ref[k].astype(BF16) for k in range(2 * nk)], axis=1)
    for a in range(chunk // 2):
        lo = 2 * a * LANES
        y = lax.dot_general(h_prev, wct_ref[lo:lo + 2 * LANES, :], (((1,), (1,)), ((), ())),
                            preferred_element_type=F32)
        for i in range(2):
            t = 2 * a + i
            slab_ref[t] += y[:, i * LANES:(i + 1) * LANES]
            for b in range(n_seq):
                o_ref[token_rows(b, t), :] = slab_ref[t, chunk_rows(b), :]


def _s5_core(proj2d, wb, wct, toe, lam, *, chunk, n_seq):
    m, n2 = proj2d.shape
    e = n2 // 2
    r = m // chunk
    nt = e // LANES
    ns2 = 2 * TILE_STATES
    kw = chunk * LANES
    return pl.pallas_call(
        functools.partial(_s5_core_body, chunk=chunk, n_seq=n_seq, n_chunks=r // n_seq),
        grid=(nt,),
        in_specs=[pl.BlockSpec((m, LANES), lambda j: (0, j)),
                  pl.BlockSpec((None, kw, ns2), lambda j: (j, 0, 0)),
                  pl.BlockSpec((None, kw, ns2), lambda j: (j, 0, 0)),
                  pl.BlockSpec((None, 2 * LANES, kw), lambda j: (j, 0, 0)),
                  pl.BlockSpec((None, 1, ns2), lambda j: (j, 0, 0))],
        out_specs=pl.BlockSpec((m, LANES), lambda j: (0, j)),
        out_shape=jax.ShapeDtypeStruct((m, e), F32),
        scratch_shapes=[pltpu.VMEM((m, LANES), F32),
                        pltpu.VMEM((r, kw), BF16),
                        pltpu.VMEM((chunk, r, LANES), F32),
                        pltpu.VMEM((ns2 // LANES, r, LANES), F32),
                        pltpu.VMEM((ns2 // LANES, r, LANES), F32)],
        compiler_params=_params(("parallel",)),
        name="s5_core",
    )(proj2d, wb, wct, toe, lam)


def _glu_body(y_ref, w_ref, b_ref, o_ref, wbf_ref):
    @pl.when(pl.program_id(0) == 0)
    def _():
        wbf_ref[...] = w_ref[...].astype(BF16)

    g = jax.nn.gelu(y_ref[...])
    acc = jnp.dot(g.astype(BF16), wbf_ref[...], preferred_element_type=F32) + b_ref[...]
    o_ref[...] = (g * jax.nn.sigmoid(acc)).astype(BF16)


def _glu(y2d, w_all, layer, b, *, tm):
    m, e = y2d.shape
    return pl.pallas_call(
        _glu_body,
        grid=(m // tm,),
        in_specs=[pl.BlockSpec((tm, e), lambda i: (i, 0)),
                  pl.BlockSpec((None, e, e), lambda i: (layer, 0, 0), pipeline_mode=pl.Buffered(1)),
                  pl.BlockSpec((1, e), lambda i: (0, 0))],
        out_specs=pl.BlockSpec((tm, e), lambda i: (i, 0)),
        out_shape=jax.ShapeDtypeStruct((m, e), BF16),
        scratch_shapes=[pltpu.VMEM((e, e), BF16)],
        compiler_params=_params(("arbitrary",)),
        name="s5_glu",
    )(y2d, w_all, b.reshape(1, e))


def _fox_cum_body(f_ref, o_ref, *, blk):
    seq = f_ref.shape[0]
    row = lax.broadcasted_iota(jnp.int32, (blk, blk), 0)
    col = lax.broadcasted_iota(jnp.int32, (blk, blk), 1)
    tri = (col <= row).astype(BF16)
    carry = jnp.zeros((1, LANES), F32)
    for i in range(seq // blk):
        x = f_ref[i * blk:(i + 1) * blk, :]
        ls = jnp.minimum(x, 0.0) - jnp.log1p(jnp.exp(-jnp.abs(x)))
        cum = carry
        rest = ls
        for _ in range(3):
            part = rest.astype(BF16)
            cum = cum + jnp.dot(tri, part, preferred_element_type=F32)
            rest = rest - part.astype(F32)
        o_ref[i * blk:(i + 1) * blk, :] = cum
        carry = cum[blk - 1:blk, :]


def _fox_cum(f3d, *, blk):
    b, seq, _ = f3d.shape
    return pl.pallas_call(
        functools.partial(_fox_cum_body, blk=blk),
        grid=(b,),
        in_specs=[pl.BlockSpec((None, seq, LANES), lambda i: (i, 0, 0))],
        out_specs=pl.BlockSpec((None, seq, LANES), lambda i: (i, 0, 0)),
        out_shape=jax.ShapeDtypeStruct((b, seq, LANES), F32),
        compiler_params=_params(("parallel",)),
        name="fox_cum",
    )(f3d)


def _fox_attn_body(q_ref, k_ref, v_ref, ck_ref, qw_ref, kw_ref, o_ref,
                   kn_ref, m_ref, acc_ref, *, tq, scale):
    seq = k_ref.shape[0]
    nq = seq // tq
    log2e = 1.4426950408889634

    def q_block(i):
        rows = slice(i * tq, (i + 1) * tq)
        return (_rms_rows(q_ref[rows, :].astype(F32), qw_ref[...]) * (scale * log2e)).astype(BF16)

    def k_block(j):
        rows = slice(j * tq, (j + 1) * tq)
        kn_ref[rows, :] = _rms_rows(k_ref[rows, :].astype(F32), kw_ref[...]).astype(BF16)

    pairs = [(i, j) for i in range(nq) for j in range(i + 1)]
    qn = {}

    def logits(i, j):
        if j == 0:
            qn[i] = q_block(i)
        if j == i:
            k_block(j)
        return lax.dot_general(qn[i], kn_ref[j * tq:(j + 1) * tq, :],
                               (((1,), (1,)), ((), ())), preferred_element_type=F32)

    n_lt = tq // LANES
    ones_blk = jnp.ones((tq, LANES), BF16)
    s_next = logits(*pairs[0])
    for idx, (i, j) in enumerate(pairs):
        s = s_next
        if idx + 1 < len(pairs):
            s_next = logits(*pairs[idx + 1])
        c0 = ck_ref[i:i + 1, 0:1]
        dk = (ck_ref[j:j + 1, :] - c0) * log2e
        s_t = [s[:, t * LANES:(t + 1) * LANES] - dk[:, t * LANES:(t + 1) * LANES] for t in range(n_lt)]
        if i == j:
            r_i = lax.broadcasted_iota(jnp.int32, (tq, LANES), 0)
            c_i = lax.broadcasted_iota(jnp.int32, (tq, LANES), 1)
            s_t = [jnp.where(c_i + t * LANES <= r_i, s_t[t], NEG_BIG) for t in range(n_lt)]
        mx = functools.reduce(jnp.maximum, s_t)
        row_max = jnp.broadcast_to(jnp.max(mx, axis=-1, keepdims=True), (tq, LANES))
        v_blk = v_ref[j * tq:(j + 1) * tq, :]
        if j == 0:
            m_new = row_max
        else:
            m_old = m_ref[...]
            m_new = jnp.maximum(m_old, row_max)
        p = jnp.concatenate([jnp.exp2(s_t[t] - m_new).astype(BF16) for t in range(n_lt)], axis=1)
        pv = jnp.dot(p, jnp.concatenate([v_blk, ones_blk], axis=1), preferred_element_type=F32)
        if j == 0:
            acc_new = pv
        else:
            alpha = jnp.exp2(m_old - m_new)
            acc_new = jnp.concatenate([alpha, alpha], axis=1) * acc_ref[...] + pv
        if j == i:
            o_ref[i * tq:(i + 1) * tq, :] = (acc_new[:, 0:LANES] / acc_new[:, LANES:2 * LANES]).astype(BF16)
        else:
            m_ref[...] = m_new
            acc_ref[...] = acc_new


def _fox_attn(proj3d, cum_k, q_w, k_w, *, heads, tq):
    b, seq, _ = proj3d.shape
    dh = FOX_HEAD_DIM
    nq = seq // tq
    return pl.pallas_call(
        functools.partial(_fox_attn_body, tq=tq, scale=dh ** -0.5),
        grid=(b, heads),
        in_specs=[pl.BlockSpec((None, seq, dh), lambda bi, h: (bi, 0, h)),
                  pl.BlockSpec((None, seq, dh), lambda bi, h: (bi, 0, heads + h)),
                  pl.BlockSpec((None, seq, dh), lambda bi, h: (bi, 0, 2 * heads + h)),
                  pl.BlockSpec((None, None, nq, tq), lambda bi, h: (bi, h, 0, 0)),
                  pl.BlockSpec((1, dh), lambda bi, h: (0, 0)),
                  pl.BlockSpec((1, dh), lambda bi, h: (0, 0))],
        out_specs=pl.BlockSpec((None, seq, dh), lambda bi, h: (bi, 0, h)),
        out_shape=jax.ShapeDtypeStruct((b, seq, heads * dh), BF16),
        scratch_shapes=[pltpu.VMEM((seq, dh), BF16),
                        pltpu.VMEM((tq, LANES), F32),
                        pltpu.VMEM((tq, dh + LANES), F32)],
        compiler_params=_params(("parallel", "parallel")),
        name="fox_attn",
    )(proj3d, proj3d, proj3d, cum_k, q_w.reshape(1, dh), k_w.reshape(1, dh))


def _pool_body(u_ref, w_ref, s_ref, o_ref, sa_ref, sb_ref, *, windows):
    seq, cg = u_ref.shape
    halo = POOL_HALO
    grp = pl.program_id(1)
    zeros = jnp.zeros((halo, cg), F32)
    sa_ref[0:halo, :] = zeros
    sb_ref[0:halo, :] = zeros
    t_idx = lax.broadcasted_iota(jnp.int32, (seq, 1), 0)

    for gi, win in enumerate(windows):
        @pl.when(grp == gi)
        def _(win=win):
            u = u_ref[...].astype(F32)
            sa_ref[halo:halo + seq, :] = u
            bufs = (sa_ref, sb_ref)
            s = u
            shift = 1
            level = 0
            while shift < win:
                src = bufs[level % 2]
                s = src[halo:halo + seq, :] + src[halo - shift:halo - shift + seq, :]
                shift *= 2
                level += 1
                if shift < win:
                    bufs[level % 2][halo:halo + seq, :] = s
            cnt = jnp.minimum(t_idx + 1, win).astype(F32)
            diff = (s / cnt - u).astype(BF16)
            mixed = jnp.dot(diff, w_ref[...].astype(BF16), preferred_element_type=F32)
            o_ref[...] = (mixed * s_ref[...]).astype(BF16)


def _pool(proj3d, w_group_all, layer, scale, *, windows):
    b, seq, n2 = proj3d.shape
    e = n2 // 2
    ng = len(windows)
    cg = e // ng
    return pl.pallas_call(
        functools.partial(_pool_body, windows=windows),
        grid=(b, ng),
        in_specs=[pl.BlockSpec((None, seq, cg), lambda bi, g: (bi, 0, g)),
                  pl.BlockSpec((None, None, cg, cg), lambda bi, g: (layer, g, 0, 0)),
                  pl.BlockSpec((1, cg), lambda bi, g: (0, g))],
        out_specs=pl.BlockSpec((None, seq, cg), lambda bi, g: (bi, 0, g)),
        out_shape=jax.ShapeDtypeStruct((b, seq, e), BF16),
        scratch_shapes=[pltpu.VMEM((POOL_HALO + seq, cg), F32),
                        pltpu.VMEM((POOL_HALO + seq, cg), F32)],
        compiler_params=_params(("parallel", "parallel")),
        name="pool_mix",
    )(proj3d, w_group_all, scale.reshape(1, e))


def _pick(n, pref):
    t = min(n, pref)
    while n % t:
        t //= 2
    return t


def kernel(x, norm_w, out_proj, s5_in_proj, s5_a_re, s5_a_im, s5_log_dt, s5_b_re, s5_b_im, s5_c_re, s5_c_im, s5_d, s5_w_glu, s5_b_glu, fox_in_proj, fox_q_norm, fox_k_norm, fox_f_bias, pool_in_proj, pool_w_group, pool_scale):
    bsz, seq, d = x.shape
    depth = norm_w.shape[0]
    e = s5_d.shape[1]
    heads = e // FOX_HEAD_DIM
    m = bsz * seq
    n_mixers = 3

    assert e % (LANES * len(POOL_WINDOWS)) == 0 and e % FOX_HEAD_DIM == 0 and heads <= LANES
    assert seq % CHUNK == 0 and s5_a_re.shape[1:] == (e // S5_GROUP, S5_STATE)
    assert fox_in_proj.shape[2] == 4 * e + heads and out_proj.shape[1:] == (e, d)

    tm = _pick(m, 1024)
    tn_in = _pick(2 * e, 1024)
    tm_out = _pick(m, 512)
    tm_glu = _pick(m, 512)
    tq = _pick(seq, 512)

    h = x.reshape(m, d)
    for i in range(depth):
        kind, j = i % n_mixers, i // n_mixers
        if kind == 0:
            wb, wct, toe, lam = _s5_prep(s5_a_re[j], s5_a_im[j], s5_log_dt[j], s5_b_re[j], s5_b_im[j],
                                         s5_c_re[j], s5_c_im[j], s5_d[j], chunk=CHUNK)
            proj, w_out = _norm_proj(h, norm_w[i], s5_in_proj, j, out_proj, i, n_out=2 * e, tm=tm, tn=tn_in)
            g = _s5_core(proj, wb, wct, toe, lam, chunk=CHUNK, n_seq=bsz)
            y = _glu(g, s5_w_glu, j, s5_b_glu[j], tm=tm_glu)
            h = _out_proj(y, proj, 1, h, w_out, tm=tm_out)
        elif kind == 1:
            n_main = 4 * e
            b_f = jnp.pad(fox_f_bias[j], (0, LANES - heads)).reshape(1, LANES)
            proj, w_out, f_logit = _norm_proj(h, norm_w[i], jnp.swapaxes(fox_in_proj, 1, 2), j, out_proj, i,
                                              n_out=n_main, tm=tm, tn=tn_in, w_transposed=True, extra_bias=b_f)
            cum = _fox_cum(f_logit.reshape(bsz, seq, LANES), blk=_pick(seq, 256))
            cum_k = cum[:, :, :heads].transpose(0, 2, 1).reshape(bsz, heads, seq // tq, tq)
            y = _fox_attn(proj.reshape(bsz, seq, n_main), cum_k, fox_q_norm[j], fox_k_norm[j],
                          heads=heads, tq=tq)
            h = _out_proj(y.reshape(m, e), proj, 3, h, w_out, tm=tm_out)
        else:
            proj, w_out = _norm_proj(h, norm_w[i], pool_in_proj, j, out_proj, i, n_out=2 * e, tm=tm, tn=tn_in)
            y = _pool(proj.reshape(bsz, seq, 2 * e), pool_w_group, j, pool_scale[j], windows=POOL_WINDOWS)
            h = _out_proj(y.reshape(m, e), proj, 1, h, w_out, tm=tm_out)
    return h.reshape(bsz, seq, d)
```

```python
import functools

import jax
import jax.numpy as jnp
from jax import lax
from jax.experimental import pallas as pl
from jax.experimental.pallas import tpu as pltpu

F32 = jnp.float32
BF16 = jnp.bfloat16

LANES = 128
S5_GROUP = 16
S5_STATE = 64
TILE_GROUPS = LANES // S5_GROUP
TILE_STATES = TILE_GROUPS * S5_STATE
CHUNK = 8
FOX_HEAD_DIM = 128
POOL_WINDOWS = (2, 4, 8, 16)
POOL_HALO = 16
EPS = 1e-6
NEG_BIG = -1e30
VMEM_LIMIT = 56 * 1024 * 1024
N_X_PIECES = 4


def _params(semantics):
    return pltpu.CompilerParams(dimension_semantics=semantics, vmem_limit_bytes=VMEM_LIMIT)


def _rms_rows(x, w):
    ms = jnp.mean(x * x, axis=-1, keepdims=True)
    return x * lax.rsqrt(ms + EPS) * w


def _silu(z):
    return z * jax.nn.sigmoid(z)


def _norm_proj_body(*refs, w_transposed, n_extra, n_x):
    x_refs, (nw_ref, w_ref, wo_ref), rest = refs[:n_x], refs[n_x:n_x + 3], refs[n_x + 3:]
    if n_extra:
        wf_ref, bf_ref, o_ref, wo_bf_ref, of_ref, xn_ref = rest
    else:
        o_ref, wo_bf_ref, xn_ref = rest
    j = pl.program_id(1)
    wo_bf_ref[...] = wo_ref[...].astype(BF16)
    w_contract = 1 if w_transposed else 0

    def matmul(xn, w):
        return lax.dot_general(xn, w, (((1,), (w_contract,)), ((), ())), preferred_element_type=F32)

    def emit(xn):
        o_ref[...] = matmul(xn, w_ref[...].astype(BF16)).astype(BF16)

    @pl.when(j == 0)
    def _():
        x = jnp.concatenate([r[...] for r in x_refs], axis=1)
        xn = _rms_rows(x, nw_ref[...]).astype(BF16)
        xn_ref[...] = xn
        emit(xn)
        if n_extra:
            idx = lax.broadcasted_iota(jnp.int32, wf_ref.shape, 0 if w_transposed else 1)
            wf = jnp.where(idx < n_extra, wf_ref[...], 0.0).astype(BF16)
            of_ref[...] = matmul(xn, wf) + bf_ref[...]

    @pl.when(j > 0)
    def _():
        emit(xn_ref[...])


def _norm_proj(h2d, norm_w, w_all, layer, w_out_all, layer_out, *, n_out, tm, tn, w_transposed=False,
               extra_bias=None):
    m, d = h2d.shape
    nj = n_out // tn
    e_out, d_out = w_out_all.shape[1:]
    wo_rows = e_out // ((m // tm) * nj)
    if w_transposed:
        w_spec = pl.BlockSpec((None, tn, d), lambda i, j: (layer, j, 0))
        wf_spec = pl.BlockSpec((None, LANES, d), lambda i, j: (layer, n_out // LANES, 0))
        n_extra = w_all.shape[1] - n_out
    else:
        w_spec = pl.BlockSpec((None, d, tn), lambda i, j: (layer, 0, j))
        wf_spec = pl.BlockSpec((None, d, LANES), lambda i, j: (layer, 0, n_out // LANES))
        n_extra = w_all.shape[2] - n_out
    ni = m // tm
    n_x = min(N_X_PIECES, nj)

    def x_piece(q):
        return pl.BlockSpec((tm, d // n_x), lambda i, j: (jnp.minimum(i + jnp.where(j > q, 1, 0), ni - 1), q))

    in_specs = [x_piece(q) for q in range(n_x)]
    in_specs += [pl.BlockSpec((1, d), lambda i, j: (0, 0)),
                w_spec,
                pl.BlockSpec((None, wo_rows, d_out), lambda i, j: (layer_out, i * nj + j, 0))]
    out_specs = [pl.BlockSpec((tm, tn), lambda i, j: (i, j)),
                 pl.BlockSpec((wo_rows, d_out), lambda i, j: (i * nj + j, 0))]
    out_shape = [jax.ShapeDtypeStruct((m, n_out), BF16),
                 jax.ShapeDtypeStruct((e_out, d_out), BF16)]
    args = [h2d] * n_x + [norm_w.reshape(1, d), w_all, w_out_all]
    if extra_bias is None:
        n_extra = 0
    else:
        in_specs += [wf_spec, pl.BlockSpec((1, LANES), lambda i, j: (0, 0))]
        out_specs += [pl.BlockSpec((tm, LANES), lambda i, j: (i, 0))]
        out_shape += [jax.ShapeDtypeStruct((m, LANES), F32)]
        args += [w_all, extra_bias]
    outs = pl.pallas_call(
        functools.partial(_norm_proj_body, w_transposed=w_transposed, n_extra=n_extra, n_x=n_x),
        grid=(m // tm, n_out // tn),
        in_specs=in_specs,
        out_specs=out_specs,
        out_shape=out_shape,
        scratch_shapes=[pltpu.VMEM((tm, d), BF16)],
        compiler_params=_params(("arbitrary", "arbitrary")),
        name="norm_proj",
    )(*args)
    return outs


def _out_proj_body(y_ref, z_ref, h_ref, w_ref, o_ref):
    a = (y_ref[...].astype(F32) * _silu(z_ref[...].astype(F32))).astype(BF16)
    o_ref[...] = h_ref[...] + jnp.dot(a, w_ref[...], preferred_element_type=F32)


def _out_proj(y2d, z2d, z_block, h2d, w_bf16, *, tm):
    m, e = y2d.shape
    d = w_bf16.shape[1]
    return pl.pallas_call(
        _out_proj_body,
        grid=(m // tm,),
        in_specs=[pl.BlockSpec((tm, e), lambda i: (i, 0)),
                  pl.BlockSpec((tm, e), lambda i: (i, z_block)),
                  pl.BlockSpec((tm, d), lambda i: (i, 0)),
                  pl.BlockSpec((e, d), lambda i: (0, 0), pipeline_mode=pl.Buffered(1))],
        out_specs=pl.BlockSpec((tm, d), lambda i: (i, 0)),
        out_shape=jax.ShapeDtypeStruct((m, d), F32),
        compiler_params=_params(("parallel",)),
        name="out_proj",
    )(y2d, z2d, h2d, w_bf16)


def _s5_prep_body(a_ref, bt_ref, c_ref, d_ref, wb_ref, wct_ref, toe_ref, lam_ref, *, chunk):
    ns = TILE_STATES
    n_lt = ns // LANES

    def block_diag(x):
        x2 = jnp.concatenate([x, x], axis=1)
        grp_row = lax.broadcasted_iota(jnp.int32, (LANES, LANES), 0) // S5_GROUP
        grp_lane = lax.broadcasted_iota(jnp.int32, (LANES, LANES), 1) // S5_STATE
        per_tile = LANES // S5_STATE
        return jnp.concatenate([jnp.where(grp_row == grp_lane + per_tile * t, x2, 0.0)
                                for t in range(n_lt)], axis=1)

    ar, ai, ldt = a_ref[0:1, :], a_ref[1:2, :], a_ref[2:3, :]
    dt = jnp.exp(ldt)
    k_rows = lax.broadcasted_iota(jnp.int32, (chunk + 1, 1), 0).astype(F32)
    mag = jnp.exp(ar * dt * k_rows)
    ang = ai * dt * k_rows
    pr, pi = mag * jnp.cos(ang), mag * jnp.sin(ang)
    abar_r, abar_i = pr[1:2, :], pi[1:2, :]
    den = ar * ar + ai * ai
    xr = abar_r - 1.0
    fr = (xr * ar + abar_i * ai) / den
    fi = (abar_i * ar - xr * ai) / den
    bt_re, bt_im = block_diag(bt_ref[0]), block_diag(bt_ref[1])
    bbar_re = fr * bt_re - fi * bt_im
    bbar_im = fr * bt_im + fi * bt_re
    c_re, c_im = block_diag(c_ref[0]), block_diag(c_ref[1])
    lam_ref[:, 0:ns] = pr[chunk:chunk + 1, :]
    lam_ref[:, ns:2 * ns] = pi[chunk:chunk + 1, :]

    bb_re, bb_im = bbar_re.astype(BF16), bbar_im.astype(BF16)
    cb_re, cb_im = c_re.astype(BF16), c_im.astype(BF16)
    for tau in range(chunk):
        qr, qi = pr[tau:tau + 1, :].astype(BF16), pi[tau:tau + 1, :].astype(BF16)
        t = chunk - 1 - tau
        wb_ref[t * LANES:(t + 1) * LANES, 0:ns] = bb_re * qr - bb_im * qi
        wb_ref[t * LANES:(t + 1) * LANES, ns:2 * ns] = bb_re * qi + bb_im * qr
        qr, qi = pr[tau + 1:tau + 2, :].astype(BF16), pi[tau + 1:tau + 2, :].astype(BF16)
        wct_ref[tau * LANES:(tau + 1) * LANES, 0:ns] = cb_re * qr - cb_im * qi
        wct_ref[tau * LANES:(tau + 1) * LANES, ns:2 * ns] = -(cb_re * qi + cb_im * qr)

    c_cat = jnp.concatenate([cb_re, -cb_im], axis=1)
    k_all = lax.dot_general(wb_ref[...], c_cat, (((1,), (1,)), ((), ())),
                            preferred_element_type=F32)
    zero_blk = jnp.zeros((LANES, LANES), BF16)
    eye = lax.broadcasted_iota(jnp.int32, (LANES, LANES), 0) == lax.broadcasted_iota(jnp.int32, (LANES, LANES), 1)
    for tau in range(chunk):
        t = chunk - 1 - tau
        k_tau = k_all[t * LANES:(t + 1) * LANES, :]
        if tau == 0:
            k_tau = k_tau + jnp.where(eye, d_ref[...], 0.0)
        k_tau = k_tau.astype(BF16)
        toe_ref[0:LANES, tau * LANES:(tau + 1) * LANES] = k_tau
        if tau + 1 < chunk:
            toe_ref[LANES:2 * LANES, (tau + 1) * LANES:(tau + 2) * LANES] = k_tau
    toe_ref[LANES:2 * LANES, 0:LANES] = zero_blk


def _s5_prep(a_re, a_im, log_dt, b_re, b_im, c_re, c_im, d_skip, *, chunk):
    g, p = a_re.shape
    nt = g // TILE_GROUPS
    ns = TILE_STATES
    ldt = jnp.broadcast_to(log_dt[:, None], (g, p))
    a_rows = jnp.stack([a_re.reshape(nt, ns), a_im.reshape(nt, ns), ldt.reshape(nt, ns)], axis=1)
    bt = jnp.stack([b_re, b_im], axis=0).transpose(1, 0, 3, 2).reshape(nt, TILE_GROUPS, 2, S5_GROUP, p)
    bt = bt.transpose(0, 2, 1, 3, 4).reshape(nt, 2, LANES, p)
    cc = jnp.stack([c_re.reshape(nt, LANES, p), c_im.reshape(nt, LANES, p)], axis=1)
    kw = chunk * LANES
    return pl.pallas_call(
        functools.partial(_s5_prep_body, chunk=chunk),
        grid=(nt,),
        in_specs=[pl.BlockSpec((None, 3, ns), lambda j: (j, 0, 0)),
                  pl.BlockSpec((None, 2, LANES, p), lambda j: (j, 0, 0, 0)),
                  pl.BlockSpec((None, 2, LANES, p), lambda j: (j, 0, 0, 0)),
                  pl.BlockSpec((1, LANES), lambda j: (0, j))],
        out_specs=[pl.BlockSpec((None, kw, 2 * ns), lambda j: (j, 0, 0)),
                   pl.BlockSpec((None, kw, 2 * ns), lambda j: (j, 0, 0)),
                   pl.BlockSpec((None, 2 * LANES, kw), lambda j: (j, 0, 0)),
                   pl.BlockSpec((None, 1, 2 * ns), lambda j: (j, 0, 0))],
        out_shape=[jax.ShapeDtypeStruct((nt, kw, 2 * ns), BF16),
                   jax.ShapeDtypeStruct((nt, kw, 2 * ns), BF16),
                   jax.ShapeDtypeStruct((nt, 2 * LANES, kw), BF16),
                   jax.ShapeDtypeStruct((nt, 1, 2 * ns), F32)],
        compiler_params=_params(("parallel",)),
        name="s5_prep",
    )(a_rows, bt, cc, d_skip.reshape(1, g * S5_GROUP))


def _s5_core_body(ubf_ref, wb_ref, wct_ref, toe_ref, lam_ref, o_ref,
                  u_ref, ub_ref, slab_ref, hin_ref, hprev_ref, *, chunk, n_seq, n_chunks):
    ns = TILE_STATES
    nk = ns // LANES
    seq = n_chunks * chunk

    def token_rows(b, t):
        return pl.ds(b * seq + t, n_chunks, stride=chunk)

    def chunk_rows(b):
        return pl.ds(b, n_chunks, stride=n_seq)

    u_ref[...] = ubf_ref[...].astype(F32)
    for t in range(chunk):
        for b in range(n_seq):
            slab_ref[t, chunk_rows(b), :] = u_ref[token_rows(b, t), :]
        ub_ref[:, t * LANES:(t + 1) * LANES] = slab_ref[t].astype(BF16)

    h_in = jnp.dot(ub_ref[...], wb_ref[...], preferred_element_type=F32)
    for k in range(2 * nk):
        hin_ref[k] = h_in[:, k * LANES:(k + 1) * LANES]

    for a in range(chunk // 2):
        lo = 2 * a * LANES
        width = (chunk - 2 * a) * LANES
        part = jnp.dot(ub_ref[:, lo:lo + 2 * LANES], toe_ref[:, 0:width], preferred_element_type=F32)
        for i in range(chunk - 2 * a):
            if a == 0:
                slab_ref[i] = part[:, i * LANES:(i + 1) * LANES]
            else:
                slab_ref[2 * a + i] += part[:, i * LANES:(i + 1) * LANES]

    lam = [jnp.broadcast_to(lam_ref[:, k * LANES:(k + 1) * LANES], (n_seq, LANES)) for k in range(2 * nk)]
    state = [jnp.zeros((n_seq, LANES), F32) for _ in range(2 * nk)]
    for c in range(n_chunks):
        rows = pl.ds(c * n_seq, n_seq)
        new_state = []
        for k in range(nk):
            hr, hi = state[k], state[nk + k]
            hprev_ref[k, rows, :] = hr
            hprev_ref[nk + k, rows, :] = hi
            lr, li = lam[k], lam[nk + k]
            new_state.append((lr * hr - li * hi + hin_ref[k, rows, :],
                              lr * hi + li * hr + hin_ref[nk + k, rows, :]))
        state = [s[0] for s in new_state] + [s[1] for s in new_state]

    h_prev = jnp.concatenate([hprev_ref[k].astype(BF16) for k in range(2 * nk)], axis=1)
    for a in range(chunk // 2):
        lo = 2 * a * LANES
        y = lax.dot_general(h_prev, wct_ref[lo:lo + 2 * LANES, :], (((1,), (1,)), ((), ())),
                            preferred_element_type=F32)
        for i in range(2):
            t = 2 * a + i
            slab_ref[t] += y[:, i * LANES:(i + 1) * LANES]
            for b in range(n_seq):
                o_ref[token_rows(b, t), :] = slab_ref[t, chunk_rows(b), :]


def _s5_core(proj2d, wb, wct, toe, lam, *, chunk, n_seq):
    m, n2 = proj2d.shape
    e = n2 // 2
    r = m // chunk
    nt = e // LANES
    ns2 = 2 * TILE_STATES
    kw = chunk * LANES
    return pl.pallas_call(
        functools.partial(_s5_core_body, chunk=chunk, n_seq=n_seq, n_chunks=r // n_seq),
        grid=(nt,),
        in_specs=[pl.BlockSpec((m, LANES), lambda j: (0, j)),
                  pl.BlockSpec((None, kw, ns2), lambda j: (j, 0, 0)),
                  pl.BlockSpec((None, kw, ns2), lambda j: (j, 0, 0)),
                  pl.BlockSpec((None, 2 * LANES, kw), lambda j: (j, 0, 0)),
                  pl.BlockSpec((None, 1, ns2), lambda j: (j, 0, 0))],
        out_specs=pl.BlockSpec((m, LANES), lambda j: (0, j)),
        out_shape=jax.ShapeDtypeStruct((m, e), F32),
        scratch_shapes=[pltpu.VMEM((m, LANES), F32),
                        pltpu.VMEM((r, kw), BF16),
                        pltpu.VMEM((chunk, r, LANES), F32),
                        pltpu.VMEM((ns2 // LANES, r, LANES), F32),
                        pltpu.VMEM((ns2 // LANES, r, LANES), F32)],
        compiler_params=_params(("parallel",)),
        name="s5_core",
    )(proj2d, wb, wct, toe, lam)


def _glu_body(y_ref, w_ref, b_ref, o_ref, wbf_ref):
    @pl.when(pl.program_id(0) == 0)
    def _():
        wbf_ref[...] = w_ref[...].astype(BF16)

    g = jax.nn.gelu(y_ref[...])
    acc = jnp.dot(g.astype(BF16), wbf_ref[...], preferred_element_type=F32) + b_ref[...]
    o_ref[...] = (g * jax.nn.sigmoid(acc)).astype(BF16)


def _glu(y2d, w_all, layer, b, *, tm):
    m, e = y2d.shape
    return pl.pallas_call(
        _glu_body,
        grid=(m // tm,),
        in_specs=[pl.BlockSpec((tm, e), lambda i: (i, 0)),
                  pl.BlockSpec((None, e, e), lambda i: (layer, 0, 0), pipeline_mode=pl.Buffered(1)),
                  pl.BlockSpec((1, e), lambda i: (0, 0))],
        out_specs=pl.BlockSpec((tm, e), lambda i: (i, 0)),
        out_shape=jax.ShapeDtypeStruct((m, e), BF16),
        scratch_shapes=[pltpu.VMEM((e, e), BF16)],
        compiler_params=_params(("arbitrary",)),
        name="s5_glu",
    )(y2d, w_all, b.reshape(1, e))


def _fox_cum_body(f_ref, o_ref, *, blk):
    seq = f_ref.shape[0]
    row = lax.broadcasted_iota(jnp.int32, (blk, blk), 0)
    col = lax.broadcasted_iota(jnp.int32, (blk, blk), 1)
    tri = (col <= row).astype(BF16)
    carry = jnp.zeros((1, LANES), F32)
    for i in range(seq // blk):
        x = f_ref[i * blk:(i + 1) * blk, :]
        ls = jnp.minimum(x, 0.0) - jnp.log1p(jnp.exp(-jnp.abs(x)))
        cum = carry
        rest = ls
        for _ in range(3):
            part = rest.astype(BF16)
            cum = cum + jnp.dot(tri, part, preferred_element_type=F32)
            rest = rest - part.astype(F32)
        o_ref[i * blk:(i + 1) * blk, :] = cum
        carry = cum[blk - 1:blk, :]


def _fox_cum(f3d, *, blk):
    b, seq, _ = f3d.shape
    return pl.pallas_call(
        functools.partial(_fox_cum_body, blk=blk),
        grid=(b,),
        in_specs=[pl.BlockSpec((None, seq, LANES), lambda i: (i, 0, 0))],
        out_specs=pl.BlockSpec((None, seq, LANES), lambda i: (i, 0, 0)),
        out_shape=jax.ShapeDtypeStruct((b, seq, LANES), F32),
        compiler_params=_params(("parallel",)),
        name="fox_cum",
    )(f3d)


def _fox_attn_body(q_ref, k_ref, v_ref, ck_ref, qw_ref, kw_ref, o_ref,
                   kn_ref, m_ref, acc_ref, *, tq, scale):
    seq = k_ref.shape[0]
    nq = seq // tq
    log2e = 1.4426950408889634

    def q_block(i):
        rows = slice(i * tq, (i + 1) * tq)
        return (_rms_rows(q_ref[rows, :].astype(F32), qw_ref[...]) * (scale * log2e)).astype(BF16)

    def k_block(j):
        rows = slice(j * tq, (j + 1) * tq)
        kn_ref[rows, :] = _rms_rows(k_ref[rows, :].astype(F32), kw_ref[...]).astype(BF16)

    pairs = [(i, j) for i in range(nq) for j in range(i + 1)]
    qn = {}

    def logits(i, j):
        if j == 0:
            qn[i] = q_block(i)
        if j == i:
            k_block(j)
        return lax.dot_general(qn[i], kn_ref[j * tq:(j + 1) * tq, :],
                               (((1,), (1,)), ((), ())), preferred_element_type=F32)

    n_lt = tq // LANES
    ones_blk = jnp.ones((tq, LANES), BF16)
    s_next = logits(*pairs[0])
    for idx, (i, j) in enumerate(pairs):
        s = s_next
        if idx + 1 < len(pairs):
            s_next = logits(*pairs[idx + 1])
        c0 = ck_ref[i:i + 1, 0:1]
        dk = (ck_ref[j:j + 1, :] - c0) * log2e
        s_t = [s[:, t * LANES:(t + 1) * LANES] - dk[:, t * LANES:(t + 1) * LANES] for t in range(n_lt)]
        if i == j:
            r_i = lax.broadcasted_iota(jnp.int32, (tq, LANES), 0)
            c_i = lax.broadcasted_iota(jnp.int32, (tq, LANES), 1)
            s_t = [jnp.where(c_i + t * LANES <= r_i, s_t[t], NEG_BIG) for t in range(n_lt)]
        mx = functools.reduce(jnp.maximum, s_t)
        row_max = jnp.broadcast_to(jnp.max(mx, axis=-1, keepdims=True), (tq, LANES))
        v_blk = v_ref[j * tq:(j + 1) * tq, :]
        if j == 0:
            m_new = row_max
        else:
            m_old = m_ref[...]
            m_new = jnp.maximum(m_old, row_max)
        p = jnp.concatenate([jnp.exp2(s_t[t] - m_new).astype(BF16) for t in range(n_lt)], axis=1)
        pv = jnp.dot(p, jnp.concatenate([v_blk, ones_blk], axis=1), preferred_element_type=F32)
        if j == 0:
            acc_new = pv
        else:
            alpha = jnp.exp2(m_old - m_new)
            acc_new = jnp.concatenate([alpha, alpha], axis=1) * acc_ref[...] + pv
        if j == i:
            o_ref[i * tq:(i + 1) * tq, :] = (acc_new[:, 0:LANES] / acc_new[:, LANES:2 * LANES]).astype(BF16)
        else:
            m_ref[...] = m_new
            acc_ref[...] = acc_new


def _fox_attn(proj3d, cum_k, q_w, k_w, *, heads, tq):
    b, seq, _ = proj3d.shape
    dh = FOX_HEAD_DIM
    nq = seq // tq
    return pl.pallas_call(
        functools.partial(_fox_attn_body, tq=tq, scale=dh ** -0.5),
        grid=(b, heads),
        in_specs=[pl.BlockSpec((None, seq, dh), lambda bi, h: (bi, 0, h)),
                  pl.BlockSpec((None, seq, dh), lambda bi, h: (bi, 0, heads + h)),
                  pl.BlockSpec((None, seq, dh), lambda bi, h: (bi, 0, 2 * heads + h)),
                  pl.BlockSpec((None, None, nq, tq), lambda bi, h: (bi, h, 0, 0)),
                  pl.BlockSpec((1, dh), lambda bi, h: (0, 0)),
                  pl.BlockSpec((1, dh), lambda bi, h: (0, 0))],
        out_specs=pl.BlockSpec((None, seq, dh), lambda bi, h: (bi, 0, h)),
        out_shape=jax.ShapeDtypeStruct((b, seq, heads * dh), BF16),
        scratch_shapes=[pltpu.VMEM((seq, dh), BF16),
                        pltpu.VMEM((tq, LANES), F32),
                        pltpu.VMEM((tq, dh + LANES), F32)],
        compiler_params=_params(("parallel", "parallel")),
        name="fox_attn",
    )(proj3d, proj3d, proj3d, cum_k, q_w.reshape(1, dh), k_w.reshape(1, dh))


def _pool_body(u_ref, w_ref, s_ref, o_ref, sa_ref, sb_ref, *, windows):
    seq, cg = u_ref.shape
    halo = POOL_HALO
    grp = pl.program_id(1)
    zeros = jnp.zeros((halo, cg), F32)
    sa_ref[0:halo, :] = zeros
    sb_ref[0:halo, :] = zeros
    t_idx = lax.broadcasted_iota(jnp.int32, (seq, 1), 0)

    for gi, win in enumerate(windows):
        @pl.when(grp == gi)
        def _(win=win):
            u = u_ref[...].astype(F32)
            sa_ref[halo:halo + seq, :] = u
            bufs = (sa_ref, sb_ref)
            s = u
            shift = 1
            level = 0
            while shift < win:
                src = bufs[level % 2]
                s = src[halo:halo + seq, :] + src[halo - shift:halo - shift + seq, :]
                shift *= 2
                level += 1
                if shift < win:
                    bufs[level % 2][halo:halo + seq, :] = s
            cnt = jnp.minimum(t_idx + 1, win).astype(F32)
            diff = (s / cnt - u).astype(BF16)
            mixed = jnp.dot(diff, w_ref[...].astype(BF16), preferred_element_type=F32)
            o_ref[...] = (mixed * s_ref[...]).astype(BF16)


def _pool(proj3d, w_group_all, layer, scale, *, windows):
    b, seq, n2 = proj3d.shape
    e = n2 // 2
    ng = len(windows)
    cg = e // ng
    return pl.pallas_call(
        functools.partial(_pool_body, windows=windows),
        grid=(b, ng),
        in_specs=[pl.BlockSpec((None, seq, cg), lambda bi, g: (bi, 0, g)),
                  pl.BlockSpec((None, None, cg, cg), lambda bi, g: (layer, g, 0, 0)),
                  pl.BlockSpec((1, cg), lambda bi, g: (0, g))],
        out_specs=pl.BlockSpec((None, seq, cg), lambda bi, g: (bi, 0, g)),
        out_shape=jax.ShapeDtypeStruct((b, seq, e), BF16),
        scratch_shapes=[pltpu.VMEM((POOL_HALO + seq, cg), F32),
                        pltpu.VMEM((POOL_HALO + seq, cg), F32)],
        compiler_params=_params(("parallel", "parallel")),
        name="pool_mix",
    )(proj3d, w_group_all, scale.reshape(1, e))


def _pick(n, pref):
    t = min(n, pref)
    while n % t:
        t //= 2
    return t


def kernel(x, norm_w, out_proj, s5_in_proj, s5_a_re, s5_a_im, s5_log_dt, s5_b_re, s5_b_im, s5_c_re, s5_c_im, s5_d, s5_w_glu, s5_b_glu, fox_in_proj, fox_q_norm, fox_k_norm, fox_f_bias, pool_in_proj, pool_w_group, pool_scale):
    bsz, seq, d = x.shape
    depth = norm_w.shape[0]
    e = s5_d.shape[1]
    heads = e // FOX_HEAD_DIM
    m = bsz * seq
    n_mixers = 3

    assert e % (LANES * len(POOL_WINDOWS)) == 0 and e % FOX_HEAD_DIM == 0 and heads <= LANES
    assert seq % CHUNK == 0 and s5_a_re.shape[1:] == (e // S5_GROUP, S5_STATE)
    assert fox_in_proj.shape[2] == 4 * e + heads and out_proj.shape[1:] == (e, d)

    tm = _pick(m, 1024)
    tn_in = _pick(2 * e, 1024)
    tm_out = _pick(m, 512)
    tm_glu = _pick(m, 512)
    tq = _pick(seq, 512)

    h = x.reshape(m, d)
    for i in range(depth):
        kind, j = i % n_mixers, i // n_mixers
        if kind == 0:
            wb, wct, toe, lam = _s5_prep(s5_a_re[j], s5_a_im[j], s5_log_dt[j], s5_b_re[j], s5_b_im[j],
                                         s5_c_re[j], s5_c_im[j], s5_d[j], chunk=CHUNK)
            proj, w_out = _norm_proj(h, norm_w[i], s5_in_proj, j, out_proj, i, n_out=2 * e, tm=tm, tn=tn_in)
            g = _s5_core(proj, wb, wct, toe, lam, chunk=CHUNK, n_seq=bsz)
            y = _glu(g, s5_w_glu, j, s5_b_glu[j], tm=tm_glu)
            h = _out_proj(y, proj, 1, h, w_out, tm=tm_out)
        elif kind == 1:
            n_main = 4 * e
            b_f = jnp.pad(fox_f_bias[j], (0, LANES - heads)).reshape(1, LANES)
            proj, w_out, f_logit = _norm_proj(h, norm_w[i], jnp.swapaxes(fox_in_proj, 1, 2), j, out_proj, i,
                                              n_out=n_main, tm=tm, tn=tn_in, w_transposed=True, extra_bias=b_f)
            cum = _fox_cum(f_logit.reshape(bsz, seq, LANES), blk=_pick(seq, 256))
            cum_k = cum[:, :, :heads].transpose(0, 2, 1).reshape(bsz, heads, seq // tq, tq)
            y = _fox_attn(proj.reshape(bsz, seq, n_main), cum_k, fox_q_norm[j], fox_k_norm[j],
                          heads=heads, tq=tq)
            h = _out_proj(y.reshape(m, e), proj, 3, h, w_out, tm=tm_out)
        else:
            proj, w_out = _norm_proj(h, norm_w[i], pool_in_proj, j, out_proj, i, n_out=2 * e, tm=tm, tn=tn_in)
            y = _pool(proj.reshape(bsz, seq, 2 * e), pool_w_group, j, pool_scale[j], windows=POOL_WINDOWS)
            h = _out_proj(y.reshape(m, e), proj, 1, h, w_out, tm=tm_out)
    return h.reshape(bsz, seq, d)
```

```python
import functools

import jax
import jax.numpy as jnp
from jax import lax
from jax.experimental import pallas as pl
from jax.experimental.pallas import tpu as pltpu

F32 = jnp.float32
BF16 = jnp.bfloat16

LANES = 128
S5_GROUP = 16
S5_STATE = 64
TILE_GROUPS = LANES // S5_GROUP
TILE_STATES = TILE_GROUPS * S5_STATE
CHUNK = 8
FOX_HEAD_DIM = 128
POOL_WINDOWS = (2, 4, 8, 16)
POOL_HALO = 16
EPS = 1e-6
NEG_BIG = -1e30
VMEM_LIMIT = 56 * 1024 * 1024
N_X_PIECES = 4


def _params(semantics):
    return pltpu.CompilerParams(dimension_semantics=semantics, vmem_limit_bytes=VMEM_LIMIT)


def _rms_rows(x, w):
    ms = jnp.mean(x * x, axis=-1, keepdims=True)
    return x * lax.rsqrt(ms + EPS) * w


def _silu(z):
    return z * jax.nn.sigmoid(z)


def _norm_proj_body(*refs, w_transposed, n_extra, n_x):
    x_refs, (nw_ref, w_ref, wo_ref), rest = refs[:n_x], refs[n_x:n_x + 3], refs[n_x + 3:]
    if n_extra:
        wf_ref, bf_ref, o_ref, wo_bf_ref, of_ref, xn_ref = rest
    else:
        o_ref, wo_bf_ref, xn_ref = rest
    j = pl.program_id(1)
    wo_bf_ref[...] = wo_ref[...].astype(BF16)
    w_contract = 1 if w_transposed else 0

    def matmul(xn, w):
        return lax.dot_general(xn, w, (((1,), (w_contract,)), ((), ())), preferred_element_type=F32)

    def emit(xn):
        o_ref[...] = matmul(xn, w_ref[...].astype(BF16)).astype(BF16)

    @pl.when(j == 0)
    def _():
        x = jnp.concatenate([r[...] for r in x_refs], axis=1)
        xn = _rms_rows(x, nw_ref[...]).astype(BF16)
        xn_ref[...] = xn
        emit(xn)
        if n_extra:
            idx = lax.broadcasted_iota(jnp.int32, wf_ref.shape, 0 if w_transposed else 1)
            wf = jnp.where(idx < n_extra, wf_ref[...], 0.0).astype(BF16)
            of_ref[...] = matmul(xn, wf) + bf_ref[...]

    @pl.when(j > 0)
    def _():
        emit(xn_ref[...])


def _norm_proj(h2d, norm_w, w_all, layer, w_out_all, layer_out, *, n_out, tm, tn, w_transposed=False,
               extra_bias=None):
    m, d = h2d.shape
    nj = n_out // tn
    e_out, d_out = w_out_all.shape[1:]
    wo_rows = e_out // ((m // tm) * nj)
    if w_transposed:
        w_spec = pl.BlockSpec((None, tn, d), lambda i, j: (layer, j, 0))
        wf_spec = pl.BlockSpec((None, LANES, d), lambda i, j: (layer, n_out // LANES, 0))
        n_extra = w_all.shape[1] - n_out
    else:
        w_spec = pl.BlockSpec((None, d, tn), lambda i, j: (layer, 0, j))
        wf_spec = pl.BlockSpec((None, d, LANES), lambda i, j: (layer, 0, n_out // LANES))
        n_extra = w_all.shape[2] - n_out
    ni = m // tm
    n_x = min(N_X_PIECES, nj)

    def x_piece(q):
        return pl.BlockSpec((tm, d // n_x), lambda i, j: (jnp.minimum(i + jnp.where(j > q, 1, 0), ni - 1), q))

    in_specs = [x_piece(q) for q in range(n_x)]
    in_specs += [pl.BlockSpec((1, d), lambda i, j: (0, 0)),
                w_spec,
                pl.BlockSpec((None, wo_rows, d_out), lambda i, j: (layer_out, i * nj + j, 0))]
    out_specs = [pl.BlockSpec((tm, tn), lambda i, j: (i, j)),
                 pl.BlockSpec((wo_rows, d_out), lambda i, j: (i * nj + j, 0))]
    out_shape = [jax.ShapeDtypeStruct((m, n_out), BF16),
                 jax.ShapeDtypeStruct((e_out, d_out), BF16)]
    args = [h2d] * n_x + [norm_w.reshape(1, d), w_all, w_out_all]
    if extra_bias is None:
        n_extra = 0
    else:
        in_specs += [wf_spec, pl.BlockSpec((1, LANES), lambda i, j: (0, 0))]
        out_specs += [pl.BlockSpec((tm, LANES), lambda i, j: (i, 0))]
        out_shape += [jax.ShapeDtypeStruct((m, LANES), F32)]
        args += [w_all, extra_bias]
    outs = pl.pallas_call(
        functools.partial(_norm_proj_body, w_transposed=w_transposed, n_extra=n_extra, n_x=n_x),
        grid=(m // tm, n_out // tn),
        in_specs=in_specs,
        out_specs=out_specs,
        out_shape=out_shape,
        scratch_shapes=[pltpu.VMEM((tm, d), BF16)],
        compiler_params=_params(("arbitrary", "arbitrary")),
        name="norm_proj",
    )(*args)
    return outs


def _out_proj_body(y_ref, z_ref, h_ref, w_ref, o_ref):
    a = (y_ref[...].astype(F32) * _silu(z_ref[...].astype(F32))).astype(BF16)
    o_ref[...] = h_ref[...] + jnp.dot(a, w_ref[...], preferred_element_type=F32)


def _out_proj(y2d, z2d, z_block, h2d, w_bf16, *, tm):
    m, e = y2d.shape
    d = w_bf16.shape[1]
    return pl.pallas_call(
        _out_proj_body,
        grid=(m // tm,),
        in_specs=[pl.BlockSpec((tm, e), lambda i: (i, 0)),
                  pl.BlockSpec((tm, e), lambda i: (i, z_block)),
                  pl.BlockSpec((tm, d), lambda i: (i, 0)),
                  pl.BlockSpec((e, d), lambda i: (0, 0), pipeline_mode=pl.Buffered(1))],
        out_specs=pl.BlockSpec((tm, d), lambda i: (i, 0)),
        out_shape=jax.ShapeDtypeStruct((m, d), F32),
        compiler_params=_params(("parallel",)),
        name="out_proj",
    )(y2d, z2d, h2d, w_bf16)


def _s5_operators(a_ref, bt_ref, c_ref, d_ref, wb_ref, wct_ref, toe_ref, lam_ref, *, chunk):
    ns = TILE_STATES
    n_lt = ns // LANES

    def block_diag(x):
        x2 = jnp.concatenate([x, x], axis=1)
        grp_row = lax.broadcasted_iota(jnp.int32, (LANES, LANES), 0) // S5_GROUP
        grp_lane = lax.broadcasted_iota(jnp.int32, (LANES, LANES), 1) // S5_STATE
        per_tile = LANES // S5_STATE
        return jnp.concatenate([jnp.where(grp_row == grp_lane + per_tile * t, x2, 0.0)
                                for t in range(n_lt)], axis=1)

    ar, ai, ldt = a_ref[0:1, :], a_ref[1:2, :], a_ref[2:3, :]
    dt = jnp.exp(ldt)
    k_rows = lax.broadcasted_iota(jnp.int32, (chunk + 1, 1), 0).astype(F32)
    mag = jnp.exp(ar * dt * k_rows)
    ang = ai * dt * k_rows
    pr, pi = mag * jnp.cos(ang), mag * jnp.sin(ang)
    abar_r, abar_i = pr[1:2, :], pi[1:2, :]
    den = ar * ar + ai * ai
    xr = abar_r - 1.0
    fr = (xr * ar + abar_i * ai) / den
    fi = (abar_i * ar - xr * ai) / den
    bt_re, bt_im = block_diag(bt_ref[0]), block_diag(bt_ref[1])
    bbar_re = fr * bt_re - fi * bt_im
    bbar_im = fr * bt_im + fi * bt_re
    c_re, c_im = block_diag(c_ref[0]), block_diag(c_ref[1])
    lam_ref[:, 0:ns] = pr[chunk:chunk + 1, :]
    lam_ref[:, ns:2 * ns] = pi[chunk:chunk + 1, :]

    bb_re, bb_im = bbar_re.astype(BF16), bbar_im.astype(BF16)
    cb_re, cb_im = c_re.astype(BF16), c_im.astype(BF16)
    for tau in range(chunk):
        qr, qi = pr[tau:tau + 1, :].astype(BF16), pi[tau:tau + 1, :].astype(BF16)
        t = chunk - 1 - tau
        wb_ref[t * LANES:(t + 1) * LANES, 0:ns] = bb_re * qr - bb_im * qi
        wb_ref[t * LANES:(t + 1) * LANES, ns:2 * ns] = bb_re * qi + bb_im * qr
        qr, qi = pr[tau + 1:tau + 2, :].astype(BF16), pi[tau + 1:tau + 2, :].astype(BF16)
        wct_ref[tau * LANES:(tau + 1) * LANES, 0:ns] = cb_re * qr - cb_im * qi
        wct_ref[tau * LANES:(tau + 1) * LANES, ns:2 * ns] = -(cb_re * qi + cb_im * qr)

    def lag_operators():
        c_cat = jnp.concatenate([cb_re, -cb_im], axis=1)
        k_all = lax.dot_general(wb_ref[...], c_cat, (((1,), (1,)), ((), ())),
                                preferred_element_type=F32)
        zero_blk = jnp.zeros((LANES, LANES), BF16)
        eye = (lax.broadcasted_iota(jnp.int32, (LANES, LANES), 0)
               == lax.broadcasted_iota(jnp.int32, (LANES, LANES), 1))
        for tau in range(chunk):
            t = chunk - 1 - tau
            k_tau = k_all[t * LANES:(t + 1) * LANES, :]
            if tau == 0:
                k_tau = k_tau + jnp.where(eye, d_ref[...], 0.0)
            k_tau = k_tau.astype(BF16)
            toe_ref[0:LANES, tau * LANES:(tau + 1) * LANES] = k_tau
            if tau + 1 < chunk:
                toe_ref[LANES:2 * LANES, (tau + 1) * LANES:(tau + 2) * LANES] = k_tau
        toe_ref[LANES:2 * LANES, 0:LANES] = zero_blk

    return lag_operators


def _s5_tile_params(a_re, a_im, log_dt, b_re, b_im, c_re, c_im, d_skip):
    g, p = a_re.shape
    nt = g // TILE_GROUPS
    ns = TILE_STATES
    ldt = jnp.broadcast_to(log_dt[:, None], (g, p))
    a_rows = jnp.stack([a_re.reshape(nt, ns), a_im.reshape(nt, ns), ldt.reshape(nt, ns)], axis=1)
    bt = jnp.stack([b_re, b_im], axis=0).transpose(1, 0, 3, 2).reshape(nt, TILE_GROUPS, 2, S5_GROUP, p)
    bt = bt.transpose(0, 2, 1, 3, 4).reshape(nt, 2, LANES, p)
    cc = jnp.stack([c_re.reshape(nt, LANES, p), c_im.reshape(nt, LANES, p)], axis=1)
    return a_rows, bt, cc, d_skip.reshape(1, g * S5_GROUP)


def _s5_core_body(ubf_ref, a0_ref, bt0_ref, c0_ref, d0_ref, a1_ref, bt1_ref, c1_ref, d1_ref, o_ref,
                  u_ref, ub_ref, slab_ref, hin_ref, hprev_ref, *op_refs, chunk, n_seq, n_chunks):
    ns = TILE_STATES
    nk = ns // LANES
    seq = n_chunks * chunk
    step = pl.program_id(0)

    def token_rows(b, t):
        return pl.ds(b * seq + t, n_chunks, stride=chunk)

    def chunk_rows(b):
        return pl.ds(b, n_chunks, stride=n_seq)

    def operators(slot):
        return op_refs[4 * slot:4 * slot + 4]

    def run_tile(slot):
        wb_ref, wct_ref, toe_ref, lam_ref = operators(slot)
        next_lag_operators = _s5_operators(a1_ref, bt1_ref, c1_ref, d1_ref, *operators(1 - slot), chunk=chunk)
        u_ref[...] = ubf_ref[...].astype(F32)
        for t in range(chunk):
            for b in range(n_seq):
                slab_ref[t, chunk_rows(b), :] = u_ref[token_rows(b, t), :]
            ub_ref[:, t * LANES:(t + 1) * LANES] = slab_ref[t].astype(BF16)

        h_in = jnp.dot(ub_ref[...], wb_ref[...], preferred_element_type=F32)
        for k in range(2 * nk):
            hin_ref[k] = h_in[:, k * LANES:(k + 1) * LANES]

        for a in range(chunk // 2):
            lo = 2 * a * LANES
            width = (chunk - 2 * a) * LANES
            part = jnp.dot(ub_ref[:, lo:lo + 2 * LANES], toe_ref[:, 0:width], preferred_element_type=F32)
            for i in range(chunk - 2 * a):
                if a == 0:
                    slab_ref[i] = part[:, i * LANES:(i + 1) * LANES]
                else:
                    slab_ref[2 * a + i] += part[:, i * LANES:(i + 1) * LANES]

        lam = [jnp.broadcast_to(lam_ref[:, k * LANES:(k + 1) * LANES], (n_seq, LANES)) for k in range(2 * nk)]
        state = [jnp.zeros((n_seq, LANES), F32) for _ in range(2 * nk)]
        for c in range(n_chunks):
            rows = pl.ds(c * n_seq, n_seq)
            new_state = []
            for k in range(nk):
                hr, hi = state[k], state[nk + k]
                hprev_ref[k, rows, :] = hr
                hprev_ref[nk + k, rows, :] = hi
                lr, li = lam[k], lam[nk + k]
                new_state.append((lr * hr - li * hi + hin_ref[k, rows, :],
                                  lr * hi + li * hr + hin_ref[nk + k, rows, :]))
            state = [s[0] for s in new_state] + [s[1] for s in new_state]

        h_prev = jnp.concatenate([hprev_ref[k].astype(BF16) for k in range(2 * nk)], axis=1)
        for a in range(chunk // 2):
            lo = 2 * a * LANES
            y = lax.dot_general(h_prev, wct_ref[lo:lo + 2 * LANES, :], (((1,), (1,)), ((), ())),
                                preferred_element_type=F32)
            for i in range(2):
                t = 2 * a + i
                slab_ref[t] += y[:, i * LANES:(i + 1) * LANES]
                for b in range(n_seq):
                    o_ref[token_rows(b, t), :] = slab_ref[t, chunk_rows(b), :]
        next_lag_operators()

    @pl.when(step == 0)
    def _():
        _s5_operators(a0_ref, bt0_ref, c0_ref, d0_ref, *operators(0), chunk=chunk)()

    for parity in range(2):
        pl.when(step % 2 == parity)(functools.partial(run_tile, parity))


def _s5_core(proj2d, a_rows, bt, cc, d_row, *, chunk, n_seq):
    m, n2 = proj2d.shape
    e = n2 // 2
    r = m // chunk
    nt = e // LANES
    ns = TILE_STATES
    p = bt.shape[-1]
    kw = chunk * LANES

    def param_specs(tile):
        return [pl.BlockSpec((None, 3, ns), lambda j: (tile(j), 0, 0)),
                pl.BlockSpec((None, 2, LANES, p), lambda j: (tile(j), 0, 0, 0)),
                pl.BlockSpec((None, 2, LANES, p), lambda j: (tile(j), 0, 0, 0)),
                pl.BlockSpec((1, LANES), lambda j: (0, tile(j)))]

    params = [a_rows, bt, cc, d_row]
    return pl.pallas_call(
        functools.partial(_s5_core_body, chunk=chunk, n_seq=n_seq, n_chunks=r // n_seq),
        grid=(nt,),
        in_specs=([pl.BlockSpec((m, LANES), lambda j: (0, j))]
                  + param_specs(lambda j: j) + param_specs(lambda j: jnp.minimum(j + 1, nt - 1))),
        out_specs=pl.BlockSpec((m, LANES), lambda j: (0, j)),
        out_shape=jax.ShapeDtypeStruct((m, e), F32),
        scratch_shapes=[pltpu.VMEM((m, LANES), F32),
                        pltpu.VMEM((r, kw), BF16),
                        pltpu.VMEM((chunk, r, LANES), F32),
                        pltpu.VMEM((2 * ns // LANES, r, LANES), F32),
                        pltpu.VMEM((2 * ns // LANES, r, LANES), F32),
                        *([pltpu.VMEM((kw, 2 * ns), BF16),
                           pltpu.VMEM((kw, 2 * ns), BF16),
                           pltpu.VMEM((2 * LANES, kw), BF16),
                           pltpu.VMEM((1, 2 * ns), F32)] * 2)],
        compiler_params=_params(("arbitrary",)),
        name="s5_core",
    )(proj2d, *params, *params)


def _glu_body(y_ref, w_ref, b_ref, o_ref, wbf_ref):
    @pl.when(pl.program_id(0) == 0)
    def _():
        wbf_ref[...] = w_ref[...].astype(BF16)

    g = jax.nn.gelu(y_ref[...])
    acc = jnp.dot(g.astype(BF16), wbf_ref[...], preferred_element_type=F32) + b_ref[...]
    o_ref[...] = (g * jax.nn.sigmoid(acc)).astype(BF16)


def _glu(y2d, w_all, layer, b, *, tm):
    m, e = y2d.shape
    return pl.pallas_call(
        _glu_body,
        grid=(m // tm,),
        in_specs=[pl.BlockSpec((tm, e), lambda i: (i, 0)),
                  pl.BlockSpec((None, e, e), lambda i: (layer, 0, 0), pipeline_mode=pl.Buffered(1)),
                  pl.BlockSpec((1, e), lambda i: (0, 0))],
        out_specs=pl.BlockSpec((tm, e), lambda i: (i, 0)),
        out_shape=jax.ShapeDtypeStruct((m, e), BF16),
        scratch_shapes=[pltpu.VMEM((e, e), BF16)],
        compiler_params=_params(("arbitrary",)),
        name="s5_glu",
    )(y2d, w_all, b.reshape(1, e))


def _fox_cum_body(f_ref, o_ref, *, blk):
    seq = f_ref.shape[0]
    row = lax.broadcasted_iota(jnp.int32, (blk, blk), 0)
    col = lax.broadcasted_iota(jnp.int32, (blk, blk), 1)
    tri = (col <= row).astype(BF16)
    carry = jnp.zeros((1, LANES), F32)
    for i in range(seq // blk):
        x = f_ref[i * blk:(i + 1) * blk, :]
        ls = jnp.minimum(x, 0.0) - jnp.log1p(jnp.exp(-jnp.abs(x)))
        cum = carry
        rest = ls
        for _ in range(3):
            part = rest.astype(BF16)
            cum = cum + jnp.dot(tri, part, preferred_element_type=F32)
            rest = rest - part.astype(F32)
        o_ref[i * blk:(i + 1) * blk, :] = cum
        carry = cum[blk - 1:blk, :]


def _fox_cum(f3d, *, blk):
    b, seq, _ = f3d.shape
    return pl.pallas_call(
        functools.partial(_fox_cum_body, blk=blk),
        grid=(b,),
        in_specs=[pl.BlockSpec((None, seq, LANES), lambda i: (i, 0, 0))],
        out_specs=pl.BlockSpec((None, seq, LANES), lambda i: (i, 0, 0)),
        out_shape=jax.ShapeDtypeStruct((b, seq, LANES), F32),
        compiler_params=_params(("parallel",)),
        name="fox_cum",
    )(f3d)


def _fox_attn_body(q_ref, k_ref, v_ref, ck_ref, qw_ref, kw_ref, o_ref,
                   kn_ref, m_ref, acc_ref, *, tq, scale):
    seq = k_ref.shape[0]
    nq = seq // tq
    log2e = 1.4426950408889634

    def q_block(i):
        rows = slice(i * tq, (i + 1) * tq)
        return (_rms_rows(q_ref[rows, :].astype(F32), qw_ref[...]) * (scale * log2e)).astype(BF16)

    def k_block(j):
        rows = slice(j * tq, (j + 1) * tq)
        kn_ref[rows, :] = _rms_rows(k_ref[rows, :].astype(F32), kw_ref[...]).astype(BF16)

    pairs = [(i, j) for i in range(nq) for j in range(i + 1)]
    qn = {}

    def logits(i, j):
        if j == 0:
            qn[i] = q_block(i)
        if j == i:
            k_block(j)
        return lax.dot_general(qn[i], kn_ref[j * tq:(j + 1) * tq, :],
                               (((1,), (1,)), ((), ())), preferred_element_type=F32)

    n_lt = tq // LANES
    ones_blk = jnp.ones((tq, LANES), BF16)
    s_next = logits(*pairs[0])
    for idx, (i, j) in enumerate(pairs):
        s = s_next
        if idx + 1 < len(pairs):
            s_next = logits(*pairs[idx + 1])
        c0 = ck_ref[i:i + 1, 0:1]
        dk = (ck_ref[j:j + 1, :] - c0) * log2e
        s_t = [s[:, t * LANES:(t + 1) * LANES] - dk[:, t * LANES:(t + 1) * LANES] for t in range(n_lt)]
        if i == j:
            r_i = lax.broadcasted_iota(jnp.int32, (tq, LANES), 0)
            c_i = lax.broadcasted_iota(jnp.int32, (tq, LANES), 1)
            s_t = [jnp.where(c_i + t * LANES <= r_i, s_t[t], NEG_BIG) for t in range(n_lt)]
        mx = functools.reduce(jnp.maximum, s_t)
        row_max = jnp.broadcast_to(jnp.max(mx, axis=-1, keepdims=True), (tq, LANES))
        v_blk = v_ref[j * tq:(j + 1) * tq, :]
        if j == 0:
            m_new = row_max
        else:
            m_old = m_ref[...]
            m_new = jnp.maximum(m_old, row_max)
        p = jnp.concatenate([jnp.exp2(s_t[t] - m_new).astype(BF16) for t in range(n_lt)], axis=1)
        pv = jnp.dot(p, jnp.concatenate([v_blk, ones_blk], axis=1), preferred_element_type=F32)
        if j == 0:
            acc_new = pv
        else:
            alpha = jnp.exp2(m_old - m_new)
            acc_new = jnp.concatenate([alpha, alpha], axis=1) * acc_ref[...] + pv
        if j == i:
            o_ref[i * tq:(i + 1) * tq, :] = (acc_new[:, 0:LANES] / acc_new[:, LANES:2 * LANES]).astype(BF16)
        else:
            m_ref[...] = m_new
            acc_ref[...] = acc_new


def _fox_attn(proj3d, cum_k, q_w, k_w, *, heads, tq):
    b, seq, _ = proj3d.shape
    dh = FOX_HEAD_DIM
    nq = seq // tq
    return pl.pallas_call(
        functools.partial(_fox_attn_body, tq=tq, scale=dh ** -0.5),
        grid=(b, heads),
        in_specs=[pl.BlockSpec((None, seq, dh), lambda bi, h: (bi, 0, h)),
                  pl.BlockSpec((None, seq, dh), lambda bi, h: (bi, 0, heads + h)),
                  pl.BlockSpec((None, seq, dh), lambda bi, h: (bi, 0, 2 * heads + h)),
                  pl.BlockSpec((None, None, nq, tq), lambda bi, h: (bi, h, 0, 0)),
                  pl.BlockSpec((1, dh), lambda bi, h: (0, 0)),
                  pl.BlockSpec((1, dh), lambda bi, h: (0, 0))],
        out_specs=pl.BlockSpec((None, seq, dh), lambda bi, h: (bi, 0, h)),
        out_shape=jax.ShapeDtypeStruct((b, seq, heads * dh), BF16),
        scratch_shapes=[pltpu.VMEM((seq, dh), BF16),
                        pltpu.VMEM((tq, LANES), F32),
                        pltpu.VMEM((tq, dh + LANES), F32)],
        compiler_params=_params(("parallel", "parallel")),
        name="fox_attn",
    )(proj3d, proj3d, proj3d, cum_k, q_w.reshape(1, dh), k_w.reshape(1, dh))


def _pool_body(u_ref, w_ref, s_ref, o_ref, sa_ref, sb_ref, *, windows):
    seq, cg = u_ref.shape
    halo = POOL_HALO
    grp = pl.program_id(1)
    zeros = jnp.zeros((halo, cg), F32)
    sa_ref[0:halo, :] = zeros
    sb_ref[0:halo, :] = zeros
    t_idx = lax.broadcasted_iota(jnp.int32, (seq, 1), 0)

    for gi, win in enumerate(windows):
        @pl.when(grp == gi)
        def _(win=win):
            u = u_ref[...].astype(F32)
            sa_ref[halo:halo + seq, :] = u
            bufs = (sa_ref, sb_ref)
            s = u
            shift = 1
            level = 0
            while shift < win:
                src = bufs[level % 2]
                s = src[halo:halo + seq, :] + src[halo - shift:halo - shift + seq, :]
                shift *= 2
                level += 1
                if shift < win:
                    bufs[level % 2][halo:halo + seq, :] = s
            cnt = jnp.minimum(t_idx + 1, win).astype(F32)
            diff = (s / cnt - u).astype(BF16)
            mixed = jnp.dot(diff, w_ref[...].astype(BF16), preferred_element_type=F32)
            o_ref[...] = (mixed * s_ref[...]).astype(BF16)


def _pool(proj3d, w_group_all, layer, scale, *, windows):
    b, seq, n2 = proj3d.shape
    e = n2 // 2
    ng = len(windows)
    cg = e // ng
    return pl.pallas_call(
        functools.partial(_pool_body, windows=windows),
        grid=(b, ng),
        in_specs=[pl.BlockSpec((None, seq, cg), lambda bi, g: (bi, 0, g)),
                  pl.BlockSpec((None, None, cg, cg), lambda bi, g: (layer, g, 0, 0)),
                  pl.BlockSpec((1, cg), lambda bi, g: (0, g))],
        out_specs=pl.BlockSpec((None, seq, cg), lambda bi, g: (bi, 0, g)),
        out_shape=jax.ShapeDtypeStruct((b, seq, e), BF16),
        scratch_shapes=[pltpu.VMEM((POOL_HALO + seq, cg), F32),
                        pltpu.VMEM((POOL_HALO + seq, cg), F32)],
        compiler_params=_params(("parallel", "parallel")),
        name="pool_mix",
    )(proj3d, w_group_all, scale.reshape(1, e))


def _pick(n, pref):
    t = min(n, pref)
    while n % t:
        t //= 2
    return t


def kernel(x, norm_w, out_proj, s5_in_proj, s5_a_re, s5_a_im, s5_log_dt, s5_b_re, s5_b_im, s5_c_re, s5_c_im, s5_d, s5_w_glu, s5_b_glu, fox_in_proj, fox_q_norm, fox_k_norm, fox_f_bias, pool_in_proj, pool_w_group, pool_scale):
    bsz, seq, d = x.shape
    depth = norm_w.shape[0]
    e = s5_d.shape[1]
    heads = e // FOX_HEAD_DIM
    m = bsz * seq
    n_mixers = 3

    assert e % (LANES * len(POOL_WINDOWS)) == 0 and e % FOX_HEAD_DIM == 0 and heads <= LANES
    assert seq % CHUNK == 0 and s5_a_re.shape[1:] == (e // S5_GROUP, S5_STATE)
    assert fox_in_proj.shape[2] == 4 * e + heads and out_proj.shape[1:] == (e, d)

    tm = _pick(m, 1024)
    tn_in = _pick(2 * e, 1024)
    tm_out = _pick(m, 512)
    tm_glu = _pick(m, 512)
    tq = _pick(seq, 512)

    h = x.reshape(m, d)
    for i in range(depth):
        kind, j = i % n_mixers, i // n_mixers
        if kind == 0:
            s5_params = _s5_tile_params(s5_a_re[j], s5_a_im[j], s5_log_dt[j], s5_b_re[j], s5_b_im[j],
                                        s5_c_re[j], s5_c_im[j], s5_d[j])
            proj, w_out = _norm_proj(h, norm_w[i], s5_in_proj, j, out_proj, i, n_out=2 * e, tm=tm, tn=tn_in)
            g = _s5_core(proj, *s5_params, chunk=CHUNK, n_seq=bsz)
            y = _glu(g, s5_w_glu, j, s5_b_glu[j], tm=tm_glu)
            h = _out_proj(y, proj, 1, h, w_out, tm=tm_out)
        elif kind == 1:
            n_main = 4 * e
            b_f = jnp.pad(fox_f_bias[j], (0, LANES - heads)).reshape(1, LANES)
            proj, w_out, f_logit = _norm_proj(h, norm_w[i], jnp.swapaxes(fox_in_proj, 1, 2), j, out_proj, i,
                                              n_out=n_main, tm=tm, tn=tn_in, w_transposed=True, extra_bias=b_f)
            cum = _fox_cum(f_logit.reshape(bsz, seq, LANES), blk=_pick(seq, 256))
            cum_k = cum[:, :, :heads].transpose(0, 2, 1).reshape(bsz, heads, seq // tq, tq)
            y = _fox_attn(proj.reshape(bsz, seq, n_main), cum_k, fox_q_norm[j], fox_k_norm[j],
                          heads=heads, tq=tq)
            h = _out_proj(y.reshape(m, e), proj, 3, h, w_out, tm=tm_out)
        else:
            proj, w_out = _norm_proj(h, norm_w[i], pool_in_proj, j, out_proj, i, n_out=2 * e, tm=tm, tn=tn_in)
            y = _pool(proj.reshape(bsz, seq, 2 * e), pool_w_group, j, pool_scale[j], windows=POOL_WINDOWS)
            h = _out_proj(y.reshape(m, e), proj, 1, h, w_out, tm=tm_out)
    return h.reshape(bsz, seq, d)
```

```python
import functools

import jax
import jax.numpy as jnp
from jax import lax
from jax.experimental import pallas as pl
from jax.experimental.pallas import tpu as pltpu

F32 = jnp.float32
BF16 = jnp.bfloat16

LANES = 128
S5_GROUP = 16
S5_STATE = 64
TILE_GROUPS = LANES // S5_GROUP
TILE_STATES = TILE_GROUPS * S5_STATE
CHUNK = 8
FOX_HEAD_DIM = 128
POOL_WINDOWS = (2, 4, 8, 16)
POOL_HALO = 16
EPS = 1e-6
NEG_BIG = -1e30
VMEM_LIMIT = 56 * 1024 * 1024
N_X_PIECES = 4


def _params(semantics):
    return pltpu.CompilerParams(dimension_semantics=semantics, vmem_limit_bytes=VMEM_LIMIT)


def _rms_rows(x, w):
    ms = jnp.mean(x * x, axis=-1, keepdims=True)
    return x * lax.rsqrt(ms + EPS) * w


def _silu(z):
    return z * jax.nn.sigmoid(z)


def _norm_proj_body(*refs, w_transposed, n_extra, n_x):
    x_refs, (nw_ref, w_ref, wo_ref), rest = refs[:n_x], refs[n_x:n_x + 3], refs[n_x + 3:]
    if n_extra:
        wf_ref, bf_ref, o_ref, wo_bf_ref, of_ref, xn_ref = rest
    else:
        o_ref, wo_bf_ref, xn_ref = rest
    j = pl.program_id(1)
    wo_bf_ref[...] = wo_ref[...].astype(BF16)
    w_contract = 1 if w_transposed else 0

    def matmul(xn, w):
        return lax.dot_general(xn, w, (((1,), (w_contract,)), ((), ())), preferred_element_type=F32)

    def emit(xn):
        o_ref[...] = matmul(xn, w_ref[...].astype(BF16)).astype(BF16)

    @pl.when(j == 0)
    def _():
        x = jnp.concatenate([r[...] for r in x_refs], axis=1)
        xn = _rms_rows(x, nw_ref[...]).astype(BF16)
        xn_ref[...] = xn
        emit(xn)
        if n_extra:
            idx = lax.broadcasted_iota(jnp.int32, wf_ref.shape, 0 if w_transposed else 1)
            wf = jnp.where(idx < n_extra, wf_ref[...], 0.0).astype(BF16)
            of_ref[...] = matmul(xn, wf) + bf_ref[...]

    @pl.when(j > 0)
    def _():
        emit(xn_ref[...])


def _norm_proj(h2d, norm_w, w_all, layer, w_out_all, layer_out, *, n_out, tm, tn, w_transposed=False,
               extra_bias=None):
    m, d = h2d.shape
    nj = n_out // tn
    e_out, d_out = w_out_all.shape[1:]
    wo_rows = e_out // ((m // tm) * nj)
    if w_transposed:
        w_spec = pl.BlockSpec((None, tn, d), lambda i, j: (layer, j, 0))
        wf_spec = pl.BlockSpec((None, LANES, d), lambda i, j: (layer, n_out // LANES, 0))
        n_extra = w_all.shape[1] - n_out
    else:
        w_spec = pl.BlockSpec((None, d, tn), lambda i, j: (layer, 0, j))
        wf_spec = pl.BlockSpec((None, d, LANES), lambda i, j: (layer, 0, n_out // LANES))
        n_extra = w_all.shape[2] - n_out
    ni = m // tm
    n_x = min(N_X_PIECES, nj)

    def x_piece(q):
        return pl.BlockSpec((tm, d // n_x), lambda i, j: (jnp.minimum(i + jnp.where(j > q, 1, 0), ni - 1), q))

    in_specs = [x_piece(q) for q in range(n_x)]
    in_specs += [pl.BlockSpec((1, d), lambda i, j: (0, 0)),
                w_spec,
                pl.BlockSpec((None, wo_rows, d_out), lambda i, j: (layer_out, i * nj + j, 0))]
    out_specs = [pl.BlockSpec((tm, tn), lambda i, j: (i, j)),
                 pl.BlockSpec((wo_rows, d_out), lambda i, j: (i * nj + j, 0))]
    out_shape = [jax.ShapeDtypeStruct((m, n_out), BF16),
                 jax.ShapeDtypeStruct((e_out, d_out), BF16)]
    args = [h2d] * n_x + [norm_w.reshape(1, d), w_all, w_out_all]
    if extra_bias is None:
        n_extra = 0
    else:
        in_specs += [wf_spec, pl.BlockSpec((1, LANES), lambda i, j: (0, 0))]
        out_specs += [pl.BlockSpec((tm, LANES), lambda i, j: (i, 0))]
        out_shape += [jax.ShapeDtypeStruct((m, LANES), F32)]
        args += [w_all, extra_bias]
    outs = pl.pallas_call(
        functools.partial(_norm_proj_body, w_transposed=w_transposed, n_extra=n_extra, n_x=n_x),
        grid=(m // tm, n_out // tn),
        in_specs=in_specs,
        out_specs=out_specs,
        out_shape=out_shape,
        scratch_shapes=[pltpu.VMEM((tm, d), BF16)],
        compiler_params=_params(("arbitrary", "arbitrary")),
        name="norm_proj",
    )(*args)
    return outs


def _out_proj_body(y_ref, z_ref, h_ref, w_ref, o_ref):
    a = (y_ref[...].astype(F32) * _silu(z_ref[...].astype(F32))).astype(BF16)
    o_ref[...] = h_ref[...] + jnp.dot(a, w_ref[...], preferred_element_type=F32)


def _out_proj(y2d, z2d, z_block, h2d, w_bf16, *, tm):
    m, e = y2d.shape
    d = w_bf16.shape[1]
    return pl.pallas_call(
        _out_proj_body,
        grid=(m // tm,),
        in_specs=[pl.BlockSpec((tm, e), lambda i: (i, 0)),
                  pl.BlockSpec((tm, e), lambda i: (i, z_block)),
                  pl.BlockSpec((tm, d), lambda i: (i, 0)),
                  pl.BlockSpec((e, d), lambda i: (0, 0), pipeline_mode=pl.Buffered(1))],
        out_specs=pl.BlockSpec((tm, d), lambda i: (i, 0)),
        out_shape=jax.ShapeDtypeStruct((m, d), F32),
        compiler_params=_params(("parallel",)),
        name="out_proj",
    )(y2d, z2d, h2d, w_bf16)


def _s5_operators(a_ref, bt_ref, c_ref, d_ref, wb_ref, wct_ref, toe_ref, lam_ref, *, chunk):
    ns = TILE_STATES
    n_lt = ns // LANES

    def block_diag(x):
        x2 = jnp.concatenate([x, x], axis=1)
        grp_row = lax.broadcasted_iota(jnp.int32, (LANES, LANES), 0) // S5_GROUP
        grp_lane = lax.broadcasted_iota(jnp.int32, (LANES, LANES), 1) // S5_STATE
        per_tile = LANES // S5_STATE
        return jnp.concatenate([jnp.where(grp_row == grp_lane + per_tile * t, x2, 0.0)
                                for t in range(n_lt)], axis=1)

    ar, ai, ldt = a_ref[0:1, :], a_ref[1:2, :], a_ref[2:3, :]
    dt = jnp.exp(ldt)
    k_rows = lax.broadcasted_iota(jnp.int32, (chunk + 1, 1), 0).astype(F32)
    mag = jnp.exp(ar * dt * k_rows)
    ang = ai * dt * k_rows
    pr, pi = mag * jnp.cos(ang), mag * jnp.sin(ang)
    abar_r, abar_i = pr[1:2, :], pi[1:2, :]
    den = ar * ar + ai * ai
    xr = abar_r - 1.0
    fr = (xr * ar + abar_i * ai) / den
    fi = (abar_i * ar - xr * ai) / den
    bt_re, bt_im = block_diag(bt_ref[0]), block_diag(bt_ref[1])
    bbar_re = fr * bt_re - fi * bt_im
    bbar_im = fr * bt_im + fi * bt_re
    c_re, c_im = block_diag(c_ref[0]), block_diag(c_ref[1])
    lam_ref[:, 0:ns] = pr[chunk:chunk + 1, :]
    lam_ref[:, ns:2 * ns] = pi[chunk:chunk + 1, :]

    bb_re, bb_im = bbar_re.astype(BF16), bbar_im.astype(BF16)
    cb_re, cb_im = c_re.astype(BF16), c_im.astype(BF16)
    for tau in range(chunk):
        qr, qi = pr[tau:tau + 1, :].astype(BF16), pi[tau:tau + 1, :].astype(BF16)
        t = chunk - 1 - tau
        wb_ref[t * LANES:(t + 1) * LANES, 0:ns] = bb_re * qr - bb_im * qi
        wb_ref[t * LANES:(t + 1) * LANES, ns:2 * ns] = bb_re * qi + bb_im * qr
        qr, qi = pr[tau + 1:tau + 2, :].astype(BF16), pi[tau + 1:tau + 2, :].astype(BF16)
        wct_ref[tau * LANES:(tau + 1) * LANES, 0:ns] = cb_re * qr - cb_im * qi
        wct_ref[tau * LANES:(tau + 1) * LANES, ns:2 * ns] = -(cb_re * qi + cb_im * qr)

    def lag_operators():
        c_cat = jnp.concatenate([cb_re, -cb_im], axis=1)
        k_all = lax.dot_general(wb_ref[...], c_cat, (((1,), (1,)), ((), ())),
                                preferred_element_type=F32)
        zero_blk = jnp.zeros((LANES, LANES), BF16)
        eye = (lax.broadcasted_iota(jnp.int32, (LANES, LANES), 0)
               == lax.broadcasted_iota(jnp.int32, (LANES, LANES), 1))
        for tau in range(chunk):
            t = chunk - 1 - tau
            k_tau = k_all[t * LANES:(t + 1) * LANES, :]
            if tau == 0:
                k_tau = k_tau + jnp.where(eye, d_ref[...], 0.0)
            k_tau = k_tau.astype(BF16)
            toe_ref[0:LANES, tau * LANES:(tau + 1) * LANES] = k_tau
            if tau + 1 < chunk:
                toe_ref[LANES:2 * LANES, (tau + 1) * LANES:(tau + 2) * LANES] = k_tau
        toe_ref[LANES:2 * LANES, 0:LANES] = zero_blk

    return lag_operators


def _s5_tile_params(a_re, a_im, log_dt, b_re, b_im, c_re, c_im, d_skip):
    g, p = a_re.shape
    nt = g // TILE_GROUPS
    ns = TILE_STATES
    ldt = jnp.broadcast_to(log_dt[:, None], (g, p))
    a_rows = jnp.stack([a_re.reshape(nt, ns), a_im.reshape(nt, ns), ldt.reshape(nt, ns)], axis=1)
    bt = jnp.stack([b_re, b_im], axis=0).transpose(1, 0, 3, 2).reshape(nt, TILE_GROUPS, 2, S5_GROUP, p)
    bt = bt.transpose(0, 2, 1, 3, 4).reshape(nt, 2, LANES, p)
    cc = jnp.stack([c_re.reshape(nt, LANES, p), c_im.reshape(nt, LANES, p)], axis=1)
    return a_rows, bt, cc, d_skip.reshape(1, g * S5_GROUP)


def _s5_core_body(ubf_ref, a0_ref, bt0_ref, c0_ref, d0_ref, a1_ref, bt1_ref, c1_ref, d1_ref, o_ref,
                  u_ref, ub_ref, slab_ref, hin_ref, hprev_ref, *op_refs, chunk, n_seq, n_chunks):
    ns = TILE_STATES
    nk = ns // LANES
    seq = n_chunks * chunk
    step = pl.program_id(0)

    def token_rows(b, t):
        return pl.ds(b * seq + t, n_chunks, stride=chunk)

    def chunk_rows(b):
        return pl.ds(b, n_chunks, stride=n_seq)

    def operators(slot):
        return op_refs[4 * slot:4 * slot + 4]

    def run_tile(slot):
        wb_ref, wct_ref, toe_ref, lam_ref = operators(slot)
        next_lag_operators = _s5_operators(a1_ref, bt1_ref, c1_ref, d1_ref, *operators(1 - slot), chunk=chunk)
        u_ref[...] = ubf_ref[...].astype(F32)
        for t in range(chunk):
            for b in range(n_seq):
                slab_ref[t, chunk_rows(b), :] = u_ref[token_rows(b, t), :]
            ub_ref[:, t * LANES:(t + 1) * LANES] = slab_ref[t].astype(BF16)

        h_in = jnp.dot(ub_ref[...], wb_ref[...], preferred_element_type=F32)
        for k in range(2 * nk):
            hin_ref[k] = h_in[:, k * LANES:(k + 1) * LANES]

        for a in range(chunk // 2):
            lo = 2 * a * LANES
            width = (chunk - 2 * a) * LANES
            part = jnp.dot(ub_ref[:, lo:lo + 2 * LANES], toe_ref[:, 0:width], preferred_element_type=F32)
            for i in range(chunk - 2 * a):
                if a == 0:
                    slab_ref[i] = part[:, i * LANES:(i + 1) * LANES]
                else:
                    slab_ref[2 * a + i] += part[:, i * LANES:(i + 1) * LANES]

        lam = [jnp.broadcast_to(lam_ref[:, k * LANES:(k + 1) * LANES], (n_seq, LANES)) for k in range(2 * nk)]
        state = [jnp.zeros((n_seq, LANES), F32) for _ in range(2 * nk)]
        for c in range(n_chunks):
            rows = pl.ds(c * n_seq, n_seq)
            new_state = []
            for k in range(nk):
                hr, hi = state[k], state[nk + k]
                hprev_ref[k, rows, :] = hr
                hprev_ref[nk + k, rows, :] = hi
                lr, li = lam[k], lam[nk + k]
                new_state.append((lr * hr - li * hi + hin_ref[k, rows, :],
                                  lr * hi + li * hr + hin_ref[nk + k, rows, :]))
            state = [s[0] for s in new_state] + [s[1] for s in new_state]

        h_prev = jnp.concatenate([hprev_ref[k].astype(BF16) for k in range(2 * nk)], axis=1)
        for a in range(chunk // 2):
            lo = 2 * a * LANES
            y = lax.dot_general(h_prev, wct_ref[lo:lo + 2 * LANES, :], (((1,), (1,)), ((), ())),
                                preferred_element_type=F32)
            for i in range(2):
                t = 2 * a + i
                slab_ref[t] += y[:, i * LANES:(i + 1) * LANES]
                for b in range(n_seq):
                    o_ref[token_rows(b, t), :] = slab_ref[t, chunk_rows(b), :]
        next_lag_operators()

    @pl.when(step == 0)
    def _():
        _s5_operators(a0_ref, bt0_ref, c0_ref, d0_ref, *operators(0), chunk=chunk)()

    for parity in range(2):
        pl.when(step % 2 == parity)(functools.partial(run_tile, parity))


def _s5_core(proj2d, a_rows, bt, cc, d_row, *, chunk, n_seq):
    m, n2 = proj2d.shape
    e = n2 // 2
    r = m // chunk
    nt = e // LANES
    ns = TILE_STATES
    p = bt.shape[-1]
    kw = chunk * LANES

    def param_specs(tile):
        return [pl.BlockSpec((None, 3, ns), lambda j: (tile(j), 0, 0)),
                pl.BlockSpec((None, 2, LANES, p), lambda j: (tile(j), 0, 0, 0)),
                pl.BlockSpec((None, 2, LANES, p), lambda j: (tile(j), 0, 0, 0)),
                pl.BlockSpec((1, LANES), lambda j: (0, tile(j)))]

    params = [a_rows, bt, cc, d_row]
    return pl.pallas_call(
        functools.partial(_s5_core_body, chunk=chunk, n_seq=n_seq, n_chunks=r // n_seq),
        grid=(nt,),
        in_specs=([pl.BlockSpec((m, LANES), lambda j: (0, j))]
                  + param_specs(lambda j: j) + param_specs(lambda j: jnp.minimum(j + 1, nt - 1))),
        out_specs=pl.BlockSpec((m, LANES), lambda j: (0, j)),
        out_shape=jax.ShapeDtypeStruct((m, e), F32),
        scratch_shapes=[pltpu.VMEM((m, LANES), F32),
                        pltpu.VMEM((r, kw), BF16),
                        pltpu.VMEM((chunk, r, LANES), F32),
                        pltpu.VMEM((2 * ns // LANES, r, LANES), F32),
                        pltpu.VMEM((2 * ns // LANES, r, LANES), F32),
                        *([pltpu.VMEM((kw, 2 * ns), BF16),
                           pltpu.VMEM((kw, 2 * ns), BF16),
                           pltpu.VMEM((2 * LANES, kw), BF16),
                           pltpu.VMEM((1, 2 * ns), F32)] * 2)],
        compiler_params=_params(("arbitrary",)),
        name="s5_core",
    )(proj2d, *params, *params)


def _glu_body(y_ref, w_ref, b_ref, o_ref, wbf_ref):
    @pl.when(pl.program_id(0) == 0)
    def _():
        wbf_ref[...] = w_ref[...].astype(BF16)

    g = jax.nn.gelu(y_ref[...])
    acc = jnp.dot(g.astype(BF16), wbf_ref[...], preferred_element_type=F32) + b_ref[...]
    o_ref[...] = (g * jax.nn.sigmoid(acc)).astype(BF16)


def _glu(y2d, w_all, layer, b, *, tm):
    m, e = y2d.shape
    return pl.pallas_call(
        _glu_body,
        grid=(m // tm,),
        in_specs=[pl.BlockSpec((tm, e), lambda i: (i, 0)),
                  pl.BlockSpec((None, e, e), lambda i: (layer, 0, 0), pipeline_mode=pl.Buffered(1)),
                  pl.BlockSpec((1, e), lambda i: (0, 0))],
        out_specs=pl.BlockSpec((tm, e), lambda i: (i, 0)),
        out_shape=jax.ShapeDtypeStruct((m, e), BF16),
        scratch_shapes=[pltpu.VMEM((e, e), BF16)],
        compiler_params=_params(("arbitrary",)),
        name="s5_glu",
    )(y2d, w_all, b.reshape(1, e))


def _fox_cum_body(f_ref, o_ref, *, blk):
    seq = f_ref.shape[0]
    row = lax.broadcasted_iota(jnp.int32, (blk, blk), 0)
    col = lax.broadcasted_iota(jnp.int32, (blk, blk), 1)
    tri = (col <= row).astype(BF16)
    carry = jnp.zeros((1, LANES), F32)
    for i in range(seq // blk):
        x = f_ref[i * blk:(i + 1) * blk, :]
        ls = jnp.minimum(x, 0.0) - jnp.log1p(jnp.exp(-jnp.abs(x)))
        cum = carry
        rest = ls
        for _ in range(3):
            part = rest.astype(BF16)
            cum = cum + jnp.dot(tri, part, preferred_element_type=F32)
            rest = rest - part.astype(F32)
        o_ref[i * blk:(i + 1) * blk, :] = cum
        carry = cum[blk - 1:blk, :]


def _fox_cum(f3d, *, blk):
    b, seq, _ = f3d.shape
    return pl.pallas_call(
        functools.partial(_fox_cum_body, blk=blk),
        grid=(b,),
        in_specs=[pl.BlockSpec((None, seq, LANES), lambda i: (i, 0, 0))],
        out_specs=pl.BlockSpec((None, seq, LANES), lambda i: (i, 0, 0)),
        out_shape=jax.ShapeDtypeStruct((b, seq, LANES), F32),
        compiler_params=_params(("parallel",)),
        name="fox_cum",
    )(f3d)


def _fox_attn_body(q_ref, k_ref, v_ref, ck_ref, qw_ref, kw_ref, o_ref,
                   kn_ref, m_ref, acc_ref, *, tq, scale):
    seq = k_ref.shape[0]
    nq = seq // tq
    log2e = 1.4426950408889634

    def q_block(i):
        rows = slice(i * tq, (i + 1) * tq)
        return (_rms_rows(q_ref[rows, :].astype(F32), qw_ref[...]) * (scale * log2e)).astype(BF16)

    def k_block(j):
        rows = slice(j * tq, (j + 1) * tq)
        kn_ref[rows, :] = _rms_rows(k_ref[rows, :].astype(F32), kw_ref[...]).astype(BF16)

    pairs = [(i, j) for i in range(nq) for j in range(i + 1)]
    qn = {}

    def logits(i, j):
        if j == 0:
            qn[i] = q_block(i)
        if j == i:
            k_block(j)
        return lax.dot_general(qn[i], kn_ref[j * tq:(j + 1) * tq, :],
                               (((1,), (1,)), ((), ())), preferred_element_type=F32)

    n_lt = tq // LANES
    ones_blk = jnp.ones((tq, LANES), BF16)
    s_next = logits(*pairs[0])
    for idx, (i, j) in enumerate(pairs):
        s = s_next
        if idx + 1 < len(pairs):
            s_next = logits(*pairs[idx + 1])
        c0 = ck_ref[i:i + 1, 0:1]
        dk = (ck_ref[j:j + 1, :] - c0) * log2e
        s_t = [s[:, t * LANES:(t + 1) * LANES] - dk[:, t * LANES:(t + 1) * LANES] for t in range(n_lt)]
        if i == j:
            r_i = lax.broadcasted_iota(jnp.int32, (tq, LANES), 0)
            c_i = lax.broadcasted_iota(jnp.int32, (tq, LANES), 1)
            s_t = [jnp.where(c_i + t * LANES <= r_i, s_t[t], NEG_BIG) for t in range(n_lt)]
        mx = functools.reduce(jnp.maximum, s_t)
        row_max = jnp.broadcast_to(jnp.max(mx, axis=-1, keepdims=True), (tq, LANES))
        v_blk = v_ref[j * tq:(j + 1) * tq, :]
        if j == 0:
            m_new = row_max
        else:
            m_old = m_ref[...]
            m_new = jnp.maximum(m_old, row_max)
        p = jnp.concatenate([jnp.exp2(s_t[t] - m_new).astype(BF16) for t in range(n_lt)], axis=1)
        pv = jnp.dot(p, jnp.concatenate([v_blk, ones_blk], axis=1), preferred_element_type=F32)
        if j == 0:
            acc_new = pv
        else:
            alpha = jnp.exp2(m_old - m_new)
            acc_new = jnp.concatenate([alpha, alpha], axis=1) * acc_ref[...] + pv
        if j == i:
            o_ref[i * tq:(i + 1) * tq, :] = (acc_new[:, 0:LANES] / acc_new[:, LANES:2 * LANES]).astype(BF16)
        else:
            m_ref[...] = m_new
            acc_ref[...] = acc_new


def _fox_attn(proj3d, cum_k, q_w, k_w, *, heads, tq):
    b, seq, _ = proj3d.shape
    dh = FOX_HEAD_DIM
    nq = seq // tq
    return pl.pallas_call(
        functools.partial(_fox_attn_body, tq=tq, scale=dh ** -0.5),
        grid=(b, heads),
        in_specs=[pl.BlockSpec((None, seq, dh), lambda bi, h: (bi, 0, h)),
                  pl.BlockSpec((None, seq, dh), lambda bi, h: (bi, 0, heads + h)),
                  pl.BlockSpec((None, seq, dh), lambda bi, h: (bi, 0, 2 * heads + h)),
                  pl.BlockSpec((None, None, nq, tq), lambda bi, h: (bi, h, 0, 0)),
                  pl.BlockSpec((1, dh), lambda bi, h: (0, 0)),
                  pl.BlockSpec((1, dh), lambda bi, h: (0, 0))],
        out_specs=pl.BlockSpec((None, seq, dh), lambda bi, h: (bi, 0, h)),
        out_shape=jax.ShapeDtypeStruct((b, seq, heads * dh), BF16),
        scratch_shapes=[pltpu.VMEM((seq, dh), BF16),
                        pltpu.VMEM((tq, LANES), F32),
                        pltpu.VMEM((tq, dh + LANES), F32)],
        compiler_params=_params(("parallel", "parallel")),
        name="fox_attn",
    )(proj3d, proj3d, proj3d, cum_k, q_w.reshape(1, dh), k_w.reshape(1, dh))


def _pool_out_proj_body(u_ref, prev_ref, z_ref, h_ref, wg_ref, s_ref, w_ref, o_ref, sa_ref, sb_ref,
                        *, windows, blocks_per_seq):
    tm, e = u_ref.shape
    halo = POOL_HALO
    cg = e // len(windows)
    pos = lax.rem(pl.program_id(0), blocks_per_seq)
    t_idx = pos * tm + lax.broadcasted_iota(jnp.int32, (tm, 1), 0)
    for g, win in enumerate(windows):
        cols = slice(g * cg, (g + 1) * cg)
        u = u_ref[:, cols].astype(F32)
        sa_ref[0:halo, :] = jnp.where(pos == 0, 0.0, prev_ref[:, cols].astype(F32))
        sa_ref[halo:halo + tm, :] = u
        bufs = (sa_ref, sb_ref)
        shift, level, lo = 1, 0, 0
        while True:
            src = bufs[level % 2]
            lo += shift
            if 2 * shift >= win:
                s = src[halo:halo + tm, :] + src[halo - shift:halo - shift + tm, :]
                break
            bufs[(level + 1) % 2][lo:halo + tm, :] = src[lo:halo + tm, :] + src[lo - shift:halo + tm - shift, :]
            shift *= 2
            level += 1
        cnt = jnp.minimum(t_idx + 1, win).astype(F32)
        diff = (s / cnt - u).astype(BF16)
        y = jnp.dot(diff, wg_ref[g].astype(BF16), preferred_element_type=F32) * s_ref[:, cols]
        a = (y * _silu(z_ref[:, cols].astype(F32))).astype(BF16)
        part = jnp.dot(a, w_ref[cols, :], preferred_element_type=F32)
        if g == 0:
            o_ref[...] = h_ref[...] + part
        else:
            o_ref[...] += part


def _pool_out_proj(proj2d, h2d, w_group_all, layer, scale, w_bf16, *, seq, windows, tm):
    m, n2 = proj2d.shape
    e = n2 // 2
    d = w_bf16.shape[1]
    ng = len(windows)
    cg = e // ng
    halo_blocks = tm // POOL_HALO
    return pl.pallas_call(
        functools.partial(_pool_out_proj_body, windows=windows, blocks_per_seq=seq // tm),
        grid=(m // tm,),
        in_specs=[pl.BlockSpec((tm, e), lambda i: (i, 0)),
                  pl.BlockSpec((POOL_HALO, e), lambda i: (jnp.maximum(i * halo_blocks - 1, 0), 0)),
                  pl.BlockSpec((tm, e), lambda i: (i, 1)),
                  pl.BlockSpec((tm, d), lambda i: (i, 0)),
                  pl.BlockSpec((None, ng, cg, cg), lambda i: (layer, 0, 0, 0), pipeline_mode=pl.Buffered(1)),
                  pl.BlockSpec((1, e), lambda i: (0, 0)),
                  pl.BlockSpec((e, d), lambda i: (0, 0), pipeline_mode=pl.Buffered(1))],
        out_specs=pl.BlockSpec((tm, d), lambda i: (i, 0)),
        out_shape=jax.ShapeDtypeStruct((m, d), F32),
        scratch_shapes=[pltpu.VMEM((POOL_HALO + tm, cg), F32)] * 2,
        compiler_params=_params(("parallel",)),
        name="pool_out_proj",
    )(proj2d, proj2d, proj2d, h2d, w_group_all, scale.reshape(1, e), w_bf16)


def _pick(n, pref):
    t = min(n, pref)
    while n % t:
        t //= 2
    return t


def kernel(x, norm_w, out_proj, s5_in_proj, s5_a_re, s5_a_im, s5_log_dt, s5_b_re, s5_b_im, s5_c_re, s5_c_im, s5_d, s5_w_glu, s5_b_glu, fox_in_proj, fox_q_norm, fox_k_norm, fox_f_bias, pool_in_proj, pool_w_group, pool_scale):
    bsz, seq, d = x.shape
    depth = norm_w.shape[0]
    e = s5_d.shape[1]
    heads = e // FOX_HEAD_DIM
    m = bsz * seq
    n_mixers = 3

    assert e % (LANES * len(POOL_WINDOWS)) == 0 and e % FOX_HEAD_DIM == 0 and heads <= LANES
    assert seq % CHUNK == 0 and s5_a_re.shape[1:] == (e // S5_GROUP, S5_STATE)
    assert fox_in_proj.shape[2] == 4 * e + heads and out_proj.shape[1:] == (e, d)

    tm = _pick(m, 1024)
    tn_in = _pick(2 * e, 1024)
    tm_out = _pick(m, 512)
    tm_glu = _pick(m, 512)
    tq = _pick(seq, 512)

    h = x.reshape(m, d)
    for i in range(depth):
        kind, j = i % n_mixers, i // n_mixers
        if kind == 0:
            s5_params = _s5_tile_params(s5_a_re[j], s5_a_im[j], s5_log_dt[j], s5_b_re[j], s5_b_im[j],
                                        s5_c_re[j], s5_c_im[j], s5_d[j])
            proj, w_out = _norm_proj(h, norm_w[i], s5_in_proj, j, out_proj, i, n_out=2 * e, tm=tm, tn=tn_in)
            g = _s5_core(proj, *s5_params, chunk=CHUNK, n_seq=bsz)
            y = _glu(g, s5_w_glu, j, s5_b_glu[j], tm=tm_glu)
            h = _out_proj(y, proj, 1, h, w_out, tm=tm_out)
        elif kind == 1:
            n_main = 4 * e
            b_f = jnp.pad(fox_f_bias[j], (0, LANES - heads)).reshape(1, LANES)
            proj, w_out, f_logit = _norm_proj(h, norm_w[i], jnp.swapaxes(fox_in_proj, 1, 2), j, out_proj, i,
                                              n_out=n_main, tm=tm, tn=tn_in, w_transposed=True, extra_bias=b_f)
            cum = _fox_cum(f_logit.reshape(bsz, seq, LANES), blk=_pick(seq, 256))
            cum_k = cum[:, :, :heads].transpose(0, 2, 1).reshape(bsz, heads, seq // tq, tq)
            y = _fox_attn(proj.reshape(bsz, seq, n_main), cum_k, fox_q_norm[j], fox_k_norm[j],
                          heads=heads, tq=tq)
            h = _out_proj(y.reshape(m, e), proj, 3, h, w_out, tm=tm_out)
        else:
            proj, w_out = _norm_proj(h, norm_w[i], pool_in_proj, j, out_proj, i, n_out=2 * e, tm=tm, tn=tn_in)
            h = _pool_out_proj(proj, h, pool_w_group, j, pool_scale[j], w_out, seq=seq, windows=POOL_WINDOWS,
                               tm=_pick(seq, 256))
    return h.reshape(bsz, seq, d)
```
